```python
import math
import jax
import jax.numpy as jnp
from jax import lax
import numpy as np

D_MODEL = 2048
BATCH = 2
SEQ = 16384
DEPTH = 2

GRID_W = 64
CTX_LEN = 256
NORM_EPS = 1e-6
ROPE_THETA = 10000.0
Q_BLOCK = 128
N_MOD = 6

LRU_WIDTH = 1024
LRU_BLOCKS = 8
LRU_BLOCK = LRU_WIDTH // LRU_BLOCKS
CONV_W = 4
LRU_C = 8.0

DIFF_HEADS = 8
DIFF_HEAD_DIM = 64
DIFF_V_DIM = 2 * DIFF_HEAD_DIM
DIFF_QK = DIFF_HEADS * 2 * DIFF_HEAD_DIM

HGRN_HEADS = 8
HGRN_DK = 128
HGRN_DV = 128
HGRN_WIDTH = HGRN_HEADS * HGRN_DK
CHUNK = 64

GQA_HEADS = 8
GQA_KV_HEADS = 2
GQA_REP = GQA_HEADS // GQA_KV_HEADS
GQA_HEAD_DIM = 128

FFN_HIDDEN = -(-(8 * D_MODEL) // (3 * 256)) * 256

AB_IN = 2 * LRU_WIDTH + 2 * DIFF_QK + DIFF_HEADS * DIFF_V_DIM
AB_OUT = LRU_WIDTH + DIFF_HEADS * DIFF_V_DIM
AB_SPLITS = (LRU_WIDTH, 2 * LRU_WIDTH, 2 * LRU_WIDTH + DIFF_QK, 2 * LRU_WIDTH + 2 * DIFF_QK)
CD_IN = 5 * HGRN_WIDTH + GQA_HEADS * GQA_HEAD_DIM + 2 * GQA_KV_HEADS * GQA_HEAD_DIM
CD_OUT = HGRN_WIDTH + GQA_HEADS * GQA_HEAD_DIM
CD_SPLITS = (HGRN_WIDTH, 2 * HGRN_WIDTH, 3 * HGRN_WIDTH, 4 * HGRN_WIDTH, 5 * HGRN_WIDTH,
             5 * HGRN_WIDTH + GQA_HEADS * GQA_HEAD_DIM,
             5 * HGRN_WIDTH + (GQA_HEADS + GQA_KV_HEADS) * GQA_HEAD_DIM)
N_EVEN = (DEPTH + 1) // 2
N_ODD = DEPTH // 2

kernel_name = 'hybrid_rglru_diffattn_hgrn2_gqa_dit'


def _rms_norm(x, w):
    xf = x.astype(jnp.float32)
    y = xf * lax.rsqrt(jnp.mean(xf * xf, axis=-1, keepdims=True) + NORM_EPS)
    return (y * w.astype(jnp.float32)).astype(x.dtype)


def _modulate(x, shift, scale):
    return x * (1.0 + scale) + shift


def _axial_rope(rows, head_dim):
    row = jnp.repeat(jnp.arange(rows, dtype=jnp.float32), GRID_W)
    col = jnp.tile(jnp.arange(GRID_W, dtype=jnp.float32), rows)
    n_freq = head_dim // 4
    inv = ROPE_THETA ** (-jnp.arange(n_freq, dtype=jnp.float32) / n_freq)
    ang = jnp.concatenate([row[:, None] * inv, col[:, None] * inv], axis=-1)
    return jnp.cos(ang), jnp.sin(ang)


def _apply_rope(x, cos, sin):
    shape = (cos.shape[0],) + (1,) * (x.ndim - 3) + (cos.shape[1],)
    cs = cos.reshape(shape).astype(x.dtype)
    sn = sin.reshape(shape).astype(x.dtype)
    x1, x2 = jnp.split(x, 2, axis=-1)
    return jnp.concatenate([x1 * cs - x2 * sn, x2 * cs + x1 * sn], axis=-1)


def _centred_dwconv(x, w, b):
    pad_l = (CONV_W - 1) // 2
    xp = jnp.pad(x, ((0, 0), (pad_l, CONV_W - 1 - pad_l), (0, 0)))
    T = x.shape[1]
    y = xp[:, 0:T] * w[0]
    for k in range(1, CONV_W):
        y = y + xp[:, k:k + T] * w[k]
    return y + b


def _query_blocks(fn, q):
    B, T = q.shape[:2]
    nb = T // Q_BLOCK
    qb = q.reshape((B, nb, Q_BLOCK) + q.shape[2:]).swapaxes(0, 1)
    ob = lax.map(fn, qb)
    return ob.swapaxes(0, 1).reshape((B, T) + ob.shape[3:])


def _rglru_gates(x, wa, ba, wx, bx, lam):
    B, T, _ = x.shape
    xb = x.reshape(B, T, LRU_BLOCKS, LRU_BLOCK)
    r = jax.nn.sigmoid((jnp.einsum('bthi,hij->bthj', xb, wa).reshape(B, T, LRU_WIDTH) + ba).astype(jnp.float32))
    i = jax.nn.sigmoid((jnp.einsum('bthi,hij->bthj', xb, wx).reshape(B, T, LRU_WIDTH) + bx).astype(jnp.float32))
    log_a = -LRU_C * r * jax.nn.softplus(-lam.astype(jnp.float32))
    a = jnp.exp(log_a)
    b = jnp.sqrt(-jnp.expm1(2.0 * log_a)) * (i * x.astype(jnp.float32))
    return a, b


def _linear_scan(a, b, h0):
    b = b.at[:, 0].add(a[:, 0] * h0)
    def combine(lft, rgt):
        return (lft[0] * rgt[0], rgt[0] * lft[1] + rgt[1])
    _, h = lax.associative_scan(combine, (a, b), axis=1)
    return h


def _rglru_bidir(x_lat, x_ctx, wa, ba, wx, bx, lam):
    outs_l, outs_c = [], []
    for d in range(2):
        flip = d == 1
        xc = x_ctx[:, ::-1] if flip else x_ctx
        xl = x_lat[:, ::-1] if flip else x_lat
        a_c, b_c = _rglru_gates(xc, wa[d], ba[d], wx[d], bx[d], lam[d])
        h_c = _linear_scan(a_c, b_c, jnp.zeros_like(b_c[:, 0]))
        a_l, b_l = _rglru_gates(xl, wa[d], ba[d], wx[d], bx[d], lam[d])
        h_l = _linear_scan(a_l, b_l, h_c[:, -1])
        outs_c.append(h_c[:, ::-1] if flip else h_c)
        outs_l.append(h_l[:, ::-1] if flip else h_l)
    return (outs_l[0] + outs_l[1]).astype(x_lat.dtype), (outs_c[0] + outs_c[1]).astype(x_ctx.dtype)


def _diff_attend(qb, kk, vv, lam):
    s = jnp.einsum('bqhmd,bkhmd->bhmqk', qb, kk).astype(jnp.float32) * (DIFF_HEAD_DIM ** -0.5)
    p = jax.nn.softmax(s, axis=-1)
    pd = p[:, :, 0] - lam * p[:, :, 1]
    return jnp.einsum('bhqk,bkhe->bqhe', pd.astype(vv.dtype), vv)


def _diff_post(o, w, lambda_init):
    B, T = o.shape[:2]
    return (_rms_norm(o, w) * (1.0 - lambda_init)).reshape(B, T, -1)


def _mixer_ab(h_lat, h_ctx, cos, sin, w_in, w_out, conv_w, conv_b, wa, ba, wx, bx, lam_param,
              lq1, lk1, lq2, lk2, subln_w, lambda_init, with_ctx_out):
    def project(h):
        B, T, _ = h.shape
        g, xr, q, k, v = jnp.split(h @ w_in, AB_SPLITS, axis=-1)
        q = q.reshape(B, T, DIFF_HEADS, 2, DIFF_HEAD_DIM)
        k = k.reshape(B, T, DIFF_HEADS, 2, DIFF_HEAD_DIM)
        v = v.reshape(B, T, DIFF_HEADS, DIFF_V_DIM)
        return g, _centred_dwconv(xr, conv_w, conv_b), q, k, v

    g_l, x_l, q_l, k_l, v_l = project(h_lat)
    g_c, x_c, q_c, k_c, v_c = project(h_ctx)
    q_l = _apply_rope(q_l, cos, sin)
    k_l = _apply_rope(k_l, cos, sin)
    r_l, r_c = _rglru_bidir(x_l, x_c, wa, ba, wx, bx, lam_param)
    a_l = r_l * jax.nn.gelu(g_l)
    f32 = jnp.float32
    lam = (jnp.exp(jnp.sum(lq1.astype(f32) * lk1.astype(f32)))
           - jnp.exp(jnp.sum(lq2.astype(f32) * lk2.astype(f32))) + lambda_init)
    k_all = jnp.concatenate([k_l, k_c], axis=1)
    v_all = jnp.concatenate([v_l, v_c], axis=1)
    d_l = _diff_post(_query_blocks(lambda qb: _diff_attend(qb, k_all, v_all, lam), q_l), subln_w, lambda_init)
    y_lat = jnp.concatenate([a_l, d_l], axis=-1) @ w_out
    y_ctx = None
    if with_ctx_out:
        a_c = r_c * jax.nn.gelu(g_c)
        d_c = _diff_post(_diff_attend(q_c, k_c, v_c, lam), subln_w, lambda_init)
        y_ctx = jnp.concatenate([a_c, d_c], axis=-1) @ w_out
    return y_lat, y_ctx


def _gla_chunked(q, k, v, log_f, s0):
    B, T, H, dk = q.shape
    nc = T // CHUNK
    def to_chunks(t):
        return t.reshape(B, nc, CHUNK, H, t.shape[-1]).transpose(1, 0, 3, 2, 4)
    tril = jnp.tril(jnp.ones((CHUNK, CHUNK), dtype=bool))[:, :, None]

    def body(S, inp):
        qc, kc, vc, gc = inp
        b = jnp.cumsum(gc, axis=2)
        o_inter = jnp.einsum('bhtk,bhkv->bhtv', qc * jnp.exp(b), S)
        rel = b[:, :, :, None, :] - b[:, :, None, :, :]
        decay = jnp.exp(jnp.where(tril, rel, -jnp.inf))
        scores = jnp.einsum('bhtk,bhtsk,bhsk->bhts', qc, decay, kc)
        o_intra = jnp.einsum('bhts,bhsv->bhtv', scores, vc)
        b_last = b[:, :, -1:, :]
        S_new = (jnp.exp(b_last)[:, :, 0, :, None] * S
                 + jnp.einsum('bhsk,bhsv->bhkv', kc * jnp.exp(b_last - b), vc))
        return S_new, o_inter + o_intra

    S_fin, o = lax.scan(body, s0, (to_chunks(q), to_chunks(k), to_chunks(v), to_chunks(log_f)))
    return o.transpose(1, 0, 3, 2, 4).reshape(B, T, H, v.shape[-1]), S_fin


def _hgrn2_prep(q, f_raw, i, lb, flip):
    B, T, _ = q.shape
    f = lb + (1.0 - lb) * jax.nn.sigmoid(f_raw.astype(jnp.float32))
    heads = lambda t: t.reshape(B, T, HGRN_HEADS, -1)
    arrs = [heads(q.astype(jnp.float32)), heads(1.0 - f), heads(i.astype(jnp.float32)), heads(jnp.log(f))]
    if flip:
        arrs = [a[:, ::-1] for a in arrs]
    return arrs


def _hgrn2_bidir(q_l, f_l, i_l, q_c, f_c, i_c, lb):
    outs_l, outs_c = [], []
    B = q_c.shape[0]
    for d in range(2):
        flip = d == 1
        s0 = jnp.zeros((B, HGRN_HEADS, HGRN_DK, HGRN_DV), jnp.float32)
        o_c, s_c = _gla_chunked(*_hgrn2_prep(q_c, f_c[d], i_c, lb[d], flip), s0)
        o_l, _ = _gla_chunked(*_hgrn2_prep(q_l, f_l[d], i_l, lb[d], flip), s_c)
        outs_c.append(o_c[:, ::-1] if flip else o_c)
        outs_l.append(o_l[:, ::-1] if flip else o_l)
    return outs_l[0] + outs_l[1], outs_c[0] + outs_c[1]


def _hgrn_post(o, g, w):
    B, T = g.shape[:2]
    return (_rms_norm(o, w).reshape(B, T, -1) * jax.nn.silu(g.astype(jnp.float32))).astype(g.dtype)


def _gqa_attend(qb, kk, vv):
    B, Q = qb.shape[:2]
    qg = qb.reshape(B, Q, GQA_KV_HEADS, GQA_REP, GQA_HEAD_DIM)
    s = jnp.einsum('bqgrd,bkgd->bgrqk', qg, kk).astype(jnp.float32) * (GQA_HEAD_DIM ** -0.5)
    p = jax.nn.softmax(s, axis=-1)
    o = jnp.einsum('bgrqk,bkgd->bqgrd', p.astype(vv.dtype), vv)
    return o.reshape(B, Q, GQA_HEADS * GQA_HEAD_DIM)


def _mixer_cd(h_lat, h_ctx, cos, sin, w_in, w_out, lb, hgrn_norm_w, q_norm_w, k_norm_w, with_ctx_out):
    def project(h):
        B, T, _ = h.shape
        qh, ff, fb, ih, gh, qa, ka, va = jnp.split(h @ w_in, CD_SPLITS, axis=-1)
        qa = _rms_norm(qa.reshape(B, T, GQA_HEADS, GQA_HEAD_DIM), q_norm_w)
        ka = _rms_norm(ka.reshape(B, T, GQA_KV_HEADS, GQA_HEAD_DIM), k_norm_w)
        va = va.reshape(B, T, GQA_KV_HEADS, GQA_HEAD_DIM)
        return jax.nn.silu(qh), (ff, fb), ih, gh, qa, ka, va

    qh_l, f_l, i_l, g_l, qa_l, ka_l, va_l = project(h_lat)
    qh_c, f_c, i_c, g_c, qa_c, ka_c, va_c = project(h_ctx)
    o_l, o_c = _hgrn2_bidir(qh_l, f_l, i_l, qh_c, f_c, i_c, lb)
    c_l = _hgrn_post(o_l, g_l, hgrn_norm_w)
    qa_l = _apply_rope(qa_l, cos, sin)
    ka_l = _apply_rope(ka_l, cos, sin)
    k_all = jnp.concatenate([ka_l, ka_c], axis=1)
    v_all = jnp.concatenate([va_l, va_c], axis=1)
    att_l = _query_blocks(lambda qb: _gqa_attend(qb, k_all, v_all), qa_l)
    y_lat = jnp.concatenate([c_l, att_l], axis=-1) @ w_out
    y_ctx = None
    if with_ctx_out:
        c_c = _hgrn_post(o_c, g_c, hgrn_norm_w)
        att_c = _gqa_attend(qa_c, ka_c, va_c)
        y_ctx = jnp.concatenate([c_c, att_c], axis=-1) @ w_out
    return y_lat, y_ctx


def _swiglu(h, w_gate, w_up, w_down):
    return (jax.nn.silu(h @ w_gate) * (h @ w_up)) @ w_down


def setup_inputs(seed: int = 0) -> dict:
    key = jax.random.key(seed)
    keys = list(jax.random.split(key, 40))
    f32 = jnp.float32

    def dense(shape, fan_in):
        return jax.random.normal(keys.pop(), shape, f32) * fan_in ** -0.5

    def gain(shape):
        return 1.0 + 0.02 * jax.random.normal(keys.pop(), shape, f32)

    def small(shape, s=0.02):
        return s * jax.random.normal(keys.pop(), shape, f32)

    a_base = jax.random.uniform(keys.pop(), (N_EVEN, 2, LRU_WIDTH), f32, 0.9, 0.999) ** (1.0 / LRU_C)
    lru_lambda = jnp.log(a_base) - jnp.log1p(-a_base)

    return {
        'x': jax.random.normal(keys.pop(), (BATCH, SEQ, D_MODEL), f32),
        'c': jax.random.normal(keys.pop(), (BATCH, D_MODEL), f32),
        'ctx': jax.random.normal(keys.pop(), (BATCH, CTX_LEN, D_MODEL), f32),
        'c_ctx': jax.random.normal(keys.pop(), (D_MODEL,), f32),
        'mod_w': dense((DEPTH, D_MODEL, N_MOD * D_MODEL), D_MODEL),
        'mod_b': small((DEPTH, N_MOD * D_MODEL)),
        'norm_mix_w': gain((DEPTH, D_MODEL)),
        'norm_ffn_w': gain((DEPTH, D_MODEL)),
        'ffn_w_gate': dense((DEPTH, D_MODEL, FFN_HIDDEN), D_MODEL),
        'ffn_w_up': dense((DEPTH, D_MODEL, FFN_HIDDEN), D_MODEL),
        'ffn_w_down': dense((DEPTH, FFN_HIDDEN, D_MODEL), FFN_HIDDEN),
        'ab_w_in': dense((N_EVEN, D_MODEL, AB_IN), D_MODEL),
        'ab_w_out': dense((N_EVEN, AB_OUT, D_MODEL), AB_OUT),
        'lru_conv_w': dense((N_EVEN, CONV_W, LRU_WIDTH), CONV_W),
        'lru_conv_b': small((N_EVEN, LRU_WIDTH)),
        'lru_wa': dense((N_EVEN, 2, LRU_BLOCKS, LRU_BLOCK, LRU_BLOCK), LRU_BLOCK),
        'lru_ba': small((N_EVEN, 2, LRU_WIDTH)),
        'lru_wx': dense((N_EVEN, 2, LRU_BLOCKS, LRU_BLOCK, LRU_BLOCK), LRU_BLOCK),
        'lru_bx': small((N_EVEN, 2, LRU_WIDTH)),
        'lru_lambda': lru_lambda,
        'diff_lq1': small((N_EVEN, DIFF_HEAD_DIM), 0.1),
        'diff_lk1': small((N_EVEN, DIFF_HEAD_DIM), 0.1),
        'diff_lq2': small((N_EVEN, DIFF_HEAD_DIM), 0.1),
        'diff_lk2': small((N_EVEN, DIFF_HEAD_DIM), 0.1),
        'diff_subln_w': gain((N_EVEN, DIFF_V_DIM)),
        'cd_w_in': dense((N_ODD, D_MODEL, CD_IN), D_MODEL),
        'cd_w_out': dense((N_ODD, CD_OUT, D_MODEL), CD_OUT),
        'hgrn_lb_logits': small((2, DEPTH, HGRN_WIDTH), 0.5),
        'hgrn_norm_w': gain((N_ODD, HGRN_DV)),
        'gqa_q_norm_w': gain((N_ODD, GQA_HEAD_DIM)),
        'gqa_k_norm_w': gain((N_ODD, GQA_HEAD_DIM)),
        'final_norm_w': gain((D_MODEL,)),
    }


def reference(x, c, ctx, c_ctx, mod_w, mod_b, norm_mix_w, norm_ffn_w, ffn_w_gate, ffn_w_up, ffn_w_down,
              ab_w_in, ab_w_out, lru_conv_w, lru_conv_b, lru_wa, lru_ba, lru_wx, lru_bx, lru_lambda,
              diff_lq1, diff_lk1, diff_lq2, diff_lk2, diff_subln_w,
              cd_w_in, cd_w_out, hgrn_lb_logits, hgrn_norm_w, gqa_q_norm_w, gqa_k_norm_w, final_norm_w):
    ROWS = x.shape[1] // GRID_W
    cos_b, sin_b = _axial_rope(ROWS, DIFF_HEAD_DIM)
    cos_d, sin_d = _axial_rope(ROWS, GQA_HEAD_DIM)
    lb_cum = jnp.cumsum(jax.nn.softmax(hgrn_lb_logits.astype(jnp.float32), axis=1), axis=1)
    silu_c = jax.nn.silu(c)
    silu_cc = jax.nn.silu(c_ctx)

    for l in range(DEPTH):
        last = l == DEPTH - 1
        m_lat = jnp.split((silu_c @ mod_w[l] + mod_b[l])[:, None, :], N_MOD, axis=-1)
        m_ctx = jnp.split((silu_cc @ mod_w[l] + mod_b[l])[None, None, :], N_MOD, axis=-1)
        h_lat = _modulate(_rms_norm(x, norm_mix_w[l]), m_lat[0], m_lat[1])
        h_ctx = _modulate(_rms_norm(ctx, norm_mix_w[l]), m_ctx[0], m_ctx[1])
        if l % 2 == 0:
            e = l // 2
            lambda_init = 0.8 - 0.6 * math.exp(-0.3 * l)
            y_lat, y_ctx = _mixer_ab(h_lat, h_ctx, cos_b, sin_b, ab_w_in[e], ab_w_out[e], lru_conv_w[e], lru_conv_b[e],
                                     lru_wa[e], lru_ba[e], lru_wx[e], lru_bx[e], lru_lambda[e],
                                     diff_lq1[e], diff_lk1[e], diff_lq2[e], diff_lk2[e], diff_subln_w[e],
                                     lambda_init, not last)
        else:
            o = l // 2
            lb = lb_cum[:, l] - lb_cum[:, 0]
            y_lat, y_ctx = _mixer_cd(h_lat, h_ctx, cos_d, sin_d, cd_w_in[o], cd_w_out[o], lb,
                                     hgrn_norm_w[o], gqa_q_norm_w[o], gqa_k_norm_w[o], not last)
        x = x + m_lat[2] * y_lat
        x = x + m_lat[5] * _swiglu(_modulate(_rms_norm(x, norm_ffn_w[l]), m_lat[3], m_lat[4]),
                                   ffn_w_gate[l], ffn_w_up[l], ffn_w_down[l])
        if not last:
            ctx = ctx + m_ctx[2] * y_ctx
            ctx = ctx + m_ctx[5] * _swiglu(_modulate(_rms_norm(ctx, norm_ffn_w[l]), m_ctx[3], m_ctx[4]),
                                           ffn_w_gate[l], ffn_w_up[l], ffn_w_down[l])

    return _rms_norm(x, final_norm_w)
```

```python
import functools
import math

import numpy as np
import jax
import jax.numpy as jnp
from jax import lax
from jax.experimental import pallas as pl
from jax.experimental.pallas import tpu as pltpu

F32 = jnp.float32
BF16 = jnp.bfloat16

GRID_W = 64
NORM_EPS = 1e-6
ROPE_THETA = 10000.0
N_MOD = 6
LRU_WIDTH = 1024
LRU_BLOCKS = 8
LRU_BLOCK = 128
LRU_C = 8.0
DIFF_HEADS = 8
DIFF_HEAD_DIM = 64
HGRN_HEADS = 8
HGRN_WIDTH = 1024
GQA_HEADS = 8
GQA_KV_HEADS = 2
GQA_REP = 4
GQA_HEAD_DIM = 128
LOG2E = 1.4426950408889634

LANES = 128
SUBLANES = 8
VMEM_LIMIT = 56 * 1024 * 1024

HGRN_CHUNK = 64


def _cparams(sem):
    return pltpu.CompilerParams(dimension_semantics=sem, vmem_limit_bytes=VMEM_LIMIT)


def _dot(a, b):
    return jnp.dot(a, b, preferred_element_type=F32)


def _dot_nt(a, b):
    return lax.dot_general(a, b, (((1,), (1,)), ((), ())), preferred_element_type=F32)


def _dot_tn(a, b):
    return lax.dot_general(a, b, (((0,), (0,)), ((), ())), preferred_element_type=F32)


def _neg_expm1(y):
    series = -y * (1.0 + 0.5 * y * (1.0 + (1.0 / 3.0) * y * (1.0 + 0.25 * y)))
    return jnp.where(y > -0.03, series, 1.0 - jnp.exp(y))


def _rms(x):
    return x * lax.rsqrt(jnp.mean(x * x, axis=-1, keepdims=True) + NORM_EPS)


def _mod_kernel(c_ref, w_ref, b_ref, o_ref):
    c = c_ref[...]
    a = c * jax.nn.sigmoid(c)
    o_ref[...] = jnp.dot(a, w_ref[...], preferred_element_type=F32,
                         precision=lax.Precision.HIGHEST) + b_ref[...]


def _modulation(cc, mod_w, mod_b):
    depth, d, n = mod_w.shape
    tn = 1024
    return pl.pallas_call(
        _mod_kernel,
        grid=(depth, n // tn),
        in_specs=[pl.BlockSpec((SUBLANES, d), lambda l, j: (0, 0)),
                  pl.BlockSpec((None, d, tn), lambda l, j: (l, 0, j)),
                  pl.BlockSpec((None, 1, tn), lambda l, j: (l, 0, j))],
        out_specs=pl.BlockSpec((None, SUBLANES, tn), lambda l, j: (l, 0, j)),
        out_shape=jax.ShapeDtypeStruct((depth, SUBLANES, n), F32),
        compiler_params=_cparams(("parallel", "parallel")),
        name="modulation",
    )(cc, mod_w, mod_b.reshape(depth, 1, n))


def _inproj_kernel(*refs, n_chunks, norm_chunks, rope_half):
    it = iter(refs)
    x_ref, nw_ref, sh_ref, sc_ref, w_ref = next(it), next(it), next(it), next(it), next(it)
    cw_ref = next(it) if norm_chunks else None
    if rope_half:
        cos_ref, sa_ref = next(it), next(it)
        sb_ref = next(it) if rope_half * 2 != LANES else None
    o_ref, xn_ref = next(it), next(it)

    @pl.when(pl.program_id(1) == 0)
    def _():
        x = x_ref[...]
        h = _rms(x) * nw_ref[...]
        xn_ref[...] = (h * (1.0 + sc_ref[...]) + sh_ref[...]).astype(BF16)

    acc = _dot(xn_ref[...], w_ref[...])
    for c in range(n_chunks):
        sl = slice(c * LANES, (c + 1) * LANES)
        y = acc[:, sl]
        if c < norm_chunks:
            y = _rms(y) * cw_ref[:, sl]
        if rope_half:
            if rope_half * 2 == LANES:
                y = y * cos_ref[...] + pltpu.roll(y, rope_half, 1) * sa_ref[...]
            else:
                y = (y * cos_ref[...] + pltpu.roll(y, LANES - rope_half, 1) * sa_ref[...]
                     + pltpu.roll(y, rope_half, 1) * sb_ref[...])
        o_ref[:, sl] = y.astype(o_ref.dtype)


def _inproj(x2d, seq, norm_w, mod, w, out_dtype, tn, chunk_w=None, norm_chunks=0, rope=None, rope_half=0):
    m, d = x2d.shape
    n = w.shape[1]
    tm = min(1024, seq)
    tpb = seq // tm if mod.shape[0] > 1 else m
    in_specs = [pl.BlockSpec((tm, d), lambda i, j: (i, 0)),
                pl.BlockSpec((1, d), lambda i, j: (0, 0)),
                pl.BlockSpec((None, None, 1, d), lambda i, j: (i // tpb, 0, 0, 0)),
                pl.BlockSpec((None, None, 1, d), lambda i, j: (i // tpb, 1, 0, 0)),
                pl.BlockSpec((d, tn), lambda i, j: (0, j))]
    args = [x2d, norm_w.reshape(1, d), mod, mod, w]
    if norm_chunks:
        in_specs.append(pl.BlockSpec((1, tn), lambda i, j: (0, j)))
        args.append(chunk_w)
    if rope_half:
        spt = seq // tm
        for t in rope:
            in_specs.append(pl.BlockSpec((tm, LANES), lambda i, j: (i % spt, 0)))
            args.append(t)
    kern = functools.partial(_inproj_kernel, n_chunks=tn // LANES, norm_chunks=norm_chunks, rope_half=rope_half)
    return pl.pallas_call(
        kern,
        grid=(m // tm, n // tn),
        in_specs=in_specs,
        out_specs=pl.BlockSpec((tm, tn), lambda i, j: (i, j)),
        out_shape=jax.ShapeDtypeStruct((m, n), out_dtype),
        scratch_shapes=[pltpu.VMEM((tm, d), BF16)],
        compiler_params=_cparams(("parallel", "arbitrary")),
        name="inproj",
    )(*args)


def _flash(qq_ref, k_ref, v_ref, kt_ref, vt_ref, m_ref, l_ref, acc_ref, *, tk, n_main, has_tail):
    m_ref[...] = jnp.full(m_ref.shape, -jnp.inf, F32)
    l_ref[...] = jnp.zeros(l_ref.shape, F32)
    acc_ref[...] = jnp.zeros(acc_ref.shape, F32)

    def step(k, v):
        s = _dot_nt(qq_ref[...], k)
        m_prev = m_ref[...]
        m_new = jnp.maximum(m_prev, jnp.max(s, axis=-1, keepdims=True))
        alpha = jnp.exp2(m_prev - m_new)
        p = jnp.exp2(s - m_new)
        l_ref[...] = alpha * l_ref[...] + jnp.sum(p, axis=-1, keepdims=True)
        acc_ref[...] = alpha * acc_ref[...] + _dot(p.astype(BF16), v)
        m_ref[...] = m_new

    def body(c, carry):
        off = pl.multiple_of(c * tk, tk)
        step(k_ref[pl.ds(off, tk), :], v_ref[pl.ds(off, tk), :])
        return carry

    lax.fori_loop(0, n_main, body, 0)
    if has_tail:
        step(kt_ref[...], vt_ref[...])


def _diff_attn_kernel(q_ref, k_ref, v_ref, kt_ref, vt_ref, lam_ref, sw_ref, o_ref,
                      qq_ref, m_ref, l_ref, acc_ref, *, tq, tk, n_main, has_tail, post_scale):
    q = q_ref[...]
    lane = lax.broadcasted_iota(jnp.int32, q.shape, 1)
    zero = jnp.zeros_like(q)
    qq_ref[0:tq, :] = jnp.where(lane < DIFF_HEAD_DIM, q, zero)
    qq_ref[tq:2 * tq, :] = jnp.where(lane >= DIFF_HEAD_DIM, q, zero)
    _flash(qq_ref, k_ref, v_ref, kt_ref, vt_ref, m_ref, l_ref, acc_ref, tk=tk, n_main=n_main, has_tail=has_tail)
    o = acc_ref[...] / l_ref[...]
    y = o[0:tq, :] - lam_ref[...] * o[tq:2 * tq, :]
    y = _rms(y) * sw_ref[...] * post_scale
    o_ref[...] = y.astype(o_ref.dtype)


def _diff_attn(qk, v, qk_tail, v_tail, lam_vec, subln_w, post_scale, batch, has_tail):
    t = qk.shape[1]
    tq = min(512, t)
    tk = min(1024, t)
    kern = functools.partial(_diff_attn_kernel, tq=tq, tk=tk, n_main=t // tk, has_tail=has_tail,
                             post_scale=post_scale)
    tt = qk_tail.shape[1]
    return pl.pallas_call(
        kern,
        grid=(batch, DIFF_HEADS, t // tq),
        in_specs=[pl.BlockSpec((None, tq, LANES), lambda b, h, i: (b, i, h)),
                  pl.BlockSpec((None, t, LANES), lambda b, h, i: (b, 0, DIFF_HEADS + h)),
                  pl.BlockSpec((None, t, LANES), lambda b, h, i: (b, 0, h)),
                  pl.BlockSpec((None, tt, LANES), lambda b, h, i: (b, 0, DIFF_HEADS + h)),
                  pl.BlockSpec((None, tt, LANES), lambda b, h, i: (b, 0, h)),
                  pl.BlockSpec((1, LANES), lambda b, h, i: (0, 0)),
                  pl.BlockSpec((1, LANES), lambda b, h, i: (0, 0))],
        out_specs=pl.BlockSpec((None, tq, LANES), lambda b, h, i: (b, i, h)),
        out_shape=jax.ShapeDtypeStruct((batch, t, DIFF_HEADS * LANES), BF16),
        scratch_shapes=[pltpu.VMEM((2 * tq, LANES), BF16), pltpu.VMEM((2 * tq, 1), F32),
                        pltpu.VMEM((2 * tq, 1), F32), pltpu.VMEM((2 * tq, LANES), F32)],
        compiler_params=_cparams(("parallel", "parallel", "arbitrary")),
        name="diff_attn",
    )(qk, qk, v, qk_tail, v_tail, lam_vec, subln_w)


def _gqa_kernel(q_ref, k_ref, v_ref, kt_ref, vt_ref, o_ref, qq_ref, m_ref, l_ref, acc_ref,
                *, tq, tk, n_main, has_tail):
    for r in range(GQA_REP):
        qq_ref[r * tq:(r + 1) * tq, :] = q_ref[:, r * LANES:(r + 1) * LANES]
    _flash(qq_ref, k_ref, v_ref, kt_ref, vt_ref, m_ref, l_ref, acc_ref, tk=tk, n_main=n_main, has_tail=has_tail)
    o = acc_ref[...] / l_ref[...]
    for r in range(GQA_REP):
        o_ref[:, r * LANES:(r + 1) * LANES] = o[r * tq:(r + 1) * tq, :].astype(o_ref.dtype)


def _gqa_attn(qk, v, qk_tail, v_tail, batch, has_tail):
    t = qk.shape[1]
    tq = min(256, t)
    tk = min(1024, t)
    gw = GQA_REP * LANES
    kern = functools.partial(_gqa_kernel, tq=tq, tk=tk, n_main=t // tk, has_tail=has_tail)
    tt = qk_tail.shape[1]
    return pl.pallas_call(
        kern,
        grid=(batch, GQA_KV_HEADS, t // tq),
        in_specs=[pl.BlockSpec((None, tq, gw), lambda b, g, i: (b, i, g)),
                  pl.BlockSpec((None, t, LANES), lambda b, g, i: (b, 0, GQA_HEADS + g)),
                  pl.BlockSpec((None, t, LANES), lambda b, g, i: (b, 0, g)),
                  pl.BlockSpec((None, tt, LANES), lambda b, g, i: (b, 0, GQA_HEADS + g)),
                  pl.BlockSpec((None, tt, LANES), lambda b, g, i: (b, 0, g))],
        out_specs=pl.BlockSpec((None, tq, gw), lambda b, g, i: (b, i, g)),
        out_shape=jax.ShapeDtypeStruct((batch, t, GQA_HEADS * LANES), BF16),
        scratch_shapes=[pltpu.VMEM((GQA_REP * tq, LANES), BF16), pltpu.VMEM((GQA_REP * tq, 1), F32),
                        pltpu.VMEM((GQA_REP * tq, 1), F32), pltpu.VMEM((GQA_REP * tq, LANES), F32)],
        compiler_params=_cparams(("parallel", "parallel", "arbitrary")),
        name="gqa_attn",
    )(qk, qk, v, qk_tail, v_tail)


def _rglru_kernel(xf_ref, xfp_ref, xfn_ref, xb_ref, xbp_ref, xbn_ref, cw_ref, cb_ref, wg_ref, bg_ref, cv_ref,
                  h0_ref, hf_ref, hb_ref, ht_ref, a_scr, b_scr, st_scr, *, tb, nblk):
    i = pl.program_id(1)

    @pl.when(i == 0)
    def _():
        st_scr[...] = h0_ref[...]

    row = lax.broadcasted_iota(jnp.int32, (tb, LRU_WIDTH), 0)

    def gates(d, x_ref, xp_ref, xn_ref, blk):
        x = x_ref[...]
        prev = xp_ref[SUBLANES - 1:SUBLANES, :] * (blk > 0).astype(F32)
        has_next = (blk < nblk - 1).astype(F32)
        nxt0 = xn_ref[0:1, :] * has_next
        nxt1 = xn_ref[1:2, :] * has_next
        xm1 = jnp.where(row == 0, prev, pltpu.roll(x, 1, 0))
        xp1 = jnp.where(row == tb - 1, nxt0, pltpu.roll(x, tb - 1, 0))
        xp2 = jnp.where(row == tb - 2, nxt0, jnp.where(row == tb - 1, nxt1, pltpu.roll(x, tb - 2, 0)))
        y = xm1 * cw_ref[0:1, :] + x * cw_ref[1:2, :] + xp1 * cw_ref[2:3, :] + xp2 * cw_ref[3:4, :] + cb_ref[...]
        yb = y.astype(BF16)
        for c in range(LRU_BLOCKS):
            sl = slice(c * LRU_BLOCK, (c + 1) * LRU_BLOCK)
            z = _dot(yb[:, sl], wg_ref[d, c]) + bg_ref[d, c]
            r = jax.nn.sigmoid(z[:, 0:LRU_BLOCK])
            g = jax.nn.sigmoid(z[:, LRU_BLOCK:2 * LRU_BLOCK])
            log_a = r * cv_ref[d:d + 1, sl]
            a_scr[d, :, sl] = jnp.exp(log_a)
            b_scr[d, :, sl] = jnp.sqrt(_neg_expm1(2.0 * log_a)) * (g * y[:, sl])

    gates(0, xf_ref, xfp_ref, xfn_ref, i)
    gates(1, xb_ref, xbp_ref, xbn_ref, nblk - 1 - i)

    row8 = lax.broadcasted_iota(jnp.int32, (SUBLANES, LRU_WIDTH), 0)
    nt = tb // SUBLANES

    def scan(d, out_ref):
        rev = d == 1

        def body(r, h):
            off = pl.multiple_of((nt - 1 - r if rev else r) * SUBLANES, SUBLANES)
            a8 = a_scr[d, pl.ds(off, SUBLANES), :]
            b8 = b_scr[d, pl.ds(off, SUBLANES), :]
            for s in (1, 2, 4):
                if rev:
                    ok = row8 < SUBLANES - s
                    sh = SUBLANES - s
                else:
                    ok = row8 >= s
                    sh = s
                a_sh = jnp.where(ok, pltpu.roll(a8, sh, 0), 1.0)
                b_sh = jnp.where(ok, pltpu.roll(b8, sh, 0), 0.0)
                b8 = a8 * b_sh + b8
                a8 = a8 * a_sh
            h8 = a8 * h + b8
            out_ref[pl.ds(off, SUBLANES), :] = h8
            return h8[0:1, :] if rev else h8[SUBLANES - 1:SUBLANES, :]

        st_scr[d:d + 1, :] = lax.fori_loop(0, nt, body, st_scr[d:d + 1, :])

    scan(0, hf_ref)
    scan(1, hb_ref)

    @pl.when(i == nblk - 1)
    def _():
        ht_ref[...] = st_scr[...]


def _rglru(gx, seq, batch, conv_w, conv_b, wg, bg, cv, h0):
    m = gx.shape[0]
    tb = min(256, seq)
    nblk = seq // tb
    hb8 = tb // SUBLANES
    last8 = m // SUBLANES - 1
    w = LRU_WIDTH

    def fidx(b, i):
        return b * nblk + i

    def bidx(b, i):
        return b * nblk + nblk - 1 - i

    def specs(idx):
        return [pl.BlockSpec((tb, w), lambda b, i: (idx(b, i), 1)),
                pl.BlockSpec((SUBLANES, w), lambda b, i: (jnp.maximum(idx(b, i) * hb8 - 1, 0), 1)),
                pl.BlockSpec((SUBLANES, w), lambda b, i: (jnp.minimum((idx(b, i) + 1) * hb8, last8), 1))]

    full = lambda shape: pl.BlockSpec(shape, lambda b, i: (0,) * len(shape))
    kern = functools.partial(_rglru_kernel, tb=tb, nblk=nblk)
    return pl.pallas_call(
        kern,
        grid=(batch, nblk),
        in_specs=specs(fidx) + specs(bidx) + [full(conv_w.shape), full((1, w)), full(wg.shape), full(bg.shape),
                                              full(cv.shape), pl.BlockSpec((None, 2, w), lambda b, i: (b, 0, 0))],
        out_specs=[pl.BlockSpec((tb, w), lambda b, i: (fidx(b, i), 0)),
                   pl.BlockSpec((tb, w), lambda b, i: (bidx(b, i), 0)),
                   pl.BlockSpec((None, 2, w), lambda b, i: (b, 0, 0))],
        out_shape=[jax.ShapeDtypeStruct((m, w), F32), jax.ShapeDtypeStruct((m, w), F32),
                   jax.ShapeDtypeStruct((batch, 2, w), F32)],
        scratch_shapes=[pltpu.VMEM((2, tb, w), F32), pltpu.VMEM((2, tb, w), F32), pltpu.VMEM((2, w), F32)],
        compiler_params=_cparams(("parallel", "arbitrary")),
        name="rglru",
    )(gx, gx, gx, gx, gx, gx, conv_w, conv_b.reshape(1, w), wg, bg, cv, h0)


def _hgrn_consts(c):
    t = np.arange(c)
    tinc = (t[None, :] <= t[:, None]).astype(np.float32)
    urev = (t[None, :] > t[:, None]).astype(np.float32)
    blocks = [tinc, urev]
    masks = []
    m = c // 2
    while m >= 1:
        mid = (t // (2 * m)) * (2 * m) + m
        right = t >= mid
        u = t[None, :]
        g = np.where(right[:, None], (u >= mid[:, None]) & (u <= t[:, None]), (u > t[:, None]) & (u < mid[:, None]))
        blocks.append(g.astype(np.float32))
        same = (t[:, None] // (2 * m)) == (t[None, :] // (2 * m))
        masks.append((same & right[:, None] & (~right)[None, :]).astype(np.float32))
        m //= 2
    masks.append(np.eye(c, dtype=np.float32))
    flip = lambda a: a[::-1, ::-1]
    ones = np.ones((16, c), np.float32)
    w = np.stack([np.concatenate(blocks + [ones], 0), np.concatenate([flip(b) for b in blocks] + [ones], 0)])
    cm = np.stack([np.stack(masks), np.stack([flip(a) for a in masks])])
    return w, cm


def _hgrn_kernel(qf_ref, ff_ref, vf_ref, qb_ref, fb_ref, vb_ref, lb_ref, wc_ref, cm_ref, s0_ref,
                 of_ref, ob_ref, st_ref, st_scr, *, c, levels, nchunk):
    i = pl.program_id(1)

    @pl.when(i == 0)
    def _():
        st_scr[...] = s0_ref[...]

    def one_dir(d, q_ref, f_ref, v_ref, o_ref):
        q = q_ref[...]
        q = q * jax.nn.sigmoid(q)
        lb = lb_ref[d:d + 1, :]
        f = lb + (1.0 - lb) * jax.nn.sigmoid(f_ref[...])
        kk = 1.0 - f
        g = jnp.log(f)
        g1 = g.astype(BF16)
        g2 = (g - g1.astype(F32)).astype(BF16)
        w = wc_ref[d]
        e = jnp.exp(_dot(w, g1) + _dot(w, g2))
        v = v_ref[...].astype(BF16)
        for h in range(HGRN_HEADS):
            sl = slice(h * LANES, (h + 1) * LANES)
            st = st_scr[d, h]
            qh, kh, vh = q[:, sl], kk[:, sl], v[:, sl]
            o = _dot_nt((qh * e[0:c, sl]).astype(BF16), st.astype(BF16))
            sc = cm_ref[d, levels] * _dot_nt(qh.astype(BF16), kh.astype(BF16))
            for l in range(levels):
                el = e[(2 + l) * c:(3 + l) * c, sl]
                sc = sc + cm_ref[d, l] * _dot_nt((qh * el).astype(BF16), (kh * el).astype(BF16))
            o_ref[:, sl] = o + _dot(sc.astype(BF16), vh)
            etot = e[(2 + levels) * c:(2 + levels) * c + 1, sl]
            st_scr[d, h] = st * etot + _dot_tn(vh, (kh * e[c:2 * c, sl]).astype(BF16))

    one_dir(0, qf_ref, ff_ref, vf_ref, of_ref)
    one_dir(1, qb_ref, fb_ref, vb_ref, ob_ref)

    @pl.when(i == nchunk - 1)
    def _():
        st_ref[...] = st_scr[...]


def _hgrn(z, seq, batch, lb, s0):
    m = z.shape[0]
    c = min(HGRN_CHUNK, seq)
    nchunk = seq // c
    levels = int(math.log2(c))
    wnp, cmnp = _hgrn_consts(c)
    wc = jnp.asarray(wnp, BF16)
    cm = jnp.asarray(cmnp, F32)
    w = HGRN_WIDTH

    def fidx(b, i):
        return b * nchunk + i

    def bidx(b, i):
        return b * nchunk + nchunk - 1 - i

    blk = lambda idx, col: pl.BlockSpec((c, w), lambda b, i: (idx(b, i), col))
    full = lambda shape: pl.BlockSpec(shape, lambda b, i: (0,) * len(shape))
    st_spec = pl.BlockSpec((None, 2, HGRN_HEADS, LANES, LANES), lambda b, i: (b, 0, 0, 0, 0))
    kern = functools.partial(_hgrn_kernel, c=c, levels=levels, nchunk=nchunk)
    return pl.pallas_call(
        kern,
        grid=(batch, nchunk),
        in_specs=[blk(fidx, 0), blk(fidx, 1), blk(fidx, 3), blk(bidx, 0), blk(bidx, 2), blk(bidx, 3),
                  full(lb.shape), full(wc.shape), full(cm.shape), st_spec],
        out_specs=[pl.BlockSpec((c, w), lambda b, i: (fidx(b, i), 0)),
                   pl.BlockSpec((c, w), lambda b, i: (bidx(b, i), 0)), st_spec],
        out_shape=[jax.ShapeDtypeStruct((m, w), F32), jax.ShapeDtypeStruct((m, w), F32),
                   jax.ShapeDtypeStruct((batch, 2, HGRN_HEADS, LANES, LANES), F32)],
        scratch_shapes=[pltpu.VMEM((2, HGRN_HEADS, LANES, LANES), F32)],
        compiler_params=_cparams(("parallel", "arbitrary")),
        name="hgrn2",
    )(z, z, z, z, z, z, lb, wc, cm, s0)


def _outproj_kernel(x_ref, p0_ref, p1_ref, g_ref, att_ref, nw_ref, w_ref, gt_ref, o_ref, *, mode):
    half = w_ref.shape[0] // 2
    s = p0_ref[...] + p1_ref[...]
    g = g_ref[...]
    if mode == "ab":
        a = s * jax.nn.gelu(g, approximate=True)
    else:
        parts = []
        for h in range(HGRN_HEADS):
            sl = slice(h * LANES, (h + 1) * LANES)
            parts.append(_rms(s[:, sl]) * nw_ref[...])
        a = jnp.concatenate(parts, axis=-1) * (g * jax.nn.sigmoid(g))
    acc = _dot(a.astype(BF16), w_ref[0:half, :]) + _dot(att_ref[...], w_ref[half:2 * half, :])
    o_ref[...] = x_ref[...] + gt_ref[...] * acc


def _outproj(x2d, seq, p0, p1, gsrc, gcol, att, head_norm_w, w_out, mod, mode):
    m, d = x2d.shape
    tm = min(256, seq)
    tpb = seq // tm if mod.shape[0] > 1 else m
    hw = w_out.shape[0] // 2
    kern = functools.partial(_outproj_kernel, mode=mode)
    return pl.pallas_call(
        kern,
        grid=(m // tm,),
        in_specs=[pl.BlockSpec((tm, d), lambda i: (i, 0)),
                  pl.BlockSpec((tm, hw), lambda i: (i, 0)),
                  pl.BlockSpec((tm, hw), lambda i: (i, 0)),
                  pl.BlockSpec((tm, hw), lambda i: (i, gcol)),
                  pl.BlockSpec((tm, hw), lambda i: (i, 0)),
                  pl.BlockSpec((1, LANES), lambda i: (0, 0)),
                  pl.BlockSpec(w_out.shape, lambda i: (0, 0)),
                  pl.BlockSpec((None, None, 1, d), lambda i: (i // tpb, 2, 0, 0))],
        out_specs=pl.BlockSpec((tm, d), lambda i: (i, 0)),
        out_shape=jax.ShapeDtypeStruct((m, d), F32),
        compiler_params=_cparams(("parallel",)),
        name="outproj_" + mode,
    )(x2d, p0, p1, gsrc, att, head_norm_w, w_out, mod)


def _ffn_kernel(x_ref, nw_ref, sh_ref, sc_ref, gt_ref, wg_ref, wu_ref, wd_ref, fw_ref, o_ref, hn_ref, *, final):
    j = pl.program_id(1)

    @pl.when(j == 0)
    def _():
        h = _rms(x_ref[...]) * nw_ref[...]
        hn_ref[...] = (h * (1.0 + sc_ref[...]) + sh_ref[...]).astype(BF16)
        o_ref[...] = jnp.zeros(o_ref.shape, F32)

    hn = hn_ref[...]
    g = _dot(hn, wg_ref[...])
    u = _dot(hn, wu_ref[...])
    a = (g * jax.nn.sigmoid(g) * u).astype(BF16)
    o_ref[...] += _dot(a, wd_ref[...])

    @pl.when(j == pl.num_programs(1) - 1)
    def _():
        y = x_ref[...] + gt_ref[...] * o_ref[...]
        if final:
            y = _rms(y) * fw_ref[...]
        o_ref[...] = y


def _ffn(x2d, seq, norm_w, mod, w_gate, w_up, w_down, final_w, final):
    m, d = x2d.shape
    f = w_gate.shape[1]
    tm = min(512, seq)
    tf = 512
    tpb = seq // tm if mod.shape[0] > 1 else m
    mspec = lambda k: pl.BlockSpec((None, None, 1, d), lambda i, j: (i // tpb, k, 0, 0))
    kern = functools.partial(_ffn_kernel, final=final)
    return pl.pallas_call(
        kern,
        grid=(m // tm, f // tf),
        in_specs=[pl.BlockSpec((tm, d), lambda i, j: (i, 0)),
                  pl.BlockSpec((1, d), lambda i, j: (0, 0)),
                  mspec(3), mspec(4), mspec(5),
                  pl.BlockSpec((d, tf), lambda i, j: (0, j)),
                  pl.BlockSpec((d, tf), lambda i, j: (0, j)),
                  pl.BlockSpec((tf, d), lambda i, j: (j, 0)),
                  pl.BlockSpec((1, d), lambda i, j: (0, 0))],
        out_specs=pl.BlockSpec((tm, d), lambda i, j: (i, 0)),
        out_shape=jax.ShapeDtypeStruct((m, d), F32),
        scratch_shapes=[pltpu.VMEM((tm, d), BF16)],
        compiler_params=_cparams(("parallel", "arbitrary")),
        name="ffn",
    )(x2d, norm_w.reshape(1, d), mod, mod, mod, w_gate, w_up, w_down, final_w.reshape(1, d))


def _rope_tables(rows, head_dim):
    n_freq = head_dim // 4
    half = head_dim // 2
    row = jnp.repeat(jnp.arange(rows, dtype=F32), GRID_W)
    col = jnp.tile(jnp.arange(GRID_W, dtype=F32), rows)
    inv = ROPE_THETA ** (-jnp.arange(n_freq, dtype=F32) / n_freq)
    ang = jnp.concatenate([row[:, None] * inv, col[:, None] * inv], axis=-1)
    cos, sin = jnp.cos(ang), jnp.sin(ang)
    reps = LANES // head_dim
    zero = jnp.zeros_like(sin)
    cos_t = jnp.tile(jnp.concatenate([cos, cos], -1), (1, reps))
    if half * 2 == LANES:
        return cos_t, jnp.concatenate([-sin, sin], -1), None
    sin_a = jnp.tile(jnp.concatenate([-sin, zero], -1), (1, reps))
    sin_b = jnp.tile(jnp.concatenate([zero, sin], -1), (1, reps))
    return cos_t, sin_a, sin_b


def _identity_rope(t, head_dim):
    one = jnp.ones((t, LANES), F32)
    zero = jnp.zeros((t, LANES), F32)
    return (one, zero, None) if head_dim == LANES else (one, zero, zero)


def kernel(x, c, ctx, c_ctx, mod_w, mod_b, norm_mix_w, norm_ffn_w, ffn_w_gate, ffn_w_up, ffn_w_down, ab_w_in, ab_w_out, lru_conv_w, lru_conv_b, lru_wa, lru_ba, lru_wx, lru_bx, lru_lambda, diff_lq1, diff_lk1, diff_lq2, diff_lk2, diff_subln_w, cd_w_in, cd_w_out, hgrn_lb_logits, hgrn_norm_w, gqa_q_norm_w, gqa_k_norm_w, final_norm_w):
    batch, seq, d = x.shape
    clen = ctx.shape[1]
    depth = mod_w.shape[0]
    rows = seq // GRID_W

    cc = jnp.zeros((SUBLANES, d), F32).at[0:batch].set(c).at[batch].set(c_ctx)
    mods = _modulation(cc, mod_w, mod_b)
    lb_cum = jnp.cumsum(jax.nn.softmax(hgrn_lb_logits.astype(F32), axis=1), axis=1)

    xl = x.reshape(batch * seq, d)
    xc = ctx.reshape(batch * clen, d)

    for l in range(depth):
        last = l == depth - 1
        m_lat = mods[l, 0:batch].reshape(batch, N_MOD, 1, d)
        m_ctx = mods[l, batch:batch + 1].reshape(1, N_MOD, 1, d)
        streams = ((xl, seq, m_lat), (xc, clen, m_ctx))
        if l % 2 == 0:
            e = l // 2
            lambda_init = 0.8 - 0.6 * math.exp(-0.3 * l)
            w_in = ab_w_in[e]
            qscale = DIFF_HEAD_DIM ** -0.5 * LOG2E
            w_gx = w_in[:, 0:2048].astype(BF16)
            w_qk = jnp.concatenate([w_in[:, 2048:3072] * qscale, w_in[:, 3072:4096]], axis=1).astype(BF16)
            w_v = w_in[:, 4096:5120].astype(BF16)
            ropes = (_rope_tables(rows, DIFF_HEAD_DIM), _identity_rope(clen, DIFF_HEAD_DIM))
            proj = []
            for (xs, t, md), rp in zip(streams, ropes):
                gx = _inproj(xs, t, norm_mix_w[l], md, w_gx, F32, 1024)
                qk = _inproj(xs, t, norm_mix_w[l], md, w_qk, BF16, 1024, rope=rp, rope_half=DIFF_HEAD_DIM // 2)
                v = _inproj(xs, t, norm_mix_w[l], md, w_v, BF16, 1024)
                proj.append((gx, qk.reshape(batch, t, 2048), v.reshape(batch, t, 1024)))
            (gx_l, qk_l, v_l), (gx_c, qk_c, v_c) = proj
            wg = jnp.concatenate([lru_wa[e], lru_wx[e]], axis=-1).astype(BF16)
            bg = jnp.concatenate([lru_ba[e].reshape(2, LRU_BLOCKS, 1, LRU_BLOCK),
                                  lru_bx[e].reshape(2, LRU_BLOCKS, 1, LRU_BLOCK)], axis=-1)
            cv = -LRU_C * jax.nn.softplus(-lru_lambda[e].astype(F32))
            h0 = jnp.zeros((batch, 2, LRU_WIDTH), F32)
            hf_c, hb_c, h_ctx = _rglru(gx_c, clen, batch, lru_conv_w[e], lru_conv_b[e], wg, bg, cv, h0)
            hf_l, hb_l, _ = _rglru(gx_l, seq, batch, lru_conv_w[e], lru_conv_b[e], wg, bg, cv, h_ctx)
            lam = (jnp.exp(jnp.sum(diff_lq1[e].astype(F32) * diff_lk1[e].astype(F32)))
                   - jnp.exp(jnp.sum(diff_lq2[e].astype(F32) * diff_lk2[e].astype(F32))) + lambda_init)
            lam_vec = jnp.full((1, LANES), lam, F32)
            sw = diff_subln_w[e].reshape(1, LANES)
            d_l = _diff_attn(qk_l, v_l, qk_c, v_c, lam_vec, sw, 1.0 - lambda_init, batch, True)
            w_out = ab_w_out[e].astype(BF16)
            dummy_nw = jnp.ones((1, LANES), F32)
            xl = _outproj(xl, seq, hf_l, hb_l, gx_l, 0, d_l.reshape(batch * seq, 1024), dummy_nw, w_out, m_lat, "ab")
            if not last:
                d_c = _diff_attn(qk_c, v_c, qk_c, v_c, lam_vec, sw, 1.0 - lambda_init, batch, False)
                xc = _outproj(xc, clen, hf_c, hb_c, gx_c, 0, d_c.reshape(batch * clen, 1024), dummy_nw, w_out,
                              m_ctx, "ab")
        else:
            o = l // 2
            lb = lb_cum[:, l] - lb_cum[:, 0]
            w_in = cd_w_in[o]
            w_z = w_in[:, 0:5120].astype(BF16)
            w_qk = w_in[:, 5120:6400].astype(BF16)
            w_v = w_in[:, 6400:6656].astype(BF16)
            qscale = GQA_HEAD_DIM ** -0.5 * LOG2E
            chunk_w = jnp.concatenate([jnp.tile(gqa_q_norm_w[o] * qscale, GQA_HEADS),
                                       jnp.tile(gqa_k_norm_w[o], GQA_KV_HEADS)]).reshape(1, 1280)
            ropes = (_rope_tables(rows, GQA_HEAD_DIM), _identity_rope(clen, GQA_HEAD_DIM))
            proj = []
            for (xs, t, md), rp in zip(streams, ropes):
                z = _inproj(xs, t, norm_mix_w[l], md, w_z, F32, 1024)
                qk = _inproj(xs, t, norm_mix_w[l], md, w_qk, BF16, 1280, chunk_w=chunk_w, norm_chunks=10,
                             rope=rp[0:2], rope_half=GQA_HEAD_DIM // 2)
                v = _inproj(xs, t, norm_mix_w[l], md, w_v, BF16, 256)
                proj.append((z, qk.reshape(batch, t, 1280), v.reshape(batch, t, 256)))
            (z_l, qk_l, v_l), (z_c, qk_c, v_c) = proj
            s0 = jnp.zeros((batch, 2, HGRN_HEADS, LANES, LANES), F32)
            of_c, ob_c, s_ctx = _hgrn(z_c, clen, batch, lb, s0)
            of_l, ob_l, _ = _hgrn(z_l, seq, batch, lb, s_ctx)
            att_l = _gqa_attn(qk_l, v_l, qk_c, v_c, batch, True)
            w_out = cd_w_out[o].astype(BF16)
            hnw = hgrn_norm_w[o].reshape(1, LANES)
            xl = _outproj(xl, seq, of_l, ob_l, z_l, 4, att_l.reshape(batch * seq, 1024), hnw, w_out, m_lat, "cd")
            if not last:
                att_c = _gqa_attn(qk_c, v_c, qk_c, v_c, batch, False)
                xc = _outproj(xc, clen, of_c, ob_c, z_c, 4, att_c.reshape(batch * clen, 1024), hnw, w_out,
                              m_ctx, "cd")
        wgt, wup, wdn = ffn_w_gate[l].astype(BF16), ffn_w_up[l].astype(BF16), ffn_w_down[l].astype(BF16)
        xl = _ffn(xl, seq, norm_ffn_w[l], m_lat, wgt, wup, wdn, final_norm_w, last)
        if not last:
            xc = _ffn(xc, clen, norm_ffn_w[l], m_ctx, wgt, wup, wdn, final_norm_w, False)

    return xl.reshape(batch, seq, d)
```

```python
import functools
import math

import numpy as np
import jax
import jax.numpy as jnp
from jax import lax
from jax.experimental import pallas as pl
from jax.experimental.pallas import tpu as pltpu

F32 = jnp.float32
BF16 = jnp.bfloat16

GRID_W = 64
NORM_EPS = 1e-6
ROPE_THETA = 10000.0
N_MOD = 6
LRU_WIDTH = 1024
LRU_BLOCKS = 8
LRU_BLOCK = 128
LRU_C = 8.0
DIFF_HEADS = 8
DIFF_HEAD_DIM = 64
HGRN_HEADS = 8
HGRN_WIDTH = 1024
GQA_HEADS = 8
GQA_KV_HEADS = 2
GQA_REP = 4
GQA_HEAD_DIM = 128
LOG2E = 1.4426950408889634

LANES = 128
SUBLANES = 8
VMEM_LIMIT = 56 * 1024 * 1024

HGRN_CHUNK = 64


def _cparams(sem):
    return pltpu.CompilerParams(dimension_semantics=sem, vmem_limit_bytes=VMEM_LIMIT)


def _dot(a, b):
    return jnp.dot(a, b, preferred_element_type=F32)


def _dot_nt(a, b):
    return lax.dot_general(a, b, (((1,), (1,)), ((), ())), preferred_element_type=F32)


def _dot_tn(a, b):
    return lax.dot_general(a, b, (((0,), (0,)), ((), ())), preferred_element_type=F32)


def _neg_expm1(y):
    series = -y * (1.0 + 0.5 * y * (1.0 + (1.0 / 3.0) * y * (1.0 + 0.25 * y)))
    return jnp.where(y > -0.03, series, 1.0 - jnp.exp(y))


def _rms(x):
    return x * lax.rsqrt(jnp.mean(x * x, axis=-1, keepdims=True) + NORM_EPS)


def _mod_kernel(c_ref, w_ref, b_ref, o_ref):
    c = c_ref[...]
    a = c * jax.nn.sigmoid(c)
    o_ref[...] = jnp.dot(a, w_ref[...], preferred_element_type=F32,
                         precision=lax.Precision.HIGHEST) + b_ref[...]


def _modulation(cc, mod_w, mod_b):
    depth, d, n = mod_w.shape
    tn = 1024
    return pl.pallas_call(
        _mod_kernel,
        grid=(depth, n // tn),
        in_specs=[pl.BlockSpec((SUBLANES, d), lambda l, j: (0, 0)),
                  pl.BlockSpec((None, d, tn), lambda l, j: (l, 0, j)),
                  pl.BlockSpec((None, 1, tn), lambda l, j: (l, 0, j))],
        out_specs=pl.BlockSpec((None, SUBLANES, tn), lambda l, j: (l, 0, j)),
        out_shape=jax.ShapeDtypeStruct((depth, SUBLANES, n), F32),
        compiler_params=_cparams(("parallel", "parallel")),
        name="modulation",
    )(cc, mod_w, mod_b.reshape(depth, 1, n))


def _inproj_kernel(*refs, n_chunks, norm_chunks, rope_half):
    it = iter(refs)
    x_ref, nw_ref, sh_ref, sc_ref, w_ref = next(it), next(it), next(it), next(it), next(it)
    cw_ref = next(it) if norm_chunks else None
    if rope_half:
        cos_ref, sa_ref = next(it), next(it)
        sb_ref = next(it) if rope_half * 2 != LANES else None
    o_ref, xn_ref = next(it), next(it)

    @pl.when(pl.program_id(1) == 0)
    def _():
        x = x_ref[...]
        h = _rms(x) * nw_ref[...]
        xn_ref[...] = (h * (1.0 + sc_ref[...]) + sh_ref[...]).astype(BF16)

    acc = _dot(xn_ref[...], w_ref[...])
    for c in range(n_chunks):
        sl = slice(c * LANES, (c + 1) * LANES)
        y = acc[:, sl]
        if c < norm_chunks:
            y = _rms(y) * cw_ref[:, sl]
        if rope_half:
            if rope_half * 2 == LANES:
                y = y * cos_ref[...] + pltpu.roll(y, rope_half, 1) * sa_ref[...]
            else:
                y = (y * cos_ref[...] + pltpu.roll(y, LANES - rope_half, 1) * sa_ref[...]
                     + pltpu.roll(y, rope_half, 1) * sb_ref[...])
        o_ref[:, sl] = y.astype(o_ref.dtype)


def _inproj(x2d, seq, norm_w, mod, w, out_dtype, tn, chunk_w=None, norm_chunks=0, rope=None, rope_half=0):
    m, d = x2d.shape
    n = w.shape[1]
    tm = min(1024, seq)
    tpb = seq // tm if mod.shape[0] > 1 else m
    in_specs = [pl.BlockSpec((tm, d), lambda i, j: (i, 0)),
                pl.BlockSpec((1, d), lambda i, j: (0, 0)),
                pl.BlockSpec((None, None, 1, d), lambda i, j: (i // tpb, 0, 0, 0)),
                pl.BlockSpec((None, None, 1, d), lambda i, j: (i // tpb, 1, 0, 0)),
                pl.BlockSpec((d, tn), lambda i, j: (0, j))]
    args = [x2d, norm_w.reshape(1, d), mod, mod, w]
    if norm_chunks:
        in_specs.append(pl.BlockSpec((1, tn), lambda i, j: (0, j)))
        args.append(chunk_w)
    if rope_half:
        spt = seq // tm
        for t in rope:
            in_specs.append(pl.BlockSpec((tm, LANES), lambda i, j: (i % spt, 0)))
            args.append(t)
    kern = functools.partial(_inproj_kernel, n_chunks=tn // LANES, norm_chunks=norm_chunks, rope_half=rope_half)
    return pl.pallas_call(
        kern,
        grid=(m // tm, n // tn),
        in_specs=in_specs,
        out_specs=pl.BlockSpec((tm, tn), lambda i, j: (i, j)),
        out_shape=jax.ShapeDtypeStruct((m, n), out_dtype),
        scratch_shapes=[pltpu.VMEM((tm, d), BF16)],
        compiler_params=_cparams(("parallel", "arbitrary")),
        name="inproj",
    )(*args)


def _flash(qq_ref, k_ref, v_ref, kt_ref, vt_ref, s0_ref, s1_ref, m_ref, l_ref, acc_ref, *, tk, n_main, has_tail):
    m_ref[...] = jnp.full(m_ref.shape, -jnp.inf, F32)
    l_ref[...] = jnp.zeros(l_ref.shape, F32)
    acc_ref[...] = jnp.zeros(acc_ref.shape, F32)

    def scores(off):
        return _dot_nt(qq_ref[...], k_ref[pl.ds(off, tk), :])

    def update(s, v):
        m_prev = m_ref[...]
        m_new = jnp.maximum(m_prev, jnp.max(s, axis=-1, keepdims=True))
        alpha = jnp.exp2(m_prev - m_new)
        p = jnp.exp2(s - m_new)
        l_ref[...] = alpha * l_ref[...] + jnp.sum(p, axis=-1, keepdims=True)
        acc_ref[...] = alpha * acc_ref[...] + _dot(p.astype(BF16), v)
        m_ref[...] = m_new

    if n_main % 2:
        def body(c, carry):
            off = pl.multiple_of(c * tk, tk)
            update(scores(off), v_ref[pl.ds(off, tk), :])
            return carry

        lax.fori_loop(0, n_main, body, 0)
    else:
        s0_ref[...] = scores(0)

        def body(c2, carry):
            off0 = pl.multiple_of(2 * c2 * tk, tk)
            off1 = pl.multiple_of(off0 + tk, tk)
            off2 = pl.multiple_of(jnp.minimum(2 * c2 + 2, n_main - 1) * tk, tk)
            s1_ref[...] = scores(off1)
            update(s0_ref[...], v_ref[pl.ds(off0, tk), :])
            s0_ref[...] = scores(off2)
            update(s1_ref[...], v_ref[pl.ds(off1, tk), :])
            return carry

        lax.fori_loop(0, n_main // 2, body, 0)
    if has_tail:
        update(_dot_nt(qq_ref[...], kt_ref[...]), vt_ref[...])


def _diff_attn_kernel(q_ref, k_ref, v_ref, kt_ref, vt_ref, lam_ref, sw_ref, o_ref,
                      qq_ref, s0_ref, s1_ref, m_ref, l_ref, acc_ref, *, tq, tk, n_main, has_tail, post_scale):
    q = q_ref[...]
    lane = lax.broadcasted_iota(jnp.int32, q.shape, 1)
    zero = jnp.zeros_like(q)
    qq_ref[0:tq, :] = jnp.where(lane < DIFF_HEAD_DIM, q, zero)
    qq_ref[tq:2 * tq, :] = jnp.where(lane >= DIFF_HEAD_DIM, q, zero)
    _flash(qq_ref, k_ref, v_ref, kt_ref, vt_ref, s0_ref, s1_ref, m_ref, l_ref, acc_ref,
           tk=tk, n_main=n_main, has_tail=has_tail)
    o = acc_ref[...] / l_ref[...]
    y = o[0:tq, :] - lam_ref[...] * o[tq:2 * tq, :]
    y = _rms(y) * sw_ref[...] * post_scale
    o_ref[...] = y.astype(o_ref.dtype)


def _diff_attn(qk, v, qk_tail, v_tail, lam_vec, subln_w, post_scale, batch, has_tail):
    t = qk.shape[1]
    tq = min(512, t)
    tk = min(1024, t)
    kern = functools.partial(_diff_attn_kernel, tq=tq, tk=tk, n_main=t // tk, has_tail=has_tail,
                             post_scale=post_scale)
    tt = qk_tail.shape[1]
    return pl.pallas_call(
        kern,
        grid=(batch, DIFF_HEADS, t // tq),
        in_specs=[pl.BlockSpec((None, tq, LANES), lambda b, h, i: (b, i, h)),
                  pl.BlockSpec((None, t, LANES), lambda b, h, i: (b, 0, DIFF_HEADS + h)),
                  pl.BlockSpec((None, t, LANES), lambda b, h, i: (b, 0, h)),
                  pl.BlockSpec((None, tt, LANES), lambda b, h, i: (b, 0, DIFF_HEADS + h)),
                  pl.BlockSpec((None, tt, LANES), lambda b, h, i: (b, 0, h)),
                  pl.BlockSpec((1, LANES), lambda b, h, i: (0, 0)),
                  pl.BlockSpec((1, LANES), lambda b, h, i: (0, 0))],
        out_specs=pl.BlockSpec((None, tq, LANES), lambda b, h, i: (b, i, h)),
        out_shape=jax.ShapeDtypeStruct((batch, t, DIFF_HEADS * LANES), BF16),
        scratch_shapes=[pltpu.VMEM((2 * tq, LANES), BF16), pltpu.VMEM((2 * tq, tk), F32),
                        pltpu.VMEM((2 * tq, tk), F32), pltpu.VMEM((2 * tq, 1), F32),
                        pltpu.VMEM((2 * tq, 1), F32), pltpu.VMEM((2 * tq, LANES), F32)],
        compiler_params=_cparams(("parallel", "parallel", "arbitrary")),
        name="diff_attn",
    )(qk, qk, v, qk_tail, v_tail, lam_vec, subln_w)


def _gqa_kernel(q_ref, k_ref, v_ref, kt_ref, vt_ref, o_ref, qq_ref, s0_ref, s1_ref, m_ref, l_ref, acc_ref,
                *, tq, tk, n_main, has_tail):
    for r in range(GQA_REP):
        qq_ref[r * tq:(r + 1) * tq, :] = q_ref[:, r * LANES:(r + 1) * LANES]
    _flash(qq_ref, k_ref, v_ref, kt_ref, vt_ref, s0_ref, s1_ref, m_ref, l_ref, acc_ref,
           tk=tk, n_main=n_main, has_tail=has_tail)
    o = acc_ref[...] / l_ref[...]
    for r in range(GQA_REP):
        o_ref[:, r * LANES:(r + 1) * LANES] = o[r * tq:(r + 1) * tq, :].astype(o_ref.dtype)


def _gqa_attn(qk, v, qk_tail, v_tail, batch, has_tail):
    t = qk.shape[1]
    tq = min(256, t)
    tk = min(1024, t)
    gw = GQA_REP * LANES
    kern = functools.partial(_gqa_kernel, tq=tq, tk=tk, n_main=t // tk, has_tail=has_tail)
    tt = qk_tail.shape[1]
    return pl.pallas_call(
        kern,
        grid=(batch, GQA_KV_HEADS, t // tq),
        in_specs=[pl.BlockSpec((None, tq, gw), lambda b, g, i: (b, i, g)),
                  pl.BlockSpec((None, t, LANES), lambda b, g, i: (b, 0, GQA_HEADS + g)),
                  pl.BlockSpec((None, t, LANES), lambda b, g, i: (b, 0, g)),
                  pl.BlockSpec((None, tt, LANES), lambda b, g, i: (b, 0, GQA_HEADS + g)),
                  pl.BlockSpec((None, tt, LANES), lambda b, g, i: (b, 0, g))],
        out_specs=pl.BlockSpec((None, tq, gw), lambda b, g, i: (b, i, g)),
        out_shape=jax.ShapeDtypeStruct((batch, t, GQA_HEADS * LANES), BF16),
        scratch_shapes=[pltpu.VMEM((GQA_REP * tq, LANES), BF16), pltpu.VMEM((GQA_REP * tq, tk), F32),
                        pltpu.VMEM((GQA_REP * tq, tk), F32), pltpu.VMEM((GQA_REP * tq, 1), F32),
                        pltpu.VMEM((GQA_REP * tq, 1), F32), pltpu.VMEM((GQA_REP * tq, LANES), F32)],
        compiler_params=_cparams(("parallel", "parallel", "arbitrary")),
        name="gqa_attn",
    )(qk, qk, v, qk_tail, v_tail)


def _rglru_kernel(xf_ref, xfp_ref, xfn_ref, xb_ref, xbp_ref, xbn_ref, cw_ref, cb_ref, wg_ref, bg_ref, cv_ref,
                  h0_ref, hf_ref, hb_ref, ht_ref, a_scr, b_scr, st_scr, *, tb, nblk):
    i = pl.program_id(1)

    @pl.when(i == 0)
    def _():
        st_scr[...] = h0_ref[...]

    row = lax.broadcasted_iota(jnp.int32, (tb, LRU_WIDTH), 0)

    def gates(d, x_ref, xp_ref, xn_ref, blk):
        x = x_ref[...]
        prev = xp_ref[SUBLANES - 1:SUBLANES, :] * (blk > 0).astype(F32)
        has_next = (blk < nblk - 1).astype(F32)
        nxt0 = xn_ref[0:1, :] * has_next
        nxt1 = xn_ref[1:2, :] * has_next
        xm1 = jnp.where(row == 0, prev, pltpu.roll(x, 1, 0))
        xp1 = jnp.where(row == tb - 1, nxt0, pltpu.roll(x, tb - 1, 0))
        xp2 = jnp.where(row == tb - 2, nxt0, jnp.where(row == tb - 1, nxt1, pltpu.roll(x, tb - 2, 0)))
        y = xm1 * cw_ref[0:1, :] + x * cw_ref[1:2, :] + xp1 * cw_ref[2:3, :] + xp2 * cw_ref[3:4, :] + cb_ref[...]
        yb = y.astype(BF16)
        for c in range(LRU_BLOCKS):
            sl = slice(c * LRU_BLOCK, (c + 1) * LRU_BLOCK)
            z = _dot(yb[:, sl], wg_ref[d, c]) + bg_ref[d, c]
            r = jax.nn.sigmoid(z[:, 0:LRU_BLOCK])
            g = jax.nn.sigmoid(z[:, LRU_BLOCK:2 * LRU_BLOCK])
            log_a = r * cv_ref[d:d + 1, sl]
            a_scr[d, :, sl] = jnp.exp(log_a)
            b_scr[d, :, sl] = jnp.sqrt(_neg_expm1(2.0 * log_a)) * (g * y[:, sl])

    gates(0, xf_ref, xfp_ref, xfn_ref, i)
    gates(1, xb_ref, xbp_ref, xbn_ref, nblk - 1 - i)

    row8 = lax.broadcasted_iota(jnp.int32, (SUBLANES, LRU_WIDTH), 0)
    nt = tb // SUBLANES

    def scan(d, out_ref):
        rev = d == 1

        def body(r, h):
            off = pl.multiple_of((nt - 1 - r if rev else r) * SUBLANES, SUBLANES)
            a8 = a_scr[d, pl.ds(off, SUBLANES), :]
            b8 = b_scr[d, pl.ds(off, SUBLANES), :]
            for s in (1, 2, 4):
                if rev:
                    ok = row8 < SUBLANES - s
                    sh = SUBLANES - s
                else:
                    ok = row8 >= s
                    sh = s
                a_sh = jnp.where(ok, pltpu.roll(a8, sh, 0), 1.0)
                b_sh = jnp.where(ok, pltpu.roll(b8, sh, 0), 0.0)
                b8 = a8 * b_sh + b8
                a8 = a8 * a_sh
            h8 = a8 * h + b8
            out_ref[pl.ds(off, SUBLANES), :] = h8
            return h8[0:1, :] if rev else h8[SUBLANES - 1:SUBLANES, :]

        st_scr[d:d + 1, :] = lax.fori_loop(0, nt, body, st_scr[d:d + 1, :])

    scan(0, hf_ref)
    scan(1, hb_ref)

    @pl.when(i == nblk - 1)
    def _():
        ht_ref[...] = st_scr[...]


def _rglru(gx, seq, batch, conv_w, conv_b, wg, bg, cv, h0):
    m = gx.shape[0]
    tb = min(256, seq)
    nblk = seq // tb
    hb8 = tb // SUBLANES
    last8 = m // SUBLANES - 1
    w = LRU_WIDTH

    def fidx(b, i):
        return b * nblk + i

    def bidx(b, i):
        return b * nblk + nblk - 1 - i

    def specs(idx):
        return [pl.BlockSpec((tb, w), lambda b, i: (idx(b, i), 1)),
                pl.BlockSpec((SUBLANES, w), lambda b, i: (jnp.maximum(idx(b, i) * hb8 - 1, 0), 1)),
                pl.BlockSpec((SUBLANES, w), lambda b, i: (jnp.minimum((idx(b, i) + 1) * hb8, last8), 1))]

    full = lambda shape: pl.BlockSpec(shape, lambda b, i: (0,) * len(shape))
    kern = functools.partial(_rglru_kernel, tb=tb, nblk=nblk)
    return pl.pallas_call(
        kern,
        grid=(batch, nblk),
        in_specs=specs(fidx) + specs(bidx) + [full(conv_w.shape), full((1, w)), full(wg.shape), full(bg.shape),
                                              full(cv.shape), pl.BlockSpec((None, 2, w), lambda b, i: (b, 0, 0))],
        out_specs=[pl.BlockSpec((tb, w), lambda b, i: (fidx(b, i), 0)),
                   pl.BlockSpec((tb, w), lambda b, i: (bidx(b, i), 0)),
                   pl.BlockSpec((None, 2, w), lambda b, i: (b, 0, 0))],
        out_shape=[jax.ShapeDtypeStruct((m, w), F32), jax.ShapeDtypeStruct((m, w), F32),
                   jax.ShapeDtypeStruct((batch, 2, w), F32)],
        scratch_shapes=[pltpu.VMEM((2, tb, w), F32), pltpu.VMEM((2, tb, w), F32), pltpu.VMEM((2, w), F32)],
        compiler_params=_cparams(("parallel", "arbitrary")),
        name="rglru",
    )(gx, gx, gx, gx, gx, gx, conv_w, conv_b.reshape(1, w), wg, bg, cv, h0)


def _hgrn_consts(c):
    t = np.arange(c)
    tinc = (t[None, :] <= t[:, None]).astype(np.float32)
    urev = (t[None, :] > t[:, None]).astype(np.float32)
    blocks = [tinc, urev]
    masks = []
    m = c // 2
    while m >= 1:
        mid = (t // (2 * m)) * (2 * m) + m
        right = t >= mid
        u = t[None, :]
        g = np.where(right[:, None], (u >= mid[:, None]) & (u <= t[:, None]), (u > t[:, None]) & (u < mid[:, None]))
        blocks.append(g.astype(np.float32))
        same = (t[:, None] // (2 * m)) == (t[None, :] // (2 * m))
        masks.append((same & right[:, None] & (~right)[None, :]).astype(np.float32))
        m //= 2
    masks.append(np.eye(c, dtype=np.float32))
    flip = lambda a: a[::-1, ::-1]
    ones = np.ones((16, c), np.float32)
    w = np.stack([np.concatenate(blocks + [ones], 0), np.concatenate([flip(b) for b in blocks] + [ones], 0)])
    cm = np.stack([np.stack(masks), np.stack([flip(a) for a in masks])])
    return w, cm


def _hgrn_kernel(qf_ref, ff_ref, vf_ref, qb_ref, fb_ref, vb_ref, lb_ref, wc_ref, cm_ref, s0_ref,
                 of_ref, ob_ref, st_ref, st_scr, *, c, levels, nchunk):
    i = pl.program_id(1)

    @pl.when(i == 0)
    def _():
        st_scr[...] = s0_ref[...]

    def one_dir(d, q_ref, f_ref, v_ref, o_ref):
        q = q_ref[...]
        q = q * jax.nn.sigmoid(q)
        lb = lb_ref[d:d + 1, :]
        f = lb + (1.0 - lb) * jax.nn.sigmoid(f_ref[...])
        kk = 1.0 - f
        g = jnp.log(f)
        g1 = g.astype(BF16)
        g2 = (g - g1.astype(F32)).astype(BF16)
        w = wc_ref[d]
        e = jnp.exp(_dot(w, g1) + _dot(w, g2))
        v = v_ref[...].astype(BF16)
        for h in range(HGRN_HEADS):
            sl = slice(h * LANES, (h + 1) * LANES)
            st = st_scr[d, h]
            qh, kh, vh = q[:, sl], kk[:, sl], v[:, sl]
            o = _dot_nt((qh * e[0:c, sl]).astype(BF16), st.astype(BF16))
            sc = cm_ref[d, levels] * _dot_nt(qh.astype(BF16), kh.astype(BF16))
            for l in range(levels):
                el = e[(2 + l) * c:(3 + l) * c, sl]
                sc = sc + cm_ref[d, l] * _dot_nt((qh * el).astype(BF16), (kh * el).astype(BF16))
            o_ref[:, sl] = o + _dot(sc.astype(BF16), vh)
            etot = e[(2 + levels) * c:(2 + levels) * c + 1, sl]
            st_scr[d, h] = st * etot + _dot_tn(vh, (kh * e[c:2 * c, sl]).astype(BF16))

    one_dir(0, qf_ref, ff_ref, vf_ref, of_ref)
    one_dir(1, qb_ref, fb_ref, vb_ref, ob_ref)

    @pl.when(i == nchunk - 1)
    def _():
        st_ref[...] = st_scr[...]


def _hgrn(z, seq, batch, lb, s0):
    m = z.shape[0]
    c = min(HGRN_CHUNK, seq)
    nchunk = seq // c
    levels = int(math.log2(c))
    wnp, cmnp = _hgrn_consts(c)
    wc = jnp.asarray(wnp, BF16)
    cm = jnp.asarray(cmnp, F32)
    w = HGRN_WIDTH

    def fidx(b, i):
        return b * nchunk + i

    def bidx(b, i):
        return b * nchunk + nchunk - 1 - i

    blk = lambda idx, col: pl.BlockSpec((c, w), lambda b, i: (idx(b, i), col))
    full = lambda shape: pl.BlockSpec(shape, lambda b, i: (0,) * len(shape))
    st_spec = pl.BlockSpec((None, 2, HGRN_HEADS, LANES, LANES), lambda b, i: (b, 0, 0, 0, 0))
    kern = functools.partial(_hgrn_kernel, c=c, levels=levels, nchunk=nchunk)
    return pl.pallas_call(
        kern,
        grid=(batch, nchunk),
        in_specs=[blk(fidx, 0), blk(fidx, 1), blk(fidx, 3), blk(bidx, 0), blk(bidx, 2), blk(bidx, 3),
                  full(lb.shape), full(wc.shape), full(cm.shape), st_spec],
        out_specs=[pl.BlockSpec((c, w), lambda b, i: (fidx(b, i), 0)),
                   pl.BlockSpec((c, w), lambda b, i: (bidx(b, i), 0)), st_spec],
        out_shape=[jax.ShapeDtypeStruct((m, w), F32), jax.ShapeDtypeStruct((m, w), F32),
                   jax.ShapeDtypeStruct((batch, 2, HGRN_HEADS, LANES, LANES), F32)],
        scratch_shapes=[pltpu.VMEM((2, HGRN_HEADS, LANES, LANES), F32)],
        compiler_params=_cparams(("parallel", "arbitrary")),
        name="hgrn2",
    )(z, z, z, z, z, z, lb, wc, cm, s0)


def _outproj_kernel(x_ref, p0_ref, p1_ref, g_ref, att_ref, nw_ref, w_ref, gt_ref, o_ref, *, mode):
    half = w_ref.shape[0] // 2
    s = p0_ref[...] + p1_ref[...]
    g = g_ref[...]
    if mode == "ab":
        a = s * jax.nn.gelu(g, approximate=True)
    else:
        parts = []
        for h in range(HGRN_HEADS):
            sl = slice(h * LANES, (h + 1) * LANES)
            parts.append(_rms(s[:, sl]) * nw_ref[...])
        a = jnp.concatenate(parts, axis=-1) * (g * jax.nn.sigmoid(g))
    acc = _dot(a.astype(BF16), w_ref[0:half, :]) + _dot(att_ref[...], w_ref[half:2 * half, :])
    o_ref[...] = x_ref[...] + gt_ref[...] * acc


def _outproj(x2d, seq, p0, p1, gsrc, gcol, att, head_norm_w, w_out, mod, mode):
    m, d = x2d.shape
    tm = min(256, seq)
    tpb = seq // tm if mod.shape[0] > 1 else m
    hw = w_out.shape[0] // 2
    kern = functools.partial(_outproj_kernel, mode=mode)
    return pl.pallas_call(
        kern,
        grid=(m // tm,),
        in_specs=[pl.BlockSpec((tm, d), lambda i: (i, 0)),
                  pl.BlockSpec((tm, hw), lambda i: (i, 0)),
                  pl.BlockSpec((tm, hw), lambda i: (i, 0)),
                  pl.BlockSpec((tm, hw), lambda i: (i, gcol)),
                  pl.BlockSpec((tm, hw), lambda i: (i, 0)),
                  pl.BlockSpec((1, LANES), lambda i: (0, 0)),
                  pl.BlockSpec(w_out.shape, lambda i: (0, 0)),
                  pl.BlockSpec((None, None, 1, d), lambda i: (i // tpb, 2, 0, 0))],
        out_specs=pl.BlockSpec((tm, d), lambda i: (i, 0)),
        out_shape=jax.ShapeDtypeStruct((m, d), F32),
        compiler_params=_cparams(("parallel",)),
        name="outproj_" + mode,
    )(x2d, p0, p1, gsrc, att, head_norm_w, w_out, mod)


def _ffn_kernel(x_ref, nw_ref, sh_ref, sc_ref, gt_ref, wg_ref, wu_ref, wd_ref, fw_ref, o_ref, hn_ref, *, final):
    j = pl.program_id(1)

    @pl.when(j == 0)
    def _():
        h = _rms(x_ref[...]) * nw_ref[...]
        hn_ref[...] = (h * (1.0 + sc_ref[...]) + sh_ref[...]).astype(BF16)
        o_ref[...] = jnp.zeros(o_ref.shape, F32)

    hn = hn_ref[...]
    g = _dot(hn, wg_ref[...])
    u = _dot(hn, wu_ref[...])
    a = (g * jax.nn.sigmoid(g) * u).astype(BF16)
    o_ref[...] += _dot(a, wd_ref[...])

    @pl.when(j == pl.num_programs(1) - 1)
    def _():
        y = x_ref[...] + gt_ref[...] * o_ref[...]
        if final:
            y = _rms(y) * fw_ref[...]
        o_ref[...] = y


def _ffn(x2d, seq, norm_w, mod, w_gate, w_up, w_down, final_w, final):
    m, d = x2d.shape
    f = w_gate.shape[1]
    tm = min(512, seq)
    tf = 512
    tpb = seq // tm if mod.shape[0] > 1 else m
    mspec = lambda k: pl.BlockSpec((None, None, 1, d), lambda i, j: (i // tpb, k, 0, 0))
    kern = functools.partial(_ffn_kernel, final=final)
    return pl.pallas_call(
        kern,
        grid=(m // tm, f // tf),
        in_specs=[pl.BlockSpec((tm, d), lambda i, j: (i, 0)),
                  pl.BlockSpec((1, d), lambda i, j: (0, 0)),
                  mspec(3), mspec(4), mspec(5),
                  pl.BlockSpec((d, tf), lambda i, j: (0, j)),
                  pl.BlockSpec((d, tf), lambda i, j: (0, j)),
                  pl.BlockSpec((tf, d), lambda i, j: (j, 0)),
                  pl.BlockSpec((1, d), lambda i, j: (0, 0))],
        out_specs=pl.BlockSpec((tm, d), lambda i, j: (i, 0)),
        out_shape=jax.ShapeDtypeStruct((m, d), F32),
        scratch_shapes=[pltpu.VMEM((tm, d), BF16)],
        compiler_params=_cparams(("parallel", "arbitrary")),
        name="ffn",
    )(x2d, norm_w.reshape(1, d), mod, mod, mod, w_gate, w_up, w_down, final_w.reshape(1, d))


def _rope_tables(rows, head_dim):
    n_freq = head_dim // 4
    half = head_dim // 2
    row = jnp.repeat(jnp.arange(rows, dtype=F32), GRID_W)
    col = jnp.tile(jnp.arange(GRID_W, dtype=F32), rows)
    inv = ROPE_THETA ** (-jnp.arange(n_freq, dtype=F32) / n_freq)
    ang = jnp.concatenate([row[:, None] * inv, col[:, None] * inv], axis=-1)
    cos, sin = jnp.cos(ang), jnp.sin(ang)
    reps = LANES // head_dim
    zero = jnp.zeros_like(sin)
    cos_t = jnp.tile(jnp.concatenate([cos, cos], -1), (1, reps))
    if half * 2 == LANES:
        return cos_t, jnp.concatenate([-sin, sin], -1), None
    sin_a = jnp.tile(jnp.concatenate([-sin, zero], -1), (1, reps))
    sin_b = jnp.tile(jnp.concatenate([zero, sin], -1), (1, reps))
    return cos_t, sin_a, sin_b


def _identity_rope(t, head_dim):
    one = jnp.ones((t, LANES), F32)
    zero = jnp.zeros((t, LANES), F32)
    return (one, zero, None) if head_dim == LANES else (one, zero, zero)


def kernel(x, c, ctx, c_ctx, mod_w, mod_b, norm_mix_w, norm_ffn_w, ffn_w_gate, ffn_w_up, ffn_w_down, ab_w_in, ab_w_out, lru_conv_w, lru_conv_b, lru_wa, lru_ba, lru_wx, lru_bx, lru_lambda, diff_lq1, diff_lk1, diff_lq2, diff_lk2, diff_subln_w, cd_w_in, cd_w_out, hgrn_lb_logits, hgrn_norm_w, gqa_q_norm_w, gqa_k_norm_w, final_norm_w):
    batch, seq, d = x.shape
    clen = ctx.shape[1]
    depth = mod_w.shape[0]
    rows = seq // GRID_W

    cc = jnp.zeros((SUBLANES, d), F32).at[0:batch].set(c).at[batch].set(c_ctx)
    mods = _modulation(cc, mod_w, mod_b)
    lb_cum = jnp.cumsum(jax.nn.softmax(hgrn_lb_logits.astype(F32), axis=1), axis=1)

    xl = x.reshape(batch * seq, d)
    xc = ctx.reshape(batch * clen, d)

    for l in range(depth):
        last = l == depth - 1
        m_lat = mods[l, 0:batch].reshape(batch, N_MOD, 1, d)
        m_ctx = mods[l, batch:batch + 1].reshape(1, N_MOD, 1, d)
        streams = ((xl, seq, m_lat), (xc, clen, m_ctx))
        if l % 2 == 0:
            e = l // 2
            lambda_init = 0.8 - 0.6 * math.exp(-0.3 * l)
            w_in = ab_w_in[e]
            qscale = DIFF_HEAD_DIM ** -0.5 * LOG2E
            w_gx = w_in[:, 0:2048].astype(BF16)
            w_qk = jnp.concatenate([w_in[:, 2048:3072] * qscale, w_in[:, 3072:4096]], axis=1).astype(BF16)
            w_v = w_in[:, 4096:5120].astype(BF16)
            ropes = (_rope_tables(rows, DIFF_HEAD_DIM), _identity_rope(clen, DIFF_HEAD_DIM))
            proj = []
            for (xs, t, md), rp in zip(streams, ropes):
                gx = _inproj(xs, t, norm_mix_w[l], md, w_gx, F32, 1024)
                qk = _inproj(xs, t, norm_mix_w[l], md, w_qk, BF16, 1024, rope=rp, rope_half=DIFF_HEAD_DIM // 2)
                v = _inproj(xs, t, norm_mix_w[l], md, w_v, BF16, 1024)
                proj.append((gx, qk.reshape(batch, t, 2048), v.reshape(batch, t, 1024)))
            (gx_l, qk_l, v_l), (gx_c, qk_c, v_c) = proj
            wg = jnp.concatenate([lru_wa[e], lru_wx[e]], axis=-1).astype(BF16)
            bg = jnp.concatenate([lru_ba[e].reshape(2, LRU_BLOCKS, 1, LRU_BLOCK),
                                  lru_bx[e].reshape(2, LRU_BLOCKS, 1, LRU_BLOCK)], axis=-1)
            cv = -LRU_C * jax.nn.softplus(-lru_lambda[e].astype(F32))
            h0 = jnp.zeros((batch, 2, LRU_WIDTH), F32)
            hf_c, hb_c, h_ctx = _rglru(gx_c, clen, batch, lru_conv_w[e], lru_conv_b[e], wg, bg, cv, h0)
            hf_l, hb_l, _ = _rglru(gx_l, seq, batch, lru_conv_w[e], lru_conv_b[e], wg, bg, cv, h_ctx)
            lam = (jnp.exp(jnp.sum(diff_lq1[e].astype(F32) * diff_lk1[e].astype(F32)))
                   - jnp.exp(jnp.sum(diff_lq2[e].astype(F32) * diff_lk2[e].astype(F32))) + lambda_init)
            lam_vec = jnp.full((1, LANES), lam, F32)
            sw = diff_subln_w[e].reshape(1, LANES)
            d_l = _diff_attn(qk_l, v_l, qk_c, v_c, lam_vec, sw, 1.0 - lambda_init, batch, True)
            w_out = ab_w_out[e].astype(BF16)
            dummy_nw = jnp.ones((1, LANES), F32)
            xl = _outproj(xl, seq, hf_l, hb_l, gx_l, 0, d_l.reshape(batch * seq, 1024), dummy_nw, w_out, m_lat, "ab")
            if not last:
                d_c = _diff_attn(qk_c, v_c, qk_c, v_c, lam_vec, sw, 1.0 - lambda_init, batch, False)
                xc = _outproj(xc, clen, hf_c, hb_c, gx_c, 0, d_c.reshape(batch * clen, 1024), dummy_nw, w_out,
                              m_ctx, "ab")
        else:
            o = l // 2
            lb = lb_cum[:, l] - lb_cum[:, 0]
            w_in = cd_w_in[o]
            w_z = w_in[:, 0:5120].astype(BF16)
            w_qk = w_in[:, 5120:6400].astype(BF16)
            w_v = w_in[:, 6400:6656].astype(BF16)
            qscale = GQA_HEAD_DIM ** -0.5 * LOG2E
            chunk_w = jnp.concatenate([jnp.tile(gqa_q_norm_w[o] * qscale, GQA_HEADS),
                                       jnp.tile(gqa_k_norm_w[o], GQA_KV_HEADS)]).reshape(1, 1280)
            ropes = (_rope_tables(rows, GQA_HEAD_DIM), _identity_rope(clen, GQA_HEAD_DIM))
            proj = []
            for (xs, t, md), rp in zip(streams, ropes):
                z = _inproj(xs, t, norm_mix_w[l], md, w_z, F32, 1024)
                qk = _inproj(xs, t, norm_mix_w[l], md, w_qk, BF16, 1280, chunk_w=chunk_w, norm_chunks=10,
                             rope=rp[0:2], rope_half=GQA_HEAD_DIM // 2)
                v = _inproj(xs, t, norm_mix_w[l], md, w_v, BF16, 256)
                proj.append((z, qk.reshape(batch, t, 1280), v.reshape(batch, t, 256)))
            (z_l, qk_l, v_l), (z_c, qk_c, v_c) = proj
            s0 = jnp.zeros((batch, 2, HGRN_HEADS, LANES, LANES), F32)
            of_c, ob_c, s_ctx = _hgrn(z_c, clen, batch, lb, s0)
            of_l, ob_l, _ = _hgrn(z_l, seq, batch, lb, s_ctx)
            att_l = _gqa_attn(qk_l, v_l, qk_c, v_c, batch, True)
            w_out = cd_w_out[o].astype(BF16)
            hnw = hgrn_norm_w[o].reshape(1, LANES)
            xl = _outproj(xl, seq, of_l, ob_l, z_l, 4, att_l.reshape(batch * seq, 1024), hnw, w_out, m_lat, "cd")
            if not last:
                att_c = _gqa_attn(qk_c, v_c, qk_c, v_c, batch, False)
                xc = _outproj(xc, clen, of_c, ob_c, z_c, 4, att_c.reshape(batch * clen, 1024), hnw, w_out,
                              m_ctx, "cd")
        wgt, wup, wdn = ffn_w_gate[l].astype(BF16), ffn_w_up[l].astype(BF16), ffn_w_down[l].astype(BF16)
        xl = _ffn(xl, seq, norm_ffn_w[l], m_lat, wgt, wup, wdn, final_norm_w, last)
        if not last:
            xc = _ffn(xc, clen, norm_ffn_w[l], m_ctx, wgt, wup, wdn, final_norm_w, False)

    return xl.reshape(batch, seq, d)
```

```python
import functools
import math

import numpy as np
import jax
import jax.numpy as jnp
from jax import lax
from jax.experimental import pallas as pl
from jax.experimental.pallas import tpu as pltpu

F32 = jnp.float32
BF16 = jnp.bfloat16

GRID_W = 64
NORM_EPS = 1e-6
ROPE_THETA = 10000.0
N_MOD = 6
LRU_WIDTH = 1024
LRU_BLOCKS = 8
LRU_BLOCK = 128
LRU_C = 8.0
DIFF_HEADS = 8
DIFF_HEAD_DIM = 64
HGRN_HEADS = 8
HGRN_WIDTH = 1024
GQA_HEADS = 8
GQA_KV_HEADS = 2
GQA_REP = 4
GQA_HEAD_DIM = 128
LOG2E = 1.4426950408889634

LANES = 128
SUBLANES = 8
VMEM_LIMIT = 56 * 1024 * 1024

HGRN_CHUNK = 64


def _cparams(sem):
    return pltpu.CompilerParams(dimension_semantics=sem, vmem_limit_bytes=VMEM_LIMIT)


def _dot(a, b):
    return jnp.dot(a, b, preferred_element_type=F32)


def _dot_nt(a, b):
    return lax.dot_general(a, b, (((1,), (1,)), ((), ())), preferred_element_type=F32)


def _dot_tn(a, b):
    return lax.dot_general(a, b, (((0,), (0,)), ((), ())), preferred_element_type=F32)


def _neg_expm1(y):
    series = -y * (1.0 + 0.5 * y * (1.0 + (1.0 / 3.0) * y * (1.0 + 0.25 * y)))
    return jnp.where(y > -0.03, series, 1.0 - jnp.exp(y))


def _rms(x):
    return x * lax.rsqrt(jnp.mean(x * x, axis=-1, keepdims=True) + NORM_EPS)


def _mod_kernel(c_ref, w_ref, b_ref, o_ref):
    c = c_ref[...]
    a = c * jax.nn.sigmoid(c)
    o_ref[...] = jnp.dot(a, w_ref[...], preferred_element_type=F32,
                         precision=lax.Precision.HIGHEST) + b_ref[...]


def _modulation(cc, mod_w, mod_b):
    depth, d, n = mod_w.shape
    tn = 1024
    return pl.pallas_call(
        _mod_kernel,
        grid=(depth, n // tn),
        in_specs=[pl.BlockSpec((SUBLANES, d), lambda l, j: (0, 0)),
                  pl.BlockSpec((None, d, tn), lambda l, j: (l, 0, j)),
                  pl.BlockSpec((None, 1, tn), lambda l, j: (l, 0, j))],
        out_specs=pl.BlockSpec((None, SUBLANES, tn), lambda l, j: (l, 0, j)),
        out_shape=jax.ShapeDtypeStruct((depth, SUBLANES, n), F32),
        compiler_params=_cparams(("parallel", "parallel")),
        name="modulation",
    )(cc, mod_w, mod_b.reshape(depth, 1, n))


def _inproj_kernel(*refs, n_chunks, norm_chunks, rope_half):
    it = iter(refs)
    x_ref, nw_ref, sh_ref, sc_ref, w_ref = next(it), next(it), next(it), next(it), next(it)
    cw_ref = next(it) if norm_chunks else None
    if rope_half:
        cos_ref, sa_ref = next(it), next(it)
        sb_ref = next(it) if rope_half * 2 != LANES else None
    o_ref, xn_ref = next(it), next(it)

    @pl.when(pl.program_id(1) == 0)
    def _():
        x = x_ref[...]
        h = _rms(x) * nw_ref[...]
        xn_ref[...] = (h * (1.0 + sc_ref[...]) + sh_ref[...]).astype(BF16)

    acc = _dot(xn_ref[...], w_ref[...])
    for c in range(n_chunks):
        sl = slice(c * LANES, (c + 1) * LANES)
        y = acc[:, sl]
        if c < norm_chunks:
            y = _rms(y) * cw_ref[:, sl]
        if rope_half:
            if rope_half * 2 == LANES:
                y = y * cos_ref[...] + pltpu.roll(y, rope_half, 1) * sa_ref[...]
            else:
                y = (y * cos_ref[...] + pltpu.roll(y, LANES - rope_half, 1) * sa_ref[...]
                     + pltpu.roll(y, rope_half, 1) * sb_ref[...])
        o_ref[:, sl] = y.astype(o_ref.dtype)


def _inproj(x2d, seq, norm_w, mod, w, out_dtype, tn, chunk_w=None, norm_chunks=0, rope=None, rope_half=0):
    m, d = x2d.shape
    n = w.shape[1]
    tm = min(1024, seq)
    tpb = seq // tm if mod.shape[0] > 1 else m
    in_specs = [pl.BlockSpec((tm, d), lambda i, j: (i, 0)),
                pl.BlockSpec((1, d), lambda i, j: (0, 0)),
                pl.BlockSpec((None, None, 1, d), lambda i, j: (i // tpb, 0, 0, 0)),
                pl.BlockSpec((None, None, 1, d), lambda i, j: (i // tpb, 1, 0, 0)),
                pl.BlockSpec((d, tn), lambda i, j: (0, j))]
    args = [x2d, norm_w.reshape(1, d), mod, mod, w]
    if norm_chunks:
        in_specs.append(pl.BlockSpec((1, tn), lambda i, j: (0, j)))
        args.append(chunk_w)
    if rope_half:
        spt = seq // tm
        for t in rope:
            in_specs.append(pl.BlockSpec((tm, LANES), lambda i, j: (i % spt, 0)))
            args.append(t)
    kern = functools.partial(_inproj_kernel, n_chunks=tn // LANES, norm_chunks=norm_chunks, rope_half=rope_half)
    return pl.pallas_call(
        kern,
        grid=(m // tm, n // tn),
        in_specs=in_specs,
        out_specs=pl.BlockSpec((tm, tn), lambda i, j: (i, j)),
        out_shape=jax.ShapeDtypeStruct((m, n), out_dtype),
        scratch_shapes=[pltpu.VMEM((tm, d), BF16)],
        compiler_params=_cparams(("parallel", "arbitrary")),
        name="inproj",
    )(*args)


def _augment_values(v_ref, vt_ref, va_ref, vta_ref):
    va_ref[:, 0:LANES] = v_ref[...]
    va_ref[:, LANES:2 * LANES] = jnp.ones(v_ref.shape, BF16)
    vta_ref[:, 0:LANES] = vt_ref[...]
    vta_ref[:, LANES:2 * LANES] = jnp.ones(vt_ref.shape, BF16)


def _flash(qq_ref, k_ref, va_ref, kt_ref, vta_ref, s0_ref, s1_ref, m_ref, acc_ref, *, tk, n_main, has_tail):
    m_ref[...] = jnp.full(m_ref.shape, -jnp.inf, F32)
    acc_ref[...] = jnp.zeros(acc_ref.shape, F32)

    def update(s_ref, va):
        m_prev = m_ref[...]
        m_new = jnp.maximum(m_prev, jnp.max(s_ref[...], axis=-1, keepdims=True))
        alpha = jnp.exp2(m_prev - m_new)
        p = jnp.exp2(s_ref[...] - m_new).astype(BF16)
        acc_ref[...] = alpha * acc_ref[...] + _dot(p, va)
        m_ref[...] = m_new

    def scores(off):
        return _dot_nt(qq_ref[...], k_ref[pl.ds(off, tk), :])

    if has_tail:
        st_ref = s1_ref.at[:, 0:kt_ref.shape[0]]
        st_ref[...] = _dot_nt(qq_ref[...], kt_ref[...])
    s0_ref[...] = scores(0)
    if has_tail:
        update(st_ref, vta_ref[...])

    if n_main % 2:
        def body(c, carry):
            off = pl.multiple_of(c * tk, tk)
            s0_ref[...] = scores(off)
            update(s0_ref, va_ref[pl.ds(off, tk), :])
            return carry

        lax.fori_loop(0, n_main, body, 0)
    else:
        def body(c2, carry):
            off0 = pl.multiple_of(2 * c2 * tk, tk)
            off1 = pl.multiple_of(off0 + tk, tk)
            off2 = pl.multiple_of(jnp.minimum(2 * c2 + 2, n_main - 1) * tk, tk)
            s1_ref[...] = scores(off1)
            update(s0_ref, va_ref[pl.ds(off0, tk), :])
            s0_ref[...] = scores(off2)
            update(s1_ref, va_ref[pl.ds(off1, tk), :])
            return carry

        lax.fori_loop(0, n_main // 2, body, 0, unroll=4 if n_main % 8 == 0 else 1)
    return acc_ref[:, 0:LANES] / acc_ref[:, LANES:2 * LANES]


def _flash_scratch(rows, tk, t, tt):
    return [pltpu.VMEM((rows, LANES), BF16), pltpu.VMEM((t, 2 * LANES), BF16), pltpu.VMEM((tt, 2 * LANES), BF16),
            pltpu.VMEM((rows, tk), F32), pltpu.VMEM((rows, tk), F32), pltpu.VMEM((rows, 1), F32),
            pltpu.VMEM((rows, 2 * LANES), F32)]


def _diff_attn_kernel(q_ref, k_ref, v_ref, kt_ref, vt_ref, lam_ref, sw_ref, o_ref,
                      qq_ref, va_ref, vta_ref, s0_ref, s1_ref, m_ref, acc_ref,
                      *, tq, tk, n_main, has_tail, post_scale):
    @pl.when(pl.program_id(2) == 0)
    def _():
        _augment_values(v_ref, vt_ref, va_ref, vta_ref)

    q = q_ref[...]
    lane = lax.broadcasted_iota(jnp.int32, q.shape, 1)
    zero = jnp.zeros_like(q)
    qq_ref[0:tq, :] = jnp.where(lane < DIFF_HEAD_DIM, q, zero)
    qq_ref[tq:2 * tq, :] = jnp.where(lane >= DIFF_HEAD_DIM, q, zero)
    o = _flash(qq_ref, k_ref, va_ref, kt_ref, vta_ref, s0_ref, s1_ref, m_ref, acc_ref,
               tk=tk, n_main=n_main, has_tail=has_tail)
    y = o[0:tq, :] - lam_ref[...] * o[tq:2 * tq, :]
    y = _rms(y) * sw_ref[...] * post_scale
    o_ref[...] = y.astype(o_ref.dtype)


def _diff_attn(qk, v, qk_tail, v_tail, lam_vec, subln_w, post_scale, batch, has_tail):
    t = qk.shape[1]
    tq = min(512, t)
    tk = min(1024, t)
    kern = functools.partial(_diff_attn_kernel, tq=tq, tk=tk, n_main=t // tk, has_tail=has_tail,
                             post_scale=post_scale)
    tt = qk_tail.shape[1]
    return pl.pallas_call(
        kern,
        grid=(batch, DIFF_HEADS, t // tq),
        in_specs=[pl.BlockSpec((None, tq, LANES), lambda b, h, i: (b, i, h)),
                  pl.BlockSpec((None, t, LANES), lambda b, h, i: (b, 0, DIFF_HEADS + h)),
                  pl.BlockSpec((None, t, LANES), lambda b, h, i: (b, 0, h)),
                  pl.BlockSpec((None, tt, LANES), lambda b, h, i: (b, 0, DIFF_HEADS + h)),
                  pl.BlockSpec((None, tt, LANES), lambda b, h, i: (b, 0, h)),
                  pl.BlockSpec((1, LANES), lambda b, h, i: (0, 0)),
                  pl.BlockSpec((1, LANES), lambda b, h, i: (0, 0))],
        out_specs=pl.BlockSpec((None, tq, LANES), lambda b, h, i: (b, i, h)),
        out_shape=jax.ShapeDtypeStruct((batch, t, DIFF_HEADS * LANES), BF16),
        scratch_shapes=_flash_scratch(2 * tq, tk, t, tt),
        compiler_params=_cparams(("parallel", "parallel", "arbitrary")),
        name="diff_attn",
    )(qk, qk, v, qk_tail, v_tail, lam_vec, subln_w)


def _gqa_kernel(q_ref, k_ref, v_ref, kt_ref, vt_ref, o_ref, qq_ref, va_ref, vta_ref, s0_ref, s1_ref, m_ref, acc_ref,
                *, tq, tk, n_main, has_tail):
    @pl.when(pl.program_id(2) == 0)
    def _():
        _augment_values(v_ref, vt_ref, va_ref, vta_ref)

    for r in range(GQA_REP):
        qq_ref[r * tq:(r + 1) * tq, :] = q_ref[:, r * LANES:(r + 1) * LANES]
    o = _flash(qq_ref, k_ref, va_ref, kt_ref, vta_ref, s0_ref, s1_ref, m_ref, acc_ref,
               tk=tk, n_main=n_main, has_tail=has_tail)
    for r in range(GQA_REP):
        o_ref[:, r * LANES:(r + 1) * LANES] = o[r * tq:(r + 1) * tq, :].astype(o_ref.dtype)


def _gqa_attn(qk, v, qk_tail, v_tail, batch, has_tail):
    t = qk.shape[1]
    tq = min(256, t)
    tk = min(1024, t)
    gw = GQA_REP * LANES
    kern = functools.partial(_gqa_kernel, tq=tq, tk=tk, n_main=t // tk, has_tail=has_tail)
    tt = qk_tail.shape[1]
    return pl.pallas_call(
        kern,
        grid=(batch, GQA_KV_HEADS, t // tq),
        in_specs=[pl.BlockSpec((None, tq, gw), lambda b, g, i: (b, i, g)),
                  pl.BlockSpec((None, t, LANES), lambda b, g, i: (b, 0, GQA_HEADS + g)),
                  pl.BlockSpec((None, t, LANES), lambda b, g, i: (b, 0, g)),
                  pl.BlockSpec((None, tt, LANES), lambda b, g, i: (b, 0, GQA_HEADS + g)),
                  pl.BlockSpec((None, tt, LANES), lambda b, g, i: (b, 0, g))],
        out_specs=pl.BlockSpec((None, tq, gw), lambda b, g, i: (b, i, g)),
        out_shape=jax.ShapeDtypeStruct((batch, t, GQA_HEADS * LANES), BF16),
        scratch_shapes=_flash_scratch(GQA_REP * tq, tk, t, tt),
        compiler_params=_cparams(("parallel", "parallel", "arbitrary")),
        name="gqa_attn",
    )(qk, qk, v, qk_tail, v_tail)


def _rglru_kernel(xf_ref, xfp_ref, xfn_ref, xb_ref, xbp_ref, xbn_ref, cw_ref, cb_ref, wg_ref, bg_ref, cv_ref,
                  h0_ref, hf_ref, hb_ref, ht_ref, a_scr, b_scr, st_scr, *, tb, nblk):
    i = pl.program_id(1)

    @pl.when(i == 0)
    def _():
        st_scr[...] = h0_ref[...]

    row = lax.broadcasted_iota(jnp.int32, (tb, LRU_WIDTH), 0)

    def gates(d, x_ref, xp_ref, xn_ref, blk):
        x = x_ref[...]
        prev = xp_ref[SUBLANES - 1:SUBLANES, :] * (blk > 0).astype(F32)
        has_next = (blk < nblk - 1).astype(F32)
        nxt0 = xn_ref[0:1, :] * has_next
        nxt1 = xn_ref[1:2, :] * has_next
        xm1 = jnp.where(row == 0, prev, pltpu.roll(x, 1, 0))
        xp1 = jnp.where(row == tb - 1, nxt0, pltpu.roll(x, tb - 1, 0))
        xp2 = jnp.where(row == tb - 2, nxt0, jnp.where(row == tb - 1, nxt1, pltpu.roll(x, tb - 2, 0)))
        y = xm1 * cw_ref[0:1, :] + x * cw_ref[1:2, :] + xp1 * cw_ref[2:3, :] + xp2 * cw_ref[3:4, :] + cb_ref[...]
        yb = y.astype(BF16)
        for c in range(LRU_BLOCKS):
            sl = slice(c * LRU_BLOCK, (c + 1) * LRU_BLOCK)
            z = _dot(yb[:, sl], wg_ref[d, c]) + bg_ref[d, c]
            r = jax.nn.sigmoid(z[:, 0:LRU_BLOCK])
            g = jax.nn.sigmoid(z[:, LRU_BLOCK:2 * LRU_BLOCK])
            log_a = r * cv_ref[d:d + 1, sl]
            a_scr[d, :, sl] = jnp.exp(log_a)
            b_scr[d, :, sl] = jnp.sqrt(_neg_expm1(2.0 * log_a)) * (g * y[:, sl])

    gates(0, xf_ref, xfp_ref, xfn_ref, i)
    gates(1, xb_ref, xbp_ref, xbn_ref, nblk - 1 - i)

    row8 = lax.broadcasted_iota(jnp.int32, (SUBLANES, LRU_WIDTH), 0)
    nt = tb // SUBLANES

    def scan(d, out_ref):
        rev = d == 1

        def body(r, h):
            off = pl.multiple_of((nt - 1 - r if rev else r) * SUBLANES, SUBLANES)
            a8 = a_scr[d, pl.ds(off, SUBLANES), :]
            b8 = b_scr[d, pl.ds(off, SUBLANES), :]
            for s in (1, 2, 4):
                if rev:
                    ok = row8 < SUBLANES - s
                    sh = SUBLANES - s
                else:
                    ok = row8 >= s
                    sh = s
                a_sh = jnp.where(ok, pltpu.roll(a8, sh, 0), 1.0)
                b_sh = jnp.where(ok, pltpu.roll(b8, sh, 0), 0.0)
                b8 = a8 * b_sh + b8
                a8 = a8 * a_sh
            h8 = a8 * h + b8
            out_ref[pl.ds(off, SUBLANES), :] = h8
            return h8[0:1, :] if rev else h8[SUBLANES - 1:SUBLANES, :]

        st_scr[d:d + 1, :] = lax.fori_loop(0, nt, body, st_scr[d:d + 1, :])

    scan(0, hf_ref)
    scan(1, hb_ref)

    @pl.when(i == nblk - 1)
    def _():
        ht_ref[...] = st_scr[...]


def _rglru(gx, seq, batch, conv_w, conv_b, wg, bg, cv, h0):
    m = gx.shape[0]
    tb = min(256, seq)
    nblk = seq // tb
    hb8 = tb // SUBLANES
    last8 = m // SUBLANES - 1
    w = LRU_WIDTH

    def fidx(b, i):
        return b * nblk + i

    def bidx(b, i):
        return b * nblk + nblk - 1 - i

    def specs(idx):
        return [pl.BlockSpec((tb, w), lambda b, i: (idx(b, i), 1)),
                pl.BlockSpec((SUBLANES, w), lambda b, i: (jnp.maximum(idx(b, i) * hb8 - 1, 0), 1)),
                pl.BlockSpec((SUBLANES, w), lambda b, i: (jnp.minimum((idx(b, i) + 1) * hb8, last8), 1))]

    full = lambda shape: pl.BlockSpec(shape, lambda b, i: (0,) * len(shape))
    kern = functools.partial(_rglru_kernel, tb=tb, nblk=nblk)
    return pl.pallas_call(
        kern,
        grid=(batch, nblk),
        in_specs=specs(fidx) + specs(bidx) + [full(conv_w.shape), full((1, w)), full(wg.shape), full(bg.shape),
                                              full(cv.shape), pl.BlockSpec((None, 2, w), lambda b, i: (b, 0, 0))],
        out_specs=[pl.BlockSpec((tb, w), lambda b, i: (fidx(b, i), 0)),
                   pl.BlockSpec((tb, w), lambda b, i: (bidx(b, i), 0)),
                   pl.BlockSpec((None, 2, w), lambda b, i: (b, 0, 0))],
        out_shape=[jax.ShapeDtypeStruct((m, w), F32), jax.ShapeDtypeStruct((m, w), F32),
                   jax.ShapeDtypeStruct((batch, 2, w), F32)],
        scratch_shapes=[pltpu.VMEM((2, tb, w), F32), pltpu.VMEM((2, tb, w), F32), pltpu.VMEM((2, w), F32)],
        compiler_params=_cparams(("parallel", "arbitrary")),
        name="rglru",
    )(gx, gx, gx, gx, gx, gx, conv_w, conv_b.reshape(1, w), wg, bg, cv, h0)


def _hgrn_consts(c):
    t = np.arange(c)
    tinc = (t[None, :] <= t[:, None]).astype(np.float32)
    urev = (t[None, :] > t[:, None]).astype(np.float32)
    blocks = [tinc, urev]
    masks = []
    m = c // 2
    while m >= 1:
        mid = (t // (2 * m)) * (2 * m) + m
        right = t >= mid
        u = t[None, :]
        g = np.where(right[:, None], (u >= mid[:, None]) & (u <= t[:, None]), (u > t[:, None]) & (u < mid[:, None]))
        blocks.append(g.astype(np.float32))
        same = (t[:, None] // (2 * m)) == (t[None, :] // (2 * m))
        masks.append((same & right[:, None] & (~right)[None, :]).astype(np.float32))
        m //= 2
    masks.append(np.eye(c, dtype=np.float32))
    flip = lambda a: a[::-1, ::-1]
    ones = np.ones((16, c), np.float32)
    w = np.stack([np.concatenate(blocks + [ones], 0), np.concatenate([flip(b) for b in blocks] + [ones], 0)])
    cm = np.stack([np.stack(masks), np.stack([flip(a) for a in masks])])
    return w, cm


def _hgrn_kernel(qf_ref, ff_ref, vf_ref, qb_ref, fb_ref, vb_ref, lb_ref, wc_ref, cm_ref, s0_ref,
                 of_ref, ob_ref, st_ref, st_scr, *, c, levels, nchunk):
    i = pl.program_id(1)

    @pl.when(i == 0)
    def _():
        st_scr[...] = s0_ref[...]

    def one_dir(d, q_ref, f_ref, v_ref, o_ref):
        q = q_ref[...]
        q = q * jax.nn.sigmoid(q)
        lb = lb_ref[d:d + 1, :]
        f = lb + (1.0 - lb) * jax.nn.sigmoid(f_ref[...])
        kk = 1.0 - f
        g = jnp.log(f)
        g1 = g.astype(BF16)
        g2 = (g - g1.astype(F32)).astype(BF16)
        w = wc_ref[d]
        e = jnp.exp(_dot(w, g1) + _dot(w, g2))
        v = v_ref[...].astype(BF16)
        for h in range(HGRN_HEADS):
            sl = slice(h * LANES, (h + 1) * LANES)
            st = st_scr[d, h]
            qh, kh, vh = q[:, sl], kk[:, sl], v[:, sl]
            o = _dot_nt((qh * e[0:c, sl]).astype(BF16), st.astype(BF16))
            sc = cm_ref[d, levels] * _dot_nt(qh.astype(BF16), kh.astype(BF16))
            for l in range(levels):
                el = e[(2 + l) * c:(3 + l) * c, sl]
                sc = sc + cm_ref[d, l] * _dot_nt((qh * el).astype(BF16), (kh * el).astype(BF16))
            o_ref[:, sl] = o + _dot(sc.astype(BF16), vh)
            etot = e[(2 + levels) * c:(2 + levels) * c + 1, sl]
            st_scr[d, h] = st * etot + _dot_tn(vh, (kh * e[c:2 * c, sl]).astype(BF16))

    one_dir(0, qf_ref, ff_ref, vf_ref, of_ref)
    one_dir(1, qb_ref, fb_ref, vb_ref, ob_ref)

    @pl.when(i == nchunk - 1)
    def _():
        st_ref[...] = st_scr[...]


def _hgrn(z, seq, batch, lb, s0):
    m = z.shape[0]
    c = min(HGRN_CHUNK, seq)
    nchunk = seq // c
    levels = int(math.log2(c))
    wnp, cmnp = _hgrn_consts(c)
    wc = jnp.asarray(wnp, BF16)
    cm = jnp.asarray(cmnp, F32)
    w = HGRN_WIDTH

    def fidx(b, i):
        return b * nchunk + i

    def bidx(b, i):
        return b * nchunk + nchunk - 1 - i

    blk = lambda idx, col: pl.BlockSpec((c, w), lambda b, i: (idx(b, i), col))
    full = lambda shape: pl.BlockSpec(shape, lambda b, i: (0,) * len(shape))
    st_spec = pl.BlockSpec((None, 2, HGRN_HEADS, LANES, LANES), lambda b, i: (b, 0, 0, 0, 0))
    kern = functools.partial(_hgrn_kernel, c=c, levels=levels, nchunk=nchunk)
    return pl.pallas_call(
        kern,
        grid=(batch, nchunk),
        in_specs=[blk(fidx, 0), blk(fidx, 1), blk(fidx, 3), blk(bidx, 0), blk(bidx, 2), blk(bidx, 3),
                  full(lb.shape), full(wc.shape), full(cm.shape), st_spec],
        out_specs=[pl.BlockSpec((c, w), lambda b, i: (fidx(b, i), 0)),
                   pl.BlockSpec((c, w), lambda b, i: (bidx(b, i), 0)), st_spec],
        out_shape=[jax.ShapeDtypeStruct((m, w), F32), jax.ShapeDtypeStruct((m, w), F32),
                   jax.ShapeDtypeStruct((batch, 2, HGRN_HEADS, LANES, LANES), F32)],
        scratch_shapes=[pltpu.VMEM((2, HGRN_HEADS, LANES, LANES), F32)],
        compiler_params=_cparams(("parallel", "arbitrary")),
        name="hgrn2",
    )(z, z, z, z, z, z, lb, wc, cm, s0)


def _outproj_kernel(x_ref, p0_ref, p1_ref, g_ref, att_ref, nw_ref, w_ref, gt_ref, o_ref, *, mode):
    half = w_ref.shape[0] // 2
    s = p0_ref[...] + p1_ref[...]
    g = g_ref[...]
    if mode == "ab":
        a = s * jax.nn.gelu(g, approximate=True)
    else:
        parts = []
        for h in range(HGRN_HEADS):
            sl = slice(h * LANES, (h + 1) * LANES)
            parts.append(_rms(s[:, sl]) * nw_ref[...])
        a = jnp.concatenate(parts, axis=-1) * (g * jax.nn.sigmoid(g))
    acc = _dot(a.astype(BF16), w_ref[0:half, :]) + _dot(att_ref[...], w_ref[half:2 * half, :])
    o_ref[...] = x_ref[...] + gt_ref[...] * acc


def _outproj(x2d, seq, p0, p1, gsrc, gcol, att, head_norm_w, w_out, mod, mode):
    m, d = x2d.shape
    tm = min(256, seq)
    tpb = seq // tm if mod.shape[0] > 1 else m
    hw = w_out.shape[0] // 2
    kern = functools.partial(_outproj_kernel, mode=mode)
    return pl.pallas_call(
        kern,
        grid=(m // tm,),
        in_specs=[pl.BlockSpec((tm, d), lambda i: (i, 0)),
                  pl.BlockSpec((tm, hw), lambda i: (i, 0)),
                  pl.BlockSpec((tm, hw), lambda i: (i, 0)),
                  pl.BlockSpec((tm, hw), lambda i: (i, gcol)),
                  pl.BlockSpec((tm, hw), lambda i: (i, 0)),
                  pl.BlockSpec((1, LANES), lambda i: (0, 0)),
                  pl.BlockSpec(w_out.shape, lambda i: (0, 0)),
                  pl.BlockSpec((None, None, 1, d), lambda i: (i // tpb, 2, 0, 0))],
        out_specs=pl.BlockSpec((tm, d), lambda i: (i, 0)),
        out_shape=jax.ShapeDtypeStruct((m, d), F32),
        compiler_params=_cparams(("parallel",)),
        name="outproj_" + mode,
    )(x2d, p0, p1, gsrc, att, head_norm_w, w_out, mod)


def _ffn_kernel(x_ref, nw_ref, sh_ref, sc_ref, gt_ref, wg_ref, wu_ref, wd_ref, fw_ref, o_ref, hn_ref, *, final):
    j = pl.program_id(1)

    @pl.when(j == 0)
    def _():
        h = _rms(x_ref[...]) * nw_ref[...]
        hn_ref[...] = (h * (1.0 + sc_ref[...]) + sh_ref[...]).astype(BF16)
        o_ref[...] = jnp.zeros(o_ref.shape, F32)

    hn = hn_ref[...]
    g = _dot(hn, wg_ref[...])
    u = _dot(hn, wu_ref[...])
    a = (g * jax.nn.sigmoid(g) * u).astype(BF16)
    o_ref[...] += _dot(a, wd_ref[...])

    @pl.when(j == pl.num_programs(1) - 1)
    def _():
        y = x_ref[...] + gt_ref[...] * o_ref[...]
        if final:
            y = _rms(y) * fw_ref[...]
        o_ref[...] = y


def _ffn(x2d, seq, norm_w, mod, w_gate, w_up, w_down, final_w, final):
    m, d = x2d.shape
    f = w_gate.shape[1]
    tm = min(512, seq)
    tf = 512
    tpb = seq // tm if mod.shape[0] > 1 else m
    mspec = lambda k: pl.BlockSpec((None, None, 1, d), lambda i, j: (i // tpb, k, 0, 0))
    kern = functools.partial(_ffn_kernel, final=final)
    return pl.pallas_call(
        kern,
        grid=(m // tm, f // tf),
        in_specs=[pl.BlockSpec((tm, d), lambda i, j: (i, 0)),
                  pl.BlockSpec((1, d), lambda i, j: (0, 0)),
                  mspec(3), mspec(4), mspec(5),
                  pl.BlockSpec((d, tf), lambda i, j: (0, j)),
                  pl.BlockSpec((d, tf), lambda i, j: (0, j)),
                  pl.BlockSpec((tf, d), lambda i, j: (j, 0)),
                  pl.BlockSpec((1, d), lambda i, j: (0, 0))],
        out_specs=pl.BlockSpec((tm, d), lambda i, j: (i, 0)),
        out_shape=jax.ShapeDtypeStruct((m, d), F32),
        scratch_shapes=[pltpu.VMEM((tm, d), BF16)],
        compiler_params=_cparams(("parallel", "arbitrary")),
        name="ffn",
    )(x2d, norm_w.reshape(1, d), mod, mod, mod, w_gate, w_up, w_down, final_w.reshape(1, d))


def _rope_tables(rows, head_dim):
    n_freq = head_dim // 4
    half = head_dim // 2
    row = jnp.repeat(jnp.arange(rows, dtype=F32), GRID_W)
    col = jnp.tile(jnp.arange(GRID_W, dtype=F32), rows)
    inv = ROPE_THETA ** (-jnp.arange(n_freq, dtype=F32) / n_freq)
    ang = jnp.concatenate([row[:, None] * inv, col[:, None] * inv], axis=-1)
    cos, sin = jnp.cos(ang), jnp.sin(ang)
    reps = LANES // head_dim
    zero = jnp.zeros_like(sin)
    cos_t = jnp.tile(jnp.concatenate([cos, cos], -1), (1, reps))
    if half * 2 == LANES:
        return cos_t, jnp.concatenate([-sin, sin], -1), None
    sin_a = jnp.tile(jnp.concatenate([-sin, zero], -1), (1, reps))
    sin_b = jnp.tile(jnp.concatenate([zero, sin], -1), (1, reps))
    return cos_t, sin_a, sin_b


def _identity_rope(t, head_dim):
    one = jnp.ones((t, LANES), F32)
    zero = jnp.zeros((t, LANES), F32)
    return (one, zero, None) if head_dim == LANES else (one, zero, zero)


def kernel(x, c, ctx, c_ctx, mod_w, mod_b, norm_mix_w, norm_ffn_w, ffn_w_gate, ffn_w_up, ffn_w_down, ab_w_in, ab_w_out, lru_conv_w, lru_conv_b, lru_wa, lru_ba, lru_wx, lru_bx, lru_lambda, diff_lq1, diff_lk1, diff_lq2, diff_lk2, diff_subln_w, cd_w_in, cd_w_out, hgrn_lb_logits, hgrn_norm_w, gqa_q_norm_w, gqa_k_norm_w, final_norm_w):
    batch, seq, d = x.shape
    clen = ctx.shape[1]
    depth = mod_w.shape[0]
    rows = seq // GRID_W

    cc = jnp.zeros((SUBLANES, d), F32).at[0:batch].set(c).at[batch].set(c_ctx)
    mods = _modulation(cc, mod_w, mod_b)
    lb_cum = jnp.cumsum(jax.nn.softmax(hgrn_lb_logits.astype(F32), axis=1), axis=1)

    xl = x.reshape(batch * seq, d)
    xc = ctx.reshape(batch * clen, d)

    for l in range(depth):
        last = l == depth - 1
        m_lat = mods[l, 0:batch].reshape(batch, N_MOD, 1, d)
        m_ctx = mods[l, batch:batch + 1].reshape(1, N_MOD, 1, d)
        streams = ((xl, seq, m_lat), (xc, clen, m_ctx))
        if l % 2 == 0:
            e = l // 2
            lambda_init = 0.8 - 0.6 * math.exp(-0.3 * l)
            w_in = ab_w_in[e]
            qscale = DIFF_HEAD_DIM ** -0.5 * LOG2E
            w_gx = w_in[:, 0:2048].astype(BF16)
            w_qk = jnp.concatenate([w_in[:, 2048:3072] * qscale, w_in[:, 3072:4096]], axis=1).astype(BF16)
            w_v = w_in[:, 4096:5120].astype(BF16)
            ropes = (_rope_tables(rows, DIFF_HEAD_DIM), _identity_rope(clen, DIFF_HEAD_DIM))
            proj = []
            for (xs, t, md), rp in zip(streams, ropes):
                gx = _inproj(xs, t, norm_mix_w[l], md, w_gx, F32, 1024)
                qk = _inproj(xs, t, norm_mix_w[l], md, w_qk, BF16, 1024, rope=rp, rope_half=DIFF_HEAD_DIM // 2)
                v = _inproj(xs, t, norm_mix_w[l], md, w_v, BF16, 1024)
                proj.append((gx, qk.reshape(batch, t, 2048), v.reshape(batch, t, 1024)))
            (gx_l, qk_l, v_l), (gx_c, qk_c, v_c) = proj
            wg = jnp.concatenate([lru_wa[e], lru_wx[e]], axis=-1).astype(BF16)
            bg = jnp.concatenate([lru_ba[e].reshape(2, LRU_BLOCKS, 1, LRU_BLOCK),
                                  lru_bx[e].reshape(2, LRU_BLOCKS, 1, LRU_BLOCK)], axis=-1)
            cv = -LRU_C * jax.nn.softplus(-lru_lambda[e].astype(F32))
            h0 = jnp.zeros((batch, 2, LRU_WIDTH), F32)
            hf_c, hb_c, h_ctx = _rglru(gx_c, clen, batch, lru_conv_w[e], lru_conv_b[e], wg, bg, cv, h0)
            hf_l, hb_l, _ = _rglru(gx_l, seq, batch, lru_conv_w[e], lru_conv_b[e], wg, bg, cv, h_ctx)
            lam = (jnp.exp(jnp.sum(diff_lq1[e].astype(F32) * diff_lk1[e].astype(F32)))
                   - jnp.exp(jnp.sum(diff_lq2[e].astype(F32) * diff_lk2[e].astype(F32))) + lambda_init)
            lam_vec = jnp.full((1, LANES), lam, F32)
            sw = diff_subln_w[e].reshape(1, LANES)
            d_l = _diff_attn(qk_l, v_l, qk_c, v_c, lam_vec, sw, 1.0 - lambda_init, batch, True)
            w_out = ab_w_out[e].astype(BF16)
            dummy_nw = jnp.ones((1, LANES), F32)
            xl = _outproj(xl, seq, hf_l, hb_l, gx_l, 0, d_l.reshape(batch * seq, 1024), dummy_nw, w_out, m_lat, "ab")
            if not last:
                d_c = _diff_attn(qk_c, v_c, qk_c, v_c, lam_vec, sw, 1.0 - lambda_init, batch, False)
                xc = _outproj(xc, clen, hf_c, hb_c, gx_c, 0, d_c.reshape(batch * clen, 1024), dummy_nw, w_out,
                              m_ctx, "ab")
        else:
            o = l // 2
            lb = lb_cum[:, l] - lb_cum[:, 0]
            w_in = cd_w_in[o]
            w_z = w_in[:, 0:5120].astype(BF16)
            w_qk = w_in[:, 5120:6400].astype(BF16)
            w_v = w_in[:, 6400:6656].astype(BF16)
            qscale = GQA_HEAD_DIM ** -0.5 * LOG2E
            chunk_w = jnp.concatenate([jnp.tile(gqa_q_norm_w[o] * qscale, GQA_HEADS),
                                       jnp.tile(gqa_k_norm_w[o], GQA_KV_HEADS)]).reshape(1, 1280)
            ropes = (_rope_tables(rows, GQA_HEAD_DIM), _identity_rope(clen, GQA_HEAD_DIM))
            proj = []
            for (xs, t, md), rp in zip(streams, ropes):
                z = _inproj(xs, t, norm_mix_w[l], md, w_z, F32, 1024)
                qk = _inproj(xs, t, norm_mix_w[l], md, w_qk, BF16, 1280, chunk_w=chunk_w, norm_chunks=10,
                             rope=rp[0:2], rope_half=GQA_HEAD_DIM // 2)
                v = _inproj(xs, t, norm_mix_w[l], md, w_v, BF16, 256)
                proj.append((z, qk.reshape(batch, t, 1280), v.reshape(batch, t, 256)))
            (z_l, qk_l, v_l), (z_c, qk_c, v_c) = proj
            s0 = jnp.zeros((batch, 2, HGRN_HEADS, LANES, LANES), F32)
            of_c, ob_c, s_ctx = _hgrn(z_c, clen, batch, lb, s0)
            of_l, ob_l, _ = _hgrn(z_l, seq, batch, lb, s_ctx)
            att_l = _gqa_attn(qk_l, v_l, qk_c, v_c, batch, True)
            w_out = cd_w_out[o].astype(BF16)
            hnw = hgrn_norm_w[o].reshape(1, LANES)
            xl = _outproj(xl, seq, of_l, ob_l, z_l, 4, att_l.reshape(batch * seq, 1024), hnw, w_out, m_lat, "cd")
            if not last:
                att_c = _gqa_attn(qk_c, v_c, qk_c, v_c, batch, False)
                xc = _outproj(xc, clen, of_c, ob_c, z_c, 4, att_c.reshape(batch * clen, 1024), hnw, w_out,
                              m_ctx, "cd")
        wgt, wup, wdn = ffn_w_gate[l].astype(BF16), ffn_w_up[l].astype(BF16), ffn_w_down[l].astype(BF16)
        xl = _ffn(xl, seq, norm_ffn_w[l], m_lat, wgt, wup, wdn, final_norm_w, last)
        if not last:
            xc = _ffn(xc, clen, norm_ffn_w[l], m_ctx, wgt, wup, wdn, final_norm_w, False)

    return xl.reshape(batch, seq, d)
```

```python
import functools
import math

import numpy as np
import jax
import jax.numpy as jnp
from jax import lax
from jax.experimental import pallas as pl
from jax.experimental.pallas import tpu as pltpu

F32 = jnp.float32
BF16 = jnp.bfloat16

GRID_W = 64
NORM_EPS = 1e-6
ROPE_THETA = 10000.0
N_MOD = 6
LRU_WIDTH = 1024
LRU_BLOCKS = 8
LRU_BLOCK = 128
LRU_C = 8.0
DIFF_HEADS = 8
DIFF_HEAD_DIM = 64
HGRN_HEADS = 8
HGRN_WIDTH = 1024
GQA_HEADS = 8
GQA_KV_HEADS = 2
GQA_REP = 4
GQA_HEAD_DIM = 128
LOG2E = 1.4426950408889634

LANES = 128
SUBLANES = 8
VMEM_LIMIT = 56 * 1024 * 1024

HGRN_CHUNK = 64


def _cparams(sem):
    return pltpu.CompilerParams(dimension_semantics=sem, vmem_limit_bytes=VMEM_LIMIT)


def _dot(a, b):
    return jnp.dot(a, b, preferred_element_type=F32)


def _dot_nt(a, b):
    return lax.dot_general(a, b, (((1,), (1,)), ((), ())), preferred_element_type=F32)


def _dot_tn(a, b):
    return lax.dot_general(a, b, (((0,), (0,)), ((), ())), preferred_element_type=F32)


def _neg_expm1(y):
    series = -y * (1.0 + 0.5 * y * (1.0 + (1.0 / 3.0) * y * (1.0 + 0.25 * y)))
    return jnp.where(y > -0.03, series, 1.0 - jnp.exp(y))


def _rms(x):
    return x * lax.rsqrt(jnp.mean(x * x, axis=-1, keepdims=True) + NORM_EPS)


def _mod_kernel(c_ref, w_ref, b_ref, o_ref):
    c = c_ref[...]
    a = c * jax.nn.sigmoid(c)
    o_ref[...] = jnp.dot(a, w_ref[...], preferred_element_type=F32,
                         precision=lax.Precision.HIGHEST) + b_ref[...]


def _modulation(cc, mod_w, mod_b):
    depth, d, n = mod_w.shape
    tn = 1024
    return pl.pallas_call(
        _mod_kernel,
        grid=(depth, n // tn),
        in_specs=[pl.BlockSpec((SUBLANES, d), lambda l, j: (0, 0)),
                  pl.BlockSpec((None, d, tn), lambda l, j: (l, 0, j)),
                  pl.BlockSpec((None, 1, tn), lambda l, j: (l, 0, j))],
        out_specs=pl.BlockSpec((None, SUBLANES, tn), lambda l, j: (l, 0, j)),
        out_shape=jax.ShapeDtypeStruct((depth, SUBLANES, n), F32),
        compiler_params=_cparams(("parallel", "parallel")),
        name="modulation",
    )(cc, mod_w, mod_b.reshape(depth, 1, n))


def _inproj_kernel(*refs, n_chunks, norm_chunks, rope_half):
    it = iter(refs)
    x_ref, nw_ref, sh_ref, sc_ref, w_ref = next(it), next(it), next(it), next(it), next(it)
    cw_ref = next(it) if norm_chunks else None
    if rope_half:
        cos_ref, sa_ref = next(it), next(it)
        sb_ref = next(it) if rope_half * 2 != LANES else None
    o_ref, xn_ref = next(it), next(it)

    @pl.when(pl.program_id(1) == 0)
    def _():
        x = x_ref[...]
        h = _rms(x) * nw_ref[...]
        xn_ref[...] = (h * (1.0 + sc_ref[...]) + sh_ref[...]).astype(BF16)

    acc = _dot(xn_ref[...], w_ref[...])
    for c in range(n_chunks):
        sl = slice(c * LANES, (c + 1) * LANES)
        y = acc[:, sl]
        if c < norm_chunks:
            y = _rms(y) * cw_ref[:, sl]
        if rope_half:
            if rope_half * 2 == LANES:
                y = y * cos_ref[...] + pltpu.roll(y, rope_half, 1) * sa_ref[...]
            else:
                y = (y * cos_ref[...] + pltpu.roll(y, LANES - rope_half, 1) * sa_ref[...]
                     + pltpu.roll(y, rope_half, 1) * sb_ref[...])
        o_ref[:, sl] = y.astype(o_ref.dtype)


def _inproj(x2d, seq, norm_w, mod, w, out_dtype, tn, chunk_w=None, norm_chunks=0, rope=None, rope_half=0):
    m, d = x2d.shape
    n = w.shape[1]
    tm = min(1024, seq)
    tpb = seq // tm if mod.shape[0] > 1 else m
    in_specs = [pl.BlockSpec((tm, d), lambda i, j: (i, 0)),
                pl.BlockSpec((1, d), lambda i, j: (0, 0)),
                pl.BlockSpec((None, None, 1, d), lambda i, j: (i // tpb, 0, 0, 0)),
                pl.BlockSpec((None, None, 1, d), lambda i, j: (i // tpb, 1, 0, 0)),
                pl.BlockSpec((d, tn), lambda i, j: (0, j))]
    args = [x2d, norm_w.reshape(1, d), mod, mod, w]
    if norm_chunks:
        in_specs.append(pl.BlockSpec((1, tn), lambda i, j: (0, j)))
        args.append(chunk_w)
    if rope_half:
        spt = seq // tm
        for t in rope:
            in_specs.append(pl.BlockSpec((tm, LANES), lambda i, j: (i % spt, 0)))
            args.append(t)
    kern = functools.partial(_inproj_kernel, n_chunks=tn // LANES, norm_chunks=norm_chunks, rope_half=rope_half)
    return pl.pallas_call(
        kern,
        grid=(m // tm, n // tn),
        in_specs=in_specs,
        out_specs=pl.BlockSpec((tm, tn), lambda i, j: (i, j)),
        out_shape=jax.ShapeDtypeStruct((m, n), out_dtype),
        scratch_shapes=[pltpu.VMEM((tm, d), BF16)],
        compiler_params=_cparams(("parallel", "arbitrary")),
        name="inproj",
    )(*args)


ACC_ROWS = LANES + 16


def _to_bf16_t(x):
    return x.astype(F32).T.astype(BF16)


def _transpose_values(v_ref, vt_ref, vT_ref, vtT_ref, *, tk):
    for c in range(v_ref.shape[0] // tk):
        vT_ref[c, 0:LANES, :] = _to_bf16_t(v_ref[c * tk:(c + 1) * tk, :])
        vT_ref[c, LANES:ACC_ROWS, :] = jnp.ones((ACC_ROWS - LANES, tk), BF16)
    vtT_ref[0:LANES, :] = _to_bf16_t(vt_ref[...])
    vtT_ref[LANES:ACC_ROWS, :] = jnp.ones((ACC_ROWS - LANES, vt_ref.shape[0]), BF16)


def _flash(qT_ref, k_ref, vT_ref, kt_ref, vtT_ref, s0_ref, s1_ref, m_ref, acc_ref, *, tk, n_main, has_tail):
    m_ref[...] = jnp.full(m_ref.shape, -jnp.inf, F32)
    acc_ref[...] = jnp.zeros(acc_ref.shape, F32)

    def update(s_ref, vT):
        m_prev = m_ref[...]
        m_new = jnp.maximum(m_prev, jnp.max(s_ref[...], axis=0, keepdims=True))
        alpha = jnp.exp2(m_prev - m_new)
        p = jnp.exp2(s_ref[...] - m_new).astype(BF16)
        acc_ref[...] = alpha * acc_ref[...] + _dot(vT, p)
        m_ref[...] = m_new

    def scores(c):
        off = pl.multiple_of(c * tk, tk)
        return _dot(k_ref[pl.ds(off, tk), :], qT_ref[...])

    if has_tail:
        st_ref = s1_ref.at[0:kt_ref.shape[0], :]
        st_ref[...] = _dot(kt_ref[...], qT_ref[...])
    s0_ref[...] = scores(0)
    if has_tail:
        update(st_ref, vtT_ref[...])

    if n_main % 2:
        def body(c, carry):
            s0_ref[...] = scores(c)
            update(s0_ref, vT_ref[c])
            return carry

        lax.fori_loop(0, n_main, body, 0)
    else:
        def body(c2, carry):
            c0 = 2 * c2
            s1_ref[...] = scores(c0 + 1)
            update(s0_ref, vT_ref[c0])
            s0_ref[...] = scores(jnp.minimum(c0 + 2, n_main - 1))
            update(s1_ref, vT_ref[c0 + 1])
            return carry

        lax.fori_loop(0, n_main // 2, body, 0, unroll=4 if n_main % 8 == 0 else 1)
    return acc_ref[0:LANES, :] / acc_ref[LANES:LANES + 1, :]


def _flash_scratch(rows, tk, t, tt):
    return [pltpu.VMEM((LANES, rows), BF16), pltpu.VMEM((t // tk, ACC_ROWS, tk), BF16),
            pltpu.VMEM((ACC_ROWS, tt), BF16), pltpu.VMEM((tk, rows), F32), pltpu.VMEM((tk, rows), F32),
            pltpu.VMEM((1, rows), F32), pltpu.VMEM((ACC_ROWS, rows), F32)]


def _diff_attn_kernel(q_ref, k_ref, v_ref, kt_ref, vt_ref, lam_ref, sw_ref, o_ref,
                      qT_ref, vT_ref, vtT_ref, s0_ref, s1_ref, m_ref, acc_ref,
                      *, tq, tk, n_main, has_tail, post_scale):
    @pl.when(pl.program_id(2) == 0)
    def _():
        _transpose_values(v_ref, vt_ref, vT_ref, vtT_ref, tk=tk)

    q = q_ref[...].astype(F32)
    lane = lax.broadcasted_iota(jnp.int32, q.shape, 1)
    qT_ref[:, 0:tq] = jnp.where(lane < DIFF_HEAD_DIM, q, 0.0).T.astype(BF16)
    qT_ref[:, tq:2 * tq] = jnp.where(lane >= DIFF_HEAD_DIM, q, 0.0).T.astype(BF16)
    o = _flash(qT_ref, k_ref, vT_ref, kt_ref, vtT_ref, s0_ref, s1_ref, m_ref, acc_ref,
               tk=tk, n_main=n_main, has_tail=has_tail).T
    y = o[0:tq, :] - lam_ref[...] * o[tq:2 * tq, :]
    y = _rms(y) * sw_ref[...] * post_scale
    o_ref[...] = y.astype(o_ref.dtype)


def _diff_attn(qk, v, qk_tail, v_tail, lam_vec, subln_w, post_scale, batch, has_tail):
    t = qk.shape[1]
    tq = min(512, t)
    tk = min(1024, t)
    kern = functools.partial(_diff_attn_kernel, tq=tq, tk=tk, n_main=t // tk, has_tail=has_tail,
                             post_scale=post_scale)
    tt = qk_tail.shape[1]
    return pl.pallas_call(
        kern,
        grid=(batch, DIFF_HEADS, t // tq),
        in_specs=[pl.BlockSpec((None, tq, LANES), lambda b, h, i: (b, i, h)),
                  pl.BlockSpec((None, t, LANES), lambda b, h, i: (b, 0, DIFF_HEADS + h)),
                  pl.BlockSpec((None, t, LANES), lambda b, h, i: (b, 0, h)),
                  pl.BlockSpec((None, tt, LANES), lambda b, h, i: (b, 0, DIFF_HEADS + h)),
                  pl.BlockSpec((None, tt, LANES), lambda b, h, i: (b, 0, h)),
                  pl.BlockSpec((1, LANES), lambda b, h, i: (0, 0)),
                  pl.BlockSpec((1, LANES), lambda b, h, i: (0, 0))],
        out_specs=pl.BlockSpec((None, tq, LANES), lambda b, h, i: (b, i, h)),
        out_shape=jax.ShapeDtypeStruct((batch, t, DIFF_HEADS * LANES), BF16),
        scratch_shapes=_flash_scratch(2 * tq, tk, t, tt),
        compiler_params=_cparams(("parallel", "parallel", "arbitrary")),
        name="diff_attn",
    )(qk, qk, v, qk_tail, v_tail, lam_vec, subln_w)


def _gqa_kernel(q_ref, k_ref, v_ref, kt_ref, vt_ref, o_ref, qT_ref, vT_ref, vtT_ref, s0_ref, s1_ref, m_ref, acc_ref,
                *, tq, tk, n_main, has_tail):
    @pl.when(pl.program_id(2) == 0)
    def _():
        _transpose_values(v_ref, vt_ref, vT_ref, vtT_ref, tk=tk)

    for r in range(GQA_REP):
        qT_ref[:, r * tq:(r + 1) * tq] = _to_bf16_t(q_ref[:, r * LANES:(r + 1) * LANES])
    o = _flash(qT_ref, k_ref, vT_ref, kt_ref, vtT_ref, s0_ref, s1_ref, m_ref, acc_ref,
               tk=tk, n_main=n_main, has_tail=has_tail).T
    for r in range(GQA_REP):
        o_ref[:, r * LANES:(r + 1) * LANES] = o[r * tq:(r + 1) * tq, :].astype(o_ref.dtype)


def _gqa_attn(qk, v, qk_tail, v_tail, batch, has_tail):
    t = qk.shape[1]
    tq = min(256, t)
    tk = min(1024, t)
    gw = GQA_REP * LANES
    kern = functools.partial(_gqa_kernel, tq=tq, tk=tk, n_main=t // tk, has_tail=has_tail)
    tt = qk_tail.shape[1]
    return pl.pallas_call(
        kern,
        grid=(batch, GQA_KV_HEADS, t // tq),
        in_specs=[pl.BlockSpec((None, tq, gw), lambda b, g, i: (b, i, g)),
                  pl.BlockSpec((None, t, LANES), lambda b, g, i: (b, 0, GQA_HEADS + g)),
                  pl.BlockSpec((None, t, LANES), lambda b, g, i: (b, 0, g)),
                  pl.BlockSpec((None, tt, LANES), lambda b, g, i: (b, 0, GQA_HEADS + g)),
                  pl.BlockSpec((None, tt, LANES), lambda b, g, i: (b, 0, g))],
        out_specs=pl.BlockSpec((None, tq, gw), lambda b, g, i: (b, i, g)),
        out_shape=jax.ShapeDtypeStruct((batch, t, GQA_HEADS * LANES), BF16),
        scratch_shapes=_flash_scratch(GQA_REP * tq, tk, t, tt),
        compiler_params=_cparams(("parallel", "parallel", "arbitrary")),
        name="gqa_attn",
    )(qk, qk, v, qk_tail, v_tail)


def _rglru_kernel(xf_ref, xfp_ref, xfn_ref, xb_ref, xbp_ref, xbn_ref, cw_ref, cb_ref, wg_ref, bg_ref, cv_ref,
                  h0_ref, hf_ref, hb_ref, ht_ref, a_scr, b_scr, st_scr, *, tb, nblk):
    i = pl.program_id(1)

    @pl.when(i == 0)
    def _():
        st_scr[...] = h0_ref[...]

    row = lax.broadcasted_iota(jnp.int32, (tb, LRU_WIDTH), 0)

    def gates(d, x_ref, xp_ref, xn_ref, blk):
        x = x_ref[...]
        prev = xp_ref[SUBLANES - 1:SUBLANES, :] * (blk > 0).astype(F32)
        has_next = (blk < nblk - 1).astype(F32)
        nxt0 = xn_ref[0:1, :] * has_next
        nxt1 = xn_ref[1:2, :] * has_next
        xm1 = jnp.where(row == 0, prev, pltpu.roll(x, 1, 0))
        xp1 = jnp.where(row == tb - 1, nxt0, pltpu.roll(x, tb - 1, 0))
        xp2 = jnp.where(row == tb - 2, nxt0, jnp.where(row == tb - 1, nxt1, pltpu.roll(x, tb - 2, 0)))
        y = xm1 * cw_ref[0:1, :] + x * cw_ref[1:2, :] + xp1 * cw_ref[2:3, :] + xp2 * cw_ref[3:4, :] + cb_ref[...]
        yb = y.astype(BF16)
        for c in range(LRU_BLOCKS):
            sl = slice(c * LRU_BLOCK, (c + 1) * LRU_BLOCK)
            z = _dot(yb[:, sl], wg_ref[d, c]) + bg_ref[d, c]
            r = jax.nn.sigmoid(z[:, 0:LRU_BLOCK])
            g = jax.nn.sigmoid(z[:, LRU_BLOCK:2 * LRU_BLOCK])
            log_a = r * cv_ref[d:d + 1, sl]
            a_scr[d, :, sl] = jnp.exp(log_a)
            b_scr[d, :, sl] = jnp.sqrt(_neg_expm1(2.0 * log_a)) * (g * y[:, sl])

    gates(0, xf_ref, xfp_ref, xfn_ref, i)
    gates(1, xb_ref, xbp_ref, xbn_ref, nblk - 1 - i)

    row8 = lax.broadcasted_iota(jnp.int32, (SUBLANES, LRU_WIDTH), 0)
    nt = tb // SUBLANES

    def scan(d, out_ref):
        rev = d == 1

        def body(r, h):
            off = pl.multiple_of((nt - 1 - r if rev else r) * SUBLANES, SUBLANES)
            a8 = a_scr[d, pl.ds(off, SUBLANES), :]
            b8 = b_scr[d, pl.ds(off, SUBLANES), :]
            for s in (1, 2, 4):
                if rev:
                    ok = row8 < SUBLANES - s
                    sh = SUBLANES - s
                else:
                    ok = row8 >= s
                    sh = s
                a_sh = jnp.where(ok, pltpu.roll(a8, sh, 0), 1.0)
                b_sh = jnp.where(ok, pltpu.roll(b8, sh, 0), 0.0)
                b8 = a8 * b_sh + b8
                a8 = a8 * a_sh
            h8 = a8 * h + b8
            out_ref[pl.ds(off, SUBLANES), :] = h8
            return h8[0:1, :] if rev else h8[SUBLANES - 1:SUBLANES, :]

        st_scr[d:d + 1, :] = lax.fori_loop(0, nt, body, st_scr[d:d + 1, :])

    scan(0, hf_ref)
    scan(1, hb_ref)

    @pl.when(i == nblk - 1)
    def _():
        ht_ref[...] = st_scr[...]


def _rglru(gx, seq, batch, conv_w, conv_b, wg, bg, cv, h0):
    m = gx.shape[0]
    tb = min(256, seq)
    nblk = seq // tb
    hb8 = tb // SUBLANES
    last8 = m // SUBLANES - 1
    w = LRU_WIDTH

    def fidx(b, i):
        return b * nblk + i

    def bidx(b, i):
        return b * nblk + nblk - 1 - i

    def specs(idx):
        return [pl.BlockSpec((tb, w), lambda b, i: (idx(b, i), 1)),
                pl.BlockSpec((SUBLANES, w), lambda b, i: (jnp.maximum(idx(b, i) * hb8 - 1, 0), 1)),
                pl.BlockSpec((SUBLANES, w), lambda b, i: (jnp.minimum((idx(b, i) + 1) * hb8, last8), 1))]

    full = lambda shape: pl.BlockSpec(shape, lambda b, i: (0,) * len(shape))
    kern = functools.partial(_rglru_kernel, tb=tb, nblk=nblk)
    return pl.pallas_call(
        kern,
        grid=(batch, nblk),
        in_specs=specs(fidx) + specs(bidx) + [full(conv_w.shape), full((1, w)), full(wg.shape), full(bg.shape),
                                              full(cv.shape), pl.BlockSpec((None, 2, w), lambda b, i: (b, 0, 0))],
        out_specs=[pl.BlockSpec((tb, w), lambda b, i: (fidx(b, i), 0)),
                   pl.BlockSpec((tb, w), lambda b, i: (bidx(b, i), 0)),
                   pl.BlockSpec((None, 2, w), lambda b, i: (b, 0, 0))],
        out_shape=[jax.ShapeDtypeStruct((m, w), F32), jax.ShapeDtypeStruct((m, w), F32),
                   jax.ShapeDtypeStruct((batch, 2, w), F32)],
        scratch_shapes=[pltpu.VMEM((2, tb, w), F32), pltpu.VMEM((2, tb, w), F32), pltpu.VMEM((2, w), F32)],
        compiler_params=_cparams(("parallel", "arbitrary")),
        name="rglru",
    )(gx, gx, gx, gx, gx, gx, conv_w, conv_b.reshape(1, w), wg, bg, cv, h0)


def _hgrn_consts(c):
    t = np.arange(c)
    tinc = (t[None, :] <= t[:, None]).astype(np.float32)
    urev = (t[None, :] > t[:, None]).astype(np.float32)
    blocks = [tinc, urev]
    masks = []
    m = c // 2
    while m >= 1:
        mid = (t // (2 * m)) * (2 * m) + m
        right = t >= mid
        u = t[None, :]
        g = np.where(right[:, None], (u >= mid[:, None]) & (u <= t[:, None]), (u > t[:, None]) & (u < mid[:, None]))
        blocks.append(g.astype(np.float32))
        same = (t[:, None] // (2 * m)) == (t[None, :] // (2 * m))
        masks.append((same & right[:, None] & (~right)[None, :]).astype(np.float32))
        m //= 2
    masks.append(np.eye(c, dtype=np.float32))
    flip = lambda a: a[::-1, ::-1]
    ones = np.ones((16, c), np.float32)
    w = np.stack([np.concatenate(blocks + [ones], 0), np.concatenate([flip(b) for b in blocks] + [ones], 0)])
    cm = np.stack([np.stack(masks), np.stack([flip(a) for a in masks])])
    return w, cm


def _hgrn_kernel(qf_ref, ff_ref, vf_ref, qb_ref, fb_ref, vb_ref, lb_ref, wc_ref, cm_ref, s0_ref,
                 of_ref, ob_ref, st_ref, st_scr, *, c, levels, nchunk):
    i = pl.program_id(1)

    @pl.when(i == 0)
    def _():
        st_scr[...] = s0_ref[...]

    def one_dir(d, q_ref, f_ref, v_ref, o_ref):
        q = q_ref[...]
        q = q * jax.nn.sigmoid(q)
        lb = lb_ref[d:d + 1, :]
        f = lb + (1.0 - lb) * jax.nn.sigmoid(f_ref[...])
        kk = 1.0 - f
        g = jnp.log(f)
        g1 = g.astype(BF16)
        g2 = (g - g1.astype(F32)).astype(BF16)
        w = wc_ref[d]
        e = jnp.exp(_dot(w, g1) + _dot(w, g2))
        v = v_ref[...].astype(BF16)
        for h in range(HGRN_HEADS):
            sl = slice(h * LANES, (h + 1) * LANES)
            st = st_scr[d, h]
            qh, kh, vh = q[:, sl], kk[:, sl], v[:, sl]
            o = _dot_nt((qh * e[0:c, sl]).astype(BF16), st.astype(BF16))
            sc = cm_ref[d, levels] * _dot_nt(qh.astype(BF16), kh.astype(BF16))
            for l in range(levels):
                el = e[(2 + l) * c:(3 + l) * c, sl]
                sc = sc + cm_ref[d, l] * _dot_nt((qh * el).astype(BF16), (kh * el).astype(BF16))
            o_ref[:, sl] = o + _dot(sc.astype(BF16), vh)
            etot = e[(2 + levels) * c:(2 + levels) * c + 1, sl]
            st_scr[d, h] = st * etot + _dot_tn(vh, (kh * e[c:2 * c, sl]).astype(BF16))

    one_dir(0, qf_ref, ff_ref, vf_ref, of_ref)
    one_dir(1, qb_ref, fb_ref, vb_ref, ob_ref)

    @pl.when(i == nchunk - 1)
    def _():
        st_ref[...] = st_scr[...]


def _hgrn(z, seq, batch, lb, s0):
    m = z.shape[0]
    c = min(HGRN_CHUNK, seq)
    nchunk = seq // c
    levels = int(math.log2(c))
    wnp, cmnp = _hgrn_consts(c)
    wc = jnp.asarray(wnp, BF16)
    cm = jnp.asarray(cmnp, F32)
    w = HGRN_WIDTH

    def fidx(b, i):
        return b * nchunk + i

    def bidx(b, i):
        return b * nchunk + nchunk - 1 - i

    blk = lambda idx, col: pl.BlockSpec((c, w), lambda b, i: (idx(b, i), col))
    full = lambda shape: pl.BlockSpec(shape, lambda b, i: (0,) * len(shape))
    st_spec = pl.BlockSpec((None, 2, HGRN_HEADS, LANES, LANES), lambda b, i: (b, 0, 0, 0, 0))
    kern = functools.partial(_hgrn_kernel, c=c, levels=levels, nchunk=nchunk)
    return pl.pallas_call(
        kern,
        grid=(batch, nchunk),
        in_specs=[blk(fidx, 0), blk(fidx, 1), blk(fidx, 3), blk(bidx, 0), blk(bidx, 2), blk(bidx, 3),
                  full(lb.shape), full(wc.shape), full(cm.shape), st_spec],
        out_specs=[pl.BlockSpec((c, w), lambda b, i: (fidx(b, i), 0)),
                   pl.BlockSpec((c, w), lambda b, i: (bidx(b, i), 0)), st_spec],
        out_shape=[jax.ShapeDtypeStruct((m, w), F32), jax.ShapeDtypeStruct((m, w), F32),
                   jax.ShapeDtypeStruct((batch, 2, HGRN_HEADS, LANES, LANES), F32)],
        scratch_shapes=[pltpu.VMEM((2, HGRN_HEADS, LANES, LANES), F32)],
        compiler_params=_cparams(("parallel", "arbitrary")),
        name="hgrn2",
    )(z, z, z, z, z, z, lb, wc, cm, s0)


def _outproj_kernel(x_ref, p0_ref, p1_ref, g_ref, att_ref, nw_ref, w_ref, gt_ref, o_ref, *, mode):
    half = w_ref.shape[0] // 2
    s = p0_ref[...] + p1_ref[...]
    g = g_ref[...]
    if mode == "ab":
        a = s * jax.nn.gelu(g, approximate=True)
    else:
        parts = []
        for h in range(HGRN_HEADS):
            sl = slice(h * LANES, (h + 1) * LANES)
            parts.append(_rms(s[:, sl]) * nw_ref[...])
        a = jnp.concatenate(parts, axis=-1) * (g * jax.nn.sigmoid(g))
    acc = _dot(a.astype(BF16), w_ref[0:half, :]) + _dot(att_ref[...], w_ref[half:2 * half, :])
    o_ref[...] = x_ref[...] + gt_ref[...] * acc


def _outproj(x2d, seq, p0, p1, gsrc, gcol, att, head_norm_w, w_out, mod, mode):
    m, d = x2d.shape
    tm = min(256, seq)
    tpb = seq // tm if mod.shape[0] > 1 else m
    hw = w_out.shape[0] // 2
    kern = functools.partial(_outproj_kernel, mode=mode)
    return pl.pallas_call(
        kern,
        grid=(m // tm,),
        in_specs=[pl.BlockSpec((tm, d), lambda i: (i, 0)),
                  pl.BlockSpec((tm, hw), lambda i: (i, 0)),
                  pl.BlockSpec((tm, hw), lambda i: (i, 0)),
                  pl.BlockSpec((tm, hw), lambda i: (i, gcol)),
                  pl.BlockSpec((tm, hw), lambda i: (i, 0)),
                  pl.BlockSpec((1, LANES), lambda i: (0, 0)),
                  pl.BlockSpec(w_out.shape, lambda i: (0, 0)),
                  pl.BlockSpec((None, None, 1, d), lambda i: (i // tpb, 2, 0, 0))],
        out_specs=pl.BlockSpec((tm, d), lambda i: (i, 0)),
        out_shape=jax.ShapeDtypeStruct((m, d), F32),
        compiler_params=_cparams(("parallel",)),
        name="outproj_" + mode,
    )(x2d, p0, p1, gsrc, att, head_norm_w, w_out, mod)


def _ffn_kernel(x_ref, nw_ref, sh_ref, sc_ref, gt_ref, wg_ref, wu_ref, wd_ref, fw_ref, o_ref, hn_ref, *, final):
    j = pl.program_id(1)

    @pl.when(j == 0)
    def _():
        h = _rms(x_ref[...]) * nw_ref[...]
        hn_ref[...] = (h * (1.0 + sc_ref[...]) + sh_ref[...]).astype(BF16)
        o_ref[...] = jnp.zeros(o_ref.shape, F32)

    hn = hn_ref[...]
    g = _dot(hn, wg_ref[...])
    u = _dot(hn, wu_ref[...])
    a = (g * jax.nn.sigmoid(g) * u).astype(BF16)
    o_ref[...] += _dot(a, wd_ref[...])

    @pl.when(j == pl.num_programs(1) - 1)
    def _():
        y = x_ref[...] + gt_ref[...] * o_ref[...]
        if final:
            y = _rms(y) * fw_ref[...]
        o_ref[...] = y


def _ffn(x2d, seq, norm_w, mod, w_gate, w_up, w_down, final_w, final):
    m, d = x2d.shape
    f = w_gate.shape[1]
    tm = min(512, seq)
    tf = 512
    tpb = seq // tm if mod.shape[0] > 1 else m
    mspec = lambda k: pl.BlockSpec((None, None, 1, d), lambda i, j: (i // tpb, k, 0, 0))
    kern = functools.partial(_ffn_kernel, final=final)
    return pl.pallas_call(
        kern,
        grid=(m // tm, f // tf),
        in_specs=[pl.BlockSpec((tm, d), lambda i, j: (i, 0)),
                  pl.BlockSpec((1, d), lambda i, j: (0, 0)),
                  mspec(3), mspec(4), mspec(5),
                  pl.BlockSpec((d, tf), lambda i, j: (0, j)),
                  pl.BlockSpec((d, tf), lambda i, j: (0, j)),
                  pl.BlockSpec((tf, d), lambda i, j: (j, 0)),
                  pl.BlockSpec((1, d), lambda i, j: (0, 0))],
        out_specs=pl.BlockSpec((tm, d), lambda i, j: (i, 0)),
        out_shape=jax.ShapeDtypeStruct((m, d), F32),
        scratch_shapes=[pltpu.VMEM((tm, d), BF16)],
        compiler_params=_cparams(("parallel", "arbitrary")),
        name="ffn",
    )(x2d, norm_w.reshape(1, d), mod, mod, mod, w_gate, w_up, w_down, final_w.reshape(1, d))


def _rope_tables(rows, head_dim):
    n_freq = head_dim // 4
    half = head_dim // 2
    row = jnp.repeat(jnp.arange(rows, dtype=F32), GRID_W)
    col = jnp.tile(jnp.arange(GRID_W, dtype=F32), rows)
    inv = ROPE_THETA ** (-jnp.arange(n_freq, dtype=F32) / n_freq)
    ang = jnp.concatenate([row[:, None] * inv, col[:, None] * inv], axis=-1)
    cos, sin = jnp.cos(ang), jnp.sin(ang)
    reps = LANES // head_dim
    zero = jnp.zeros_like(sin)
    cos_t = jnp.tile(jnp.concatenate([cos, cos], -1), (1, reps))
    if half * 2 == LANES:
        return cos_t, jnp.concatenate([-sin, sin], -1), None
    sin_a = jnp.tile(jnp.concatenate([-sin, zero], -1), (1, reps))
    sin_b = jnp.tile(jnp.concatenate([zero, sin], -1), (1, reps))
    return cos_t, sin_a, sin_b


def _identity_rope(t, head_dim):
    one = jnp.ones((t, LANES), F32)
    zero = jnp.zeros((t, LANES), F32)
    return (one, zero, None) if head_dim == LANES else (one, zero, zero)


def kernel(x, c, ctx, c_ctx, mod_w, mod_b, norm_mix_w, norm_ffn_w, ffn_w_gate, ffn_w_up, ffn_w_down, ab_w_in, ab_w_out, lru_conv_w, lru_conv_b, lru_wa, lru_ba, lru_wx, lru_bx, lru_lambda, diff_lq1, diff_lk1, diff_lq2, diff_lk2, diff_subln_w, cd_w_in, cd_w_out, hgrn_lb_logits, hgrn_norm_w, gqa_q_norm_w, gqa_k_norm_w, final_norm_w):
    batch, seq, d = x.shape
    clen = ctx.shape[1]
    depth = mod_w.shape[0]
    rows = seq // GRID_W

    cc = jnp.zeros((SUBLANES, d), F32).at[0:batch].set(c).at[batch].set(c_ctx)
    mods = _modulation(cc, mod_w, mod_b)
    lb_cum = jnp.cumsum(jax.nn.softmax(hgrn_lb_logits.astype(F32), axis=1), axis=1)

    xl = x.reshape(batch * seq, d)
    xc = ctx.reshape(batch * clen, d)

    for l in range(depth):
        last = l == depth - 1
        m_lat = mods[l, 0:batch].reshape(batch, N_MOD, 1, d)
        m_ctx = mods[l, batch:batch + 1].reshape(1, N_MOD, 1, d)
        streams = ((xl, seq, m_lat), (xc, clen, m_ctx))
        if l % 2 == 0:
            e = l // 2
            lambda_init = 0.8 - 0.6 * math.exp(-0.3 * l)
            w_in = ab_w_in[e]
            qscale = DIFF_HEAD_DIM ** -0.5 * LOG2E
            w_gx = w_in[:, 0:2048].astype(BF16)
            w_qk = jnp.concatenate([w_in[:, 2048:3072] * qscale, w_in[:, 3072:4096]], axis=1).astype(BF16)
            w_v = w_in[:, 4096:5120].astype(BF16)
            ropes = (_rope_tables(rows, DIFF_HEAD_DIM), _identity_rope(clen, DIFF_HEAD_DIM))
            proj = []
            for (xs, t, md), rp in zip(streams, ropes):
                gx = _inproj(xs, t, norm_mix_w[l], md, w_gx, F32, 1024)
                qk = _inproj(xs, t, norm_mix_w[l], md, w_qk, BF16, 1024, rope=rp, rope_half=DIFF_HEAD_DIM // 2)
                v = _inproj(xs, t, norm_mix_w[l], md, w_v, BF16, 1024)
                proj.append((gx, qk.reshape(batch, t, 2048), v.reshape(batch, t, 1024)))
            (gx_l, qk_l, v_l), (gx_c, qk_c, v_c) = proj
            wg = jnp.concatenate([lru_wa[e], lru_wx[e]], axis=-1).astype(BF16)
            bg = jnp.concatenate([lru_ba[e].reshape(2, LRU_BLOCKS, 1, LRU_BLOCK),
                                  lru_bx[e].reshape(2, LRU_BLOCKS, 1, LRU_BLOCK)], axis=-1)
            cv = -LRU_C * jax.nn.softplus(-lru_lambda[e].astype(F32))
            h0 = jnp.zeros((batch, 2, LRU_WIDTH), F32)
            hf_c, hb_c, h_ctx = _rglru(gx_c, clen, batch, lru_conv_w[e], lru_conv_b[e], wg, bg, cv, h0)
            hf_l, hb_l, _ = _rglru(gx_l, seq, batch, lru_conv_w[e], lru_conv_b[e], wg, bg, cv, h_ctx)
            lam = (jnp.exp(jnp.sum(diff_lq1[e].astype(F32) * diff_lk1[e].astype(F32)))
                   - jnp.exp(jnp.sum(diff_lq2[e].astype(F32) * diff_lk2[e].astype(F32))) + lambda_init)
            lam_vec = jnp.full((1, LANES), lam, F32)
            sw = diff_subln_w[e].reshape(1, LANES)
            d_l = _diff_attn(qk_l, v_l, qk_c, v_c, lam_vec, sw, 1.0 - lambda_init, batch, True)
            w_out = ab_w_out[e].astype(BF16)
            dummy_nw = jnp.ones((1, LANES), F32)
            xl = _outproj(xl, seq, hf_l, hb_l, gx_l, 0, d_l.reshape(batch * seq, 1024), dummy_nw, w_out, m_lat, "ab")
            if not last:
                d_c = _diff_attn(qk_c, v_c, qk_c, v_c, lam_vec, sw, 1.0 - lambda_init, batch, False)
                xc = _outproj(xc, clen, hf_c, hb_c, gx_c, 0, d_c.reshape(batch * clen, 1024), dummy_nw, w_out,
                              m_ctx, "ab")
        else:
            o = l // 2
            lb = lb_cum[:, l] - lb_cum[:, 0]
            w_in = cd_w_in[o]
            w_z = w_in[:, 0:5120].astype(BF16)
            w_qk = w_in[:, 5120:6400].astype(BF16)
            w_v = w_in[:, 6400:6656].astype(BF16)
            qscale = GQA_HEAD_DIM ** -0.5 * LOG2E
            chunk_w = jnp.concatenate([jnp.tile(gqa_q_norm_w[o] * qscale, GQA_HEADS),
                                       jnp.tile(gqa_k_norm_w[o], GQA_KV_HEADS)]).reshape(1, 1280)
            ropes = (_rope_tables(rows, GQA_HEAD_DIM), _identity_rope(clen, GQA_HEAD_DIM))
            proj = []
            for (xs, t, md), rp in zip(streams, ropes):
                z = _inproj(xs, t, norm_mix_w[l], md, w_z, F32, 1024)
                qk = _inproj(xs, t, norm_mix_w[l], md, w_qk, BF16, 1280, chunk_w=chunk_w, norm_chunks=10,
                             rope=rp[0:2], rope_half=GQA_HEAD_DIM // 2)
                v = _inproj(xs, t, norm_mix_w[l], md, w_v, BF16, 256)
                proj.append((z, qk.reshape(batch, t, 1280), v.reshape(batch, t, 256)))
            (z_l, qk_l, v_l), (z_c, qk_c, v_c) = proj
            s0 = jnp.zeros((batch, 2, HGRN_HEADS, LANES, LANES), F32)
            of_c, ob_c, s_ctx = _hgrn(z_c, clen, batch, lb, s0)
            of_l, ob_l, _ = _hgrn(z_l, seq, batch, lb, s_ctx)
            att_l = _gqa_attn(qk_l, v_l, qk_c, v_c, batch, True)
            w_out = cd_w_out[o].astype(BF16)
            hnw = hgrn_norm_w[o].reshape(1, LANES)
            xl = _outproj(xl, seq, of_l, ob_l, z_l, 4, att_l.reshape(batch * seq, 1024), hnw, w_out, m_lat, "cd")
            if not last:
                att_c = _gqa_attn(qk_c, v_c, qk_c, v_c, batch, False)
                xc = _outproj(xc, clen, of_c, ob_c, z_c, 4, att_c.reshape(batch * clen, 1024), hnw, w_out,
                              m_ctx, "cd")
        wgt, wup, wdn = ffn_w_gate[l].astype(BF16), ffn_w_up[l].astype(BF16), ffn_w_down[l].astype(BF16)
        xl = _ffn(xl, seq, norm_ffn_w[l], m_lat, wgt, wup, wdn, final_norm_w, last)
        if not last:
            xc = _ffn(xc, clen, norm_ffn_w[l], m_ctx, wgt, wup, wdn, final_norm_w, False)

    return xl.reshape(batch, seq, d)
```

```python
import functools
import math

import numpy as np
import jax
import jax.numpy as jnp
from jax import lax
from jax.experimental import pallas as pl
from jax.experimental.pallas import tpu as pltpu

F32 = jnp.float32
BF16 = jnp.bfloat16

GRID_W = 64
NORM_EPS = 1e-6
ROPE_THETA = 10000.0
N_MOD = 6
LRU_WIDTH = 1024
LRU_BLOCKS = 8
LRU_BLOCK = 128
LRU_C = 8.0
DIFF_HEADS = 8
DIFF_HEAD_DIM = 64
HGRN_HEADS = 8
HGRN_WIDTH = 1024
GQA_HEADS = 8
GQA_KV_HEADS = 2
GQA_REP = 4
GQA_HEAD_DIM = 128
LOG2E = 1.4426950408889634

LANES = 128
SUBLANES = 8
VMEM_LIMIT = 56 * 1024 * 1024

HGRN_CHUNK = 128


def _cparams(sem):
    return pltpu.CompilerParams(dimension_semantics=sem, vmem_limit_bytes=VMEM_LIMIT)


def _dot(a, b):
    return jnp.dot(a, b, preferred_element_type=F32)


def _dot_nt(a, b):
    return lax.dot_general(a, b, (((1,), (1,)), ((), ())), preferred_element_type=F32)


def _dot_tn(a, b):
    return lax.dot_general(a, b, (((0,), (0,)), ((), ())), preferred_element_type=F32)


def _neg_expm1(y):
    series = -y * (1.0 + 0.5 * y * (1.0 + (1.0 / 3.0) * y * (1.0 + 0.25 * y)))
    return jnp.where(y > -0.03, series, 1.0 - jnp.exp(y))


def _rms(x):
    return x * lax.rsqrt(jnp.mean(x * x, axis=-1, keepdims=True) + NORM_EPS)


def _mod_kernel(c_ref, w_ref, b_ref, o_ref):
    c = c_ref[...]
    a = c * jax.nn.sigmoid(c)
    o_ref[...] = jnp.dot(a, w_ref[...], preferred_element_type=F32,
                         precision=lax.Precision.HIGHEST) + b_ref[...]


def _modulation(cc, mod_w, mod_b):
    depth, d, n = mod_w.shape
    tn = 1024
    return pl.pallas_call(
        _mod_kernel,
        grid=(depth, n // tn),
        in_specs=[pl.BlockSpec((SUBLANES, d), lambda l, j: (0, 0)),
                  pl.BlockSpec((None, d, tn), lambda l, j: (l, 0, j)),
                  pl.BlockSpec((None, 1, tn), lambda l, j: (l, 0, j))],
        out_specs=pl.BlockSpec((None, SUBLANES, tn), lambda l, j: (l, 0, j)),
        out_shape=jax.ShapeDtypeStruct((depth, SUBLANES, n), F32),
        compiler_params=_cparams(("parallel", "parallel")),
        name="modulation",
    )(cc, mod_w, mod_b.reshape(depth, 1, n))


def _normmod_kernel(x_ref, nw_ref, sh_ref, sc_ref, o_ref):
    h = _rms(x_ref[...]) * nw_ref[...]
    o_ref[...] = (h * (1.0 + sc_ref[...]) + sh_ref[...]).astype(o_ref.dtype)


def _normmod(x2d, seq, norm_w, mod):
    m, d = x2d.shape
    tm = min(512, seq)
    tpb = seq // tm if mod.shape[0] > 1 else m
    return pl.pallas_call(
        _normmod_kernel,
        grid=(m // tm,),
        in_specs=[pl.BlockSpec((tm, d), lambda i: (i, 0)),
                  pl.BlockSpec((1, d), lambda i: (0, 0)),
                  pl.BlockSpec((None, None, 1, d), lambda i: (i // tpb, 0, 0, 0)),
                  pl.BlockSpec((None, None, 1, d), lambda i: (i // tpb, 1, 0, 0))],
        out_specs=pl.BlockSpec((tm, d), lambda i: (i, 0)),
        out_shape=jax.ShapeDtypeStruct((m, d), BF16),
        compiler_params=_cparams(("parallel",)),
        name="normmod",
    )(x2d, norm_w.reshape(1, d), mod, mod)


def _inproj_kernel(*refs, n_chunks, norm_chunks, rope_half):
    it = iter(refs)
    x_ref, w_ref = next(it), next(it)
    cw_ref = next(it) if norm_chunks else None
    if rope_half:
        cos_ref, sa_ref = next(it), next(it)
        sb_ref = next(it) if rope_half * 2 != LANES else None
    o_ref = next(it)

    acc = _dot(x_ref[...], w_ref[...])
    for c in range(n_chunks):
        sl = slice(c * LANES, (c + 1) * LANES)
        y = acc[:, sl]
        if c < norm_chunks:
            y = _rms(y) * cw_ref[:, sl]
        if rope_half:
            if rope_half * 2 == LANES:
                y = y * cos_ref[...] + pltpu.roll(y, rope_half, 1) * sa_ref[...]
            else:
                y = (y * cos_ref[...] + pltpu.roll(y, LANES - rope_half, 1) * sa_ref[...]
                     + pltpu.roll(y, rope_half, 1) * sb_ref[...])
        o_ref[:, sl] = y.astype(o_ref.dtype)


def _inproj(hn, seq, w, out_dtype, tn, chunk_w=None, norm_chunks=0, rope=None, rope_half=0):
    m, d = hn.shape
    n = w.shape[1]
    tm = min(1024, seq)
    in_specs = [pl.BlockSpec((tm, d), lambda i, j: (i, 0)),
                pl.BlockSpec((d, tn), lambda i, j: (0, j))]
    args = [hn, w]
    if norm_chunks:
        in_specs.append(pl.BlockSpec((1, tn), lambda i, j: (0, j)))
        args.append(chunk_w)
    if rope_half:
        spt = seq // tm
        for t in rope:
            in_specs.append(pl.BlockSpec((tm, LANES), lambda i, j: (i % spt, 0)))
            args.append(t)
    kern = functools.partial(_inproj_kernel, n_chunks=tn // LANES, norm_chunks=norm_chunks, rope_half=rope_half)
    return pl.pallas_call(
        kern,
        grid=(m // tm, n // tn),
        in_specs=in_specs,
        out_specs=pl.BlockSpec((tm, tn), lambda i, j: (i, j)),
        out_shape=jax.ShapeDtypeStruct((m, n), out_dtype),
        compiler_params=_cparams(("parallel", "parallel")),
        name="inproj",
    )(*args)


ACC_ROWS = LANES + 16


def _to_bf16_t(x):
    return x.astype(F32).T.astype(BF16)


def _transpose_values(v_ref, vt_ref, vT_ref, vtT_ref, *, tk):
    for c in range(v_ref.shape[0] // tk):
        vT_ref[c, 0:LANES, :] = _to_bf16_t(v_ref[c * tk:(c + 1) * tk, :])
        vT_ref[c, LANES:ACC_ROWS, :] = jnp.ones((ACC_ROWS - LANES, tk), BF16)
    vtT_ref[0:LANES, :] = _to_bf16_t(vt_ref[...])
    vtT_ref[LANES:ACC_ROWS, :] = jnp.ones((ACC_ROWS - LANES, vt_ref.shape[0]), BF16)


def _flash(qT_ref, k_ref, vT_ref, kt_ref, vtT_ref, s0_ref, s1_ref, m_ref, acc_ref, *, tk, n_main, has_tail):
    m_ref[...] = jnp.full(m_ref.shape, -jnp.inf, F32)
    acc_ref[...] = jnp.zeros(acc_ref.shape, F32)

    def update(s_ref, vT):
        m_prev = m_ref[...]
        m_new = jnp.maximum(m_prev, jnp.max(s_ref[...], axis=0, keepdims=True))
        alpha = jnp.exp2(m_prev - m_new)
        p = jnp.exp2(s_ref[...] - m_new).astype(BF16)
        acc_ref[...] = alpha * acc_ref[...] + _dot(vT, p)
        m_ref[...] = m_new

    def scores(c):
        off = pl.multiple_of(c * tk, tk)
        return _dot(k_ref[pl.ds(off, tk), :], qT_ref[...])

    if has_tail:
        st_ref = s1_ref.at[0:kt_ref.shape[0], :]
        st_ref[...] = _dot(kt_ref[...], qT_ref[...])
    s0_ref[...] = scores(0)
    if has_tail:
        update(st_ref, vtT_ref[...])

    if n_main % 2:
        def body(c, carry):
            s0_ref[...] = scores(c)
            update(s0_ref, vT_ref[c])
            return carry

        lax.fori_loop(0, n_main, body, 0)
    else:
        def body(c2, carry):
            c0 = 2 * c2
            s1_ref[...] = scores(c0 + 1)
            update(s0_ref, vT_ref[c0])
            s0_ref[...] = scores(jnp.minimum(c0 + 2, n_main - 1))
            update(s1_ref, vT_ref[c0 + 1])
            return carry

        lax.fori_loop(0, n_main // 2, body, 0, unroll=4 if n_main % 8 == 0 else 1)
    return acc_ref[0:LANES, :] / acc_ref[LANES:LANES + 1, :]


def _flash_scratch(rows, tk, t, tt):
    return [pltpu.VMEM((LANES, rows), BF16), pltpu.VMEM((t // tk, ACC_ROWS, tk), BF16),
            pltpu.VMEM((ACC_ROWS, tt), BF16), pltpu.VMEM((tk, rows), F32), pltpu.VMEM((tk, rows), F32),
            pltpu.VMEM((1, rows), F32), pltpu.VMEM((ACC_ROWS, rows), F32)]


def _diff_attn_kernel(q_ref, k_ref, v_ref, kt_ref, vt_ref, lam_ref, sw_ref, o_ref,
                      qT_ref, vT_ref, vtT_ref, s0_ref, s1_ref, m_ref, acc_ref,
                      *, tq, tk, n_main, has_tail, post_scale):
    @pl.when(pl.program_id(2) == 0)
    def _():
        _transpose_values(v_ref, vt_ref, vT_ref, vtT_ref, tk=tk)

    q = q_ref[...].astype(F32)
    lane = lax.broadcasted_iota(jnp.int32, q.shape, 1)
    qT_ref[:, 0:tq] = jnp.where(lane < DIFF_HEAD_DIM, q, 0.0).T.astype(BF16)
    qT_ref[:, tq:2 * tq] = jnp.where(lane >= DIFF_HEAD_DIM, q, 0.0).T.astype(BF16)
    o = _flash(qT_ref, k_ref, vT_ref, kt_ref, vtT_ref, s0_ref, s1_ref, m_ref, acc_ref,
               tk=tk, n_main=n_main, has_tail=has_tail).T
    y = o[0:tq, :] - lam_ref[...] * o[tq:2 * tq, :]
    y = _rms(y) * sw_ref[...] * post_scale
    o_ref[...] = y.astype(o_ref.dtype)


def _diff_attn(qk, v, qk_tail, v_tail, lam_vec, subln_w, post_scale, batch, has_tail):
    t = qk.shape[1]
    tq = min(512, t)
    tk = min(1024, t)
    kern = functools.partial(_diff_attn_kernel, tq=tq, tk=tk, n_main=t // tk, has_tail=has_tail,
                             post_scale=post_scale)
    tt = qk_tail.shape[1]
    return pl.pallas_call(
        kern,
        grid=(batch, DIFF_HEADS, t // tq),
        in_specs=[pl.BlockSpec((None, tq, LANES), lambda b, h, i: (b, i, h)),
                  pl.BlockSpec((None, t, LANES), lambda b, h, i: (b, 0, DIFF_HEADS + h)),
                  pl.BlockSpec((None, t, LANES), lambda b, h, i: (b, 0, h)),
                  pl.BlockSpec((None, tt, LANES), lambda b, h, i: (b, 0, DIFF_HEADS + h)),
                  pl.BlockSpec((None, tt, LANES), lambda b, h, i: (b, 0, h)),
                  pl.BlockSpec((1, LANES), lambda b, h, i: (0, 0)),
                  pl.BlockSpec((1, LANES), lambda b, h, i: (0, 0))],
        out_specs=pl.BlockSpec((None, tq, LANES), lambda b, h, i: (b, i, h)),
        out_shape=jax.ShapeDtypeStruct((batch, t, DIFF_HEADS * LANES), BF16),
        scratch_shapes=_flash_scratch(2 * tq, tk, t, tt),
        compiler_params=_cparams(("parallel", "parallel", "arbitrary")),
        name="diff_attn",
    )(qk, qk, v, qk_tail, v_tail, lam_vec, subln_w)


def _gqa_kernel(q_ref, k_ref, v_ref, kt_ref, vt_ref, o_ref, qT_ref, vT_ref, vtT_ref, s0_ref, s1_ref, m_ref, acc_ref,
                *, tq, tk, n_main, has_tail):
    @pl.when(pl.program_id(2) == 0)
    def _():
        _transpose_values(v_ref, vt_ref, vT_ref, vtT_ref, tk=tk)

    for r in range(GQA_REP):
        qT_ref[:, r * tq:(r + 1) * tq] = _to_bf16_t(q_ref[:, r * LANES:(r + 1) * LANES])
    o = _flash(qT_ref, k_ref, vT_ref, kt_ref, vtT_ref, s0_ref, s1_ref, m_ref, acc_ref,
               tk=tk, n_main=n_main, has_tail=has_tail).T
    for r in range(GQA_REP):
        o_ref[:, r * LANES:(r + 1) * LANES] = o[r * tq:(r + 1) * tq, :].astype(o_ref.dtype)


def _gqa_attn(qk, v, qk_tail, v_tail, batch, has_tail):
    t = qk.shape[1]
    tq = min(256, t)
    tk = min(1024, t)
    gw = GQA_REP * LANES
    kern = functools.partial(_gqa_kernel, tq=tq, tk=tk, n_main=t // tk, has_tail=has_tail)
    tt = qk_tail.shape[1]
    return pl.pallas_call(
        kern,
        grid=(batch, GQA_KV_HEADS, t // tq),
        in_specs=[pl.BlockSpec((None, tq, gw), lambda b, g, i: (b, i, g)),
                  pl.BlockSpec((None, t, LANES), lambda b, g, i: (b, 0, GQA_HEADS + g)),
                  pl.BlockSpec((None, t, LANES), lambda b, g, i: (b, 0, g)),
                  pl.BlockSpec((None, tt, LANES), lambda b, g, i: (b, 0, GQA_HEADS + g)),
                  pl.BlockSpec((None, tt, LANES), lambda b, g, i: (b, 0, g))],
        out_specs=pl.BlockSpec((None, tq, gw), lambda b, g, i: (b, i, g)),
        out_shape=jax.ShapeDtypeStruct((batch, t, GQA_HEADS * LANES), BF16),
        scratch_shapes=_flash_scratch(GQA_REP * tq, tk, t, tt),
        compiler_params=_cparams(("parallel", "parallel", "arbitrary")),
        name="gqa_attn",
    )(qk, qk, v, qk_tail, v_tail)


def _rglru_kernel(xf_ref, xfp_ref, xfn_ref, xb_ref, xbp_ref, xbn_ref, cw_ref, cb_ref, wg_ref, bg_ref, cv_ref,
                  h0_ref, hf_ref, hb_ref, ht_ref, a_scr, b_scr, st_scr, *, tb, nblk):
    i = pl.program_id(1)

    @pl.when(i == 0)
    def _():
        st_scr[...] = h0_ref[...]

    row = lax.broadcasted_iota(jnp.int32, (tb, LRU_WIDTH), 0)

    def gates(d, x_ref, xp_ref, xn_ref, blk):
        x = x_ref[...]
        prev = xp_ref[SUBLANES - 1:SUBLANES, :] * (blk > 0).astype(F32)
        has_next = (blk < nblk - 1).astype(F32)
        nxt0 = xn_ref[0:1, :] * has_next
        nxt1 = xn_ref[1:2, :] * has_next
        xm1 = jnp.where(row == 0, prev, pltpu.roll(x, 1, 0))
        xp1 = jnp.where(row == tb - 1, nxt0, pltpu.roll(x, tb - 1, 0))
        xp2 = jnp.where(row == tb - 2, nxt0, jnp.where(row == tb - 1, nxt1, pltpu.roll(x, tb - 2, 0)))
        y = xm1 * cw_ref[0:1, :] + x * cw_ref[1:2, :] + xp1 * cw_ref[2:3, :] + xp2 * cw_ref[3:4, :] + cb_ref[...]
        yb = y.astype(BF16)
        for c in range(LRU_BLOCKS):
            sl = slice(c * LRU_BLOCK, (c + 1) * LRU_BLOCK)
            z = _dot(yb[:, sl], wg_ref[d, c]) + bg_ref[d, c]
            r = jax.nn.sigmoid(z[:, 0:LRU_BLOCK])
            g = jax.nn.sigmoid(z[:, LRU_BLOCK:2 * LRU_BLOCK])
            log_a = r * cv_ref[d:d + 1, sl]
            a_scr[d, :, sl] = jnp.exp(log_a)
            b_scr[d, :, sl] = jnp.sqrt(_neg_expm1(2.0 * log_a)) * (g * y[:, sl])

    gates(0, xf_ref, xfp_ref, xfn_ref, i)
    gates(1, xb_ref, xbp_ref, xbn_ref, nblk - 1 - i)

    row8 = lax.broadcasted_iota(jnp.int32, (SUBLANES, LRU_WIDTH), 0)
    nt = tb // SUBLANES

    def scan(d, out_ref):
        rev = d == 1

        def body(r, h):
            off = pl.multiple_of((nt - 1 - r if rev else r) * SUBLANES, SUBLANES)
            a8 = a_scr[d, pl.ds(off, SUBLANES), :]
            b8 = b_scr[d, pl.ds(off, SUBLANES), :]
            for s in (1, 2, 4):
                if rev:
                    ok = row8 < SUBLANES - s
                    sh = SUBLANES - s
                else:
                    ok = row8 >= s
                    sh = s
                a_sh = jnp.where(ok, pltpu.roll(a8, sh, 0), 1.0)
                b_sh = jnp.where(ok, pltpu.roll(b8, sh, 0), 0.0)
                b8 = a8 * b_sh + b8
                a8 = a8 * a_sh
            h8 = a8 * h + b8
            out_ref[pl.ds(off, SUBLANES), :] = h8
            return h8[0:1, :] if rev else h8[SUBLANES - 1:SUBLANES, :]

        st_scr[d:d + 1, :] = lax.fori_loop(0, nt, body, st_scr[d:d + 1, :])

    scan(0, hf_ref)
    scan(1, hb_ref)

    @pl.when(i == nblk - 1)
    def _():
        ht_ref[...] = st_scr[...]


def _rglru(gx, seq, batch, conv_w, conv_b, wg, bg, cv, h0):
    m = gx.shape[0]
    tb = min(256, seq)
    nblk = seq // tb
    hb8 = tb // SUBLANES
    last8 = m // SUBLANES - 1
    w = LRU_WIDTH

    def fidx(b, i):
        return b * nblk + i

    def bidx(b, i):
        return b * nblk + nblk - 1 - i

    def specs(idx):
        return [pl.BlockSpec((tb, w), lambda b, i: (idx(b, i), 1)),
                pl.BlockSpec((SUBLANES, w), lambda b, i: (jnp.maximum(idx(b, i) * hb8 - 1, 0), 1)),
                pl.BlockSpec((SUBLANES, w), lambda b, i: (jnp.minimum((idx(b, i) + 1) * hb8, last8), 1))]

    full = lambda shape: pl.BlockSpec(shape, lambda b, i: (0,) * len(shape))
    kern = functools.partial(_rglru_kernel, tb=tb, nblk=nblk)
    return pl.pallas_call(
        kern,
        grid=(batch, nblk),
        in_specs=specs(fidx) + specs(bidx) + [full(conv_w.shape), full((1, w)), full(wg.shape), full(bg.shape),
                                              full(cv.shape), pl.BlockSpec((None, 2, w), lambda b, i: (b, 0, 0))],
        out_specs=[pl.BlockSpec((tb, w), lambda b, i: (fidx(b, i), 0)),
                   pl.BlockSpec((tb, w), lambda b, i: (bidx(b, i), 0)),
                   pl.BlockSpec((None, 2, w), lambda b, i: (b, 0, 0))],
        out_shape=[jax.ShapeDtypeStruct((m, w), F32), jax.ShapeDtypeStruct((m, w), F32),
                   jax.ShapeDtypeStruct((batch, 2, w), F32)],
        scratch_shapes=[pltpu.VMEM((2, tb, w), F32), pltpu.VMEM((2, tb, w), F32), pltpu.VMEM((2, w), F32)],
        compiler_params=_cparams(("parallel", "arbitrary")),
        name="rglru",
    )(gx, gx, gx, gx, gx, gx, conv_w, conv_b.reshape(1, w), wg, bg, cv, h0)


def _hgrn_consts(c):
    t = np.arange(c)
    tinc = (t[None, :] <= t[:, None]).astype(np.float32)
    urev = (t[None, :] > t[:, None]).astype(np.float32)
    blocks = [tinc, urev]
    masks = []
    m = c // 2
    while m >= 1:
        mid = (t // (2 * m)) * (2 * m) + m
        right = t >= mid
        u = t[None, :]
        g = np.where(right[:, None], (u >= mid[:, None]) & (u <= t[:, None]), (u > t[:, None]) & (u < mid[:, None]))
        blocks.append(g.astype(np.float32))
        same = (t[:, None] // (2 * m)) == (t[None, :] // (2 * m))
        masks.append((same & right[:, None] & (~right)[None, :]).astype(np.float32))
        m //= 2
    masks.append(np.eye(c, dtype=np.float32))
    flip = lambda a: a[::-1, ::-1]
    ones = np.ones((16, c), np.float32)
    w = np.stack([np.concatenate(blocks + [ones], 0), np.concatenate([flip(b) for b in blocks] + [ones], 0)])
    cm = np.stack([np.stack(masks), np.stack([flip(a) for a in masks])])
    return w, cm


def _hgrn_kernel(qf_ref, ff_ref, vf_ref, qb_ref, fb_ref, vb_ref, lb_ref, wc_ref, cm_ref, s0_ref,
                 of_ref, ob_ref, st_ref, st_scr, *, c, levels, nchunk):
    i = pl.program_id(1)

    @pl.when(i == 0)
    def _():
        st_scr[...] = s0_ref[...]

    def prep(d, q_ref, f_ref, v_ref):
        q = q_ref[...]
        q = q * jax.nn.sigmoid(q)
        lb = lb_ref[d:d + 1, :]
        f = lb + (1.0 - lb) * jax.nn.sigmoid(f_ref[...])
        kk = 1.0 - f
        g = jnp.log(f)
        g1 = g.astype(BF16)
        g2 = (g - g1.astype(F32)).astype(BF16)
        w = wc_ref[d]
        e = jnp.exp(_dot(w, g1) + _dot(w, g2))
        return q, kk, v_ref[...].astype(BF16), e

    def head(d, h, q, kk, v, e, o_ref):
        sl = slice(h * LANES, (h + 1) * LANES)
        st = st_scr[d, h]
        qh, kh, vh = q[:, sl], kk[:, sl], v[:, sl]
        o = _dot_nt((qh * e[0:c, sl]).astype(BF16), st.astype(BF16))
        sc = cm_ref[d, levels] * _dot_nt(qh.astype(BF16), kh.astype(BF16))
        for l in range(levels):
            el = e[(2 + l) * c:(3 + l) * c, sl]
            sc = sc + cm_ref[d, l] * _dot_nt((qh * el).astype(BF16), (kh * el).astype(BF16))
        o_ref[:, sl] = o + _dot(sc.astype(BF16), vh)
        etot = e[(2 + levels) * c:(2 + levels) * c + 1, sl]
        st_scr[d, h] = st * etot + _dot_tn(vh, (kh * e[c:2 * c, sl]).astype(BF16))

    fwd = prep(0, qf_ref, ff_ref, vf_ref)
    bwd = prep(1, qb_ref, fb_ref, vb_ref)
    for h in range(HGRN_HEADS):
        head(0, h, *fwd, of_ref)
        head(1, h, *bwd, ob_ref)

    @pl.when(i == nchunk - 1)
    def _():
        st_ref[...] = st_scr[...]


def _hgrn(z, seq, batch, lb, s0):
    m = z.shape[0]
    c = min(HGRN_CHUNK, seq)
    nchunk = seq // c
    levels = int(math.log2(c))
    wnp, cmnp = _hgrn_consts(c)
    wc = jnp.asarray(wnp, BF16)
    cm = jnp.asarray(cmnp, F32)
    w = HGRN_WIDTH

    def fidx(b, i):
        return b * nchunk + i

    def bidx(b, i):
        return b * nchunk + nchunk - 1 - i

    blk = lambda idx, col: pl.BlockSpec((c, w), lambda b, i: (idx(b, i), col))
    full = lambda shape: pl.BlockSpec(shape, lambda b, i: (0,) * len(shape))
    st_spec = pl.BlockSpec((None, 2, HGRN_HEADS, LANES, LANES), lambda b, i: (b, 0, 0, 0, 0))
    kern = functools.partial(_hgrn_kernel, c=c, levels=levels, nchunk=nchunk)
    return pl.pallas_call(
        kern,
        grid=(batch, nchunk),
        in_specs=[blk(fidx, 0), blk(fidx, 1), blk(fidx, 3), blk(bidx, 0), blk(bidx, 2), blk(bidx, 3),
                  full(lb.shape), full(wc.shape), full(cm.shape), st_spec],
        out_specs=[pl.BlockSpec((c, w), lambda b, i: (fidx(b, i), 0)),
                   pl.BlockSpec((c, w), lambda b, i: (bidx(b, i), 0)), st_spec],
        out_shape=[jax.ShapeDtypeStruct((m, w), F32), jax.ShapeDtypeStruct((m, w), F32),
                   jax.ShapeDtypeStruct((batch, 2, HGRN_HEADS, LANES, LANES), F32)],
        scratch_shapes=[pltpu.VMEM((2, HGRN_HEADS, LANES, LANES), F32)],
        compiler_params=_cparams(("parallel", "arbitrary")),
        name="hgrn2",
    )(z, z, z, z, z, z, lb, wc, cm, s0)


def _outproj_kernel(x_ref, p0_ref, p1_ref, g_ref, att_ref, nw_ref, w_ref, gt_ref, o_ref, *, mode):
    half = w_ref.shape[0] // 2
    s = p0_ref[...] + p1_ref[...]
    g = g_ref[...]
    if mode == "ab":
        a = s * jax.nn.gelu(g, approximate=True)
    else:
        parts = []
        for h in range(HGRN_HEADS):
            sl = slice(h * LANES, (h + 1) * LANES)
            parts.append(_rms(s[:, sl]) * nw_ref[...])
        a = jnp.concatenate(parts, axis=-1) * (g * jax.nn.sigmoid(g))
    acc = _dot(a.astype(BF16), w_ref[0:half, :]) + _dot(att_ref[...], w_ref[half:2 * half, :])
    o_ref[...] = x_ref[...] + gt_ref[...] * acc


def _outproj(x2d, seq, p0, p1, gsrc, gcol, att, head_norm_w, w_out, mod, mode):
    m, d = x2d.shape
    tm = min(256, seq)
    tpb = seq // tm if mod.shape[0] > 1 else m
    hw = w_out.shape[0] // 2
    kern = functools.partial(_outproj_kernel, mode=mode)
    return pl.pallas_call(
        kern,
        grid=(m // tm,),
        in_specs=[pl.BlockSpec((tm, d), lambda i: (i, 0)),
                  pl.BlockSpec((tm, hw), lambda i: (i, 0)),
                  pl.BlockSpec((tm, hw), lambda i: (i, 0)),
                  pl.BlockSpec((tm, hw), lambda i: (i, gcol)),
                  pl.BlockSpec((tm, hw), lambda i: (i, 0)),
                  pl.BlockSpec((1, LANES), lambda i: (0, 0)),
                  pl.BlockSpec(w_out.shape, lambda i: (0, 0)),
                  pl.BlockSpec((None, None, 1, d), lambda i: (i // tpb, 2, 0, 0))],
        out_specs=pl.BlockSpec((tm, d), lambda i: (i, 0)),
        out_shape=jax.ShapeDtypeStruct((m, d), F32),
        compiler_params=_cparams(("parallel",)),
        name="outproj_" + mode,
    )(x2d, p0, p1, gsrc, att, head_norm_w, w_out, mod)


def _ffn_kernel(x_ref, nw_ref, sh_ref, sc_ref, gt_ref, wg_ref, wu_ref, wd_ref, fw_ref, o_ref, hn_ref, *, final):
    j = pl.program_id(1)

    @pl.when(j == 0)
    def _():
        h = _rms(x_ref[...]) * nw_ref[...]
        hn_ref[...] = (h * (1.0 + sc_ref[...]) + sh_ref[...]).astype(BF16)
        o_ref[...] = jnp.zeros(o_ref.shape, F32)

    hn = hn_ref[...]
    g = _dot(hn, wg_ref[...])
    u = _dot(hn, wu_ref[...])
    a = (g * jax.nn.sigmoid(g) * u).astype(BF16)
    o_ref[...] += _dot(a, wd_ref[...])

    @pl.when(j == pl.num_programs(1) - 1)
    def _():
        y = x_ref[...] + gt_ref[...] * o_ref[...]
        if final:
            y = _rms(y) * fw_ref[...]
        o_ref[...] = y


def _ffn(x2d, seq, norm_w, mod, w_gate, w_up, w_down, final_w, final):
    m, d = x2d.shape
    f = w_gate.shape[1]
    tm = min(512, seq)
    tf = 512
    tpb = seq // tm if mod.shape[0] > 1 else m
    mspec = lambda k: pl.BlockSpec((None, None, 1, d), lambda i, j: (i // tpb, k, 0, 0))
    kern = functools.partial(_ffn_kernel, final=final)
    return pl.pallas_call(
        kern,
        grid=(m // tm, f // tf),
        in_specs=[pl.BlockSpec((tm, d), lambda i, j: (i, 0)),
                  pl.BlockSpec((1, d), lambda i, j: (0, 0)),
                  mspec(3), mspec(4), mspec(5),
                  pl.BlockSpec((d, tf), lambda i, j: (0, j)),
                  pl.BlockSpec((d, tf), lambda i, j: (0, j)),
                  pl.BlockSpec((tf, d), lambda i, j: (j, 0)),
                  pl.BlockSpec((1, d), lambda i, j: (0, 0))],
        out_specs=pl.BlockSpec((tm, d), lambda i, j: (i, 0)),
        out_shape=jax.ShapeDtypeStruct((m, d), F32),
        scratch_shapes=[pltpu.VMEM((tm, d), BF16)],
        compiler_params=_cparams(("parallel", "arbitrary")),
        name="ffn",
    )(x2d, norm_w.reshape(1, d), mod, mod, mod, w_gate, w_up, w_down, final_w.reshape(1, d))


def _rope_tables(rows, head_dim):
    n_freq = head_dim // 4
    half = head_dim // 2
    row = jnp.repeat(jnp.arange(rows, dtype=F32), GRID_W)
    col = jnp.tile(jnp.arange(GRID_W, dtype=F32), rows)
    inv = ROPE_THETA ** (-jnp.arange(n_freq, dtype=F32) / n_freq)
    ang = jnp.concatenate([row[:, None] * inv, col[:, None] * inv], axis=-1)
    cos, sin = jnp.cos(ang), jnp.sin(ang)
    reps = LANES // head_dim
    zero = jnp.zeros_like(sin)
    cos_t = jnp.tile(jnp.concatenate([cos, cos], -1), (1, reps))
    if half * 2 == LANES:
        return cos_t, jnp.concatenate([-sin, sin], -1), None
    sin_a = jnp.tile(jnp.concatenate([-sin, zero], -1), (1, reps))
    sin_b = jnp.tile(jnp.concatenate([zero, sin], -1), (1, reps))
    return cos_t, sin_a, sin_b


def _identity_rope(t, head_dim):
    one = jnp.ones((t, LANES), F32)
    zero = jnp.zeros((t, LANES), F32)
    return (one, zero, None) if head_dim == LANES else (one, zero, zero)


def kernel(x, c, ctx, c_ctx, mod_w, mod_b, norm_mix_w, norm_ffn_w, ffn_w_gate, ffn_w_up, ffn_w_down, ab_w_in, ab_w_out, lru_conv_w, lru_conv_b, lru_wa, lru_ba, lru_wx, lru_bx, lru_lambda, diff_lq1, diff_lk1, diff_lq2, diff_lk2, diff_subln_w, cd_w_in, cd_w_out, hgrn_lb_logits, hgrn_norm_w, gqa_q_norm_w, gqa_k_norm_w, final_norm_w):
    batch, seq, d = x.shape
    clen = ctx.shape[1]
    depth = mod_w.shape[0]
    rows = seq // GRID_W

    cc = jnp.zeros((SUBLANES, d), F32).at[0:batch].set(c).at[batch].set(c_ctx)
    mods = _modulation(cc, mod_w, mod_b)
    lb_cum = jnp.cumsum(jax.nn.softmax(hgrn_lb_logits.astype(F32), axis=1), axis=1)

    xl = x.reshape(batch * seq, d)
    xc = ctx.reshape(batch * clen, d)

    for l in range(depth):
        last = l == depth - 1
        m_lat = mods[l, 0:batch].reshape(batch, N_MOD, 1, d)
        m_ctx = mods[l, batch:batch + 1].reshape(1, N_MOD, 1, d)
        streams = ((_normmod(xl, seq, norm_mix_w[l], m_lat), seq), (_normmod(xc, clen, norm_mix_w[l], m_ctx), clen))
        if l % 2 == 0:
            e = l // 2
            lambda_init = 0.8 - 0.6 * math.exp(-0.3 * l)
            w_in = ab_w_in[e]
            qscale = DIFF_HEAD_DIM ** -0.5 * LOG2E
            w_gx = w_in[:, 0:2048].astype(BF16)
            w_qk = jnp.concatenate([w_in[:, 2048:3072] * qscale, w_in[:, 3072:4096]], axis=1).astype(BF16)
            w_v = w_in[:, 4096:5120].astype(BF16)
            ropes = (_rope_tables(rows, DIFF_HEAD_DIM), _identity_rope(clen, DIFF_HEAD_DIM))
            proj = []
            for (hn, t), rp in zip(streams, ropes):
                gx = _inproj(hn, t, w_gx, F32, 1024)
                qk = _inproj(hn, t, w_qk, BF16, 1024, rope=rp, rope_half=DIFF_HEAD_DIM // 2)
                v = _inproj(hn, t, w_v, BF16, 1024)
                proj.append((gx, qk.reshape(batch, t, 2048), v.reshape(batch, t, 1024)))
            (gx_l, qk_l, v_l), (gx_c, qk_c, v_c) = proj
            wg = jnp.concatenate([lru_wa[e], lru_wx[e]], axis=-1).astype(BF16)
            bg = jnp.concatenate([lru_ba[e].reshape(2, LRU_BLOCKS, 1, LRU_BLOCK),
                                  lru_bx[e].reshape(2, LRU_BLOCKS, 1, LRU_BLOCK)], axis=-1)
            cv = -LRU_C * jax.nn.softplus(-lru_lambda[e].astype(F32))
            h0 = jnp.zeros((batch, 2, LRU_WIDTH), F32)
            hf_c, hb_c, h_ctx = _rglru(gx_c, clen, batch, lru_conv_w[e], lru_conv_b[e], wg, bg, cv, h0)
            hf_l, hb_l, _ = _rglru(gx_l, seq, batch, lru_conv_w[e], lru_conv_b[e], wg, bg, cv, h_ctx)
            lam = (jnp.exp(jnp.sum(diff_lq1[e].astype(F32) * diff_lk1[e].astype(F32)))
                   - jnp.exp(jnp.sum(diff_lq2[e].astype(F32) * diff_lk2[e].astype(F32))) + lambda_init)
            lam_vec = jnp.full((1, LANES), lam, F32)
            sw = diff_subln_w[e].reshape(1, LANES)
            d_l = _diff_attn(qk_l, v_l, qk_c, v_c, lam_vec, sw, 1.0 - lambda_init, batch, True)
            w_out = ab_w_out[e].astype(BF16)
            dummy_nw = jnp.ones((1, LANES), F32)
            xl = _outproj(xl, seq, hf_l, hb_l, gx_l, 0, d_l.reshape(batch * seq, 1024), dummy_nw, w_out, m_lat, "ab")
            if not last:
                d_c = _diff_attn(qk_c, v_c, qk_c, v_c, lam_vec, sw, 1.0 - lambda_init, batch, False)
                xc = _outproj(xc, clen, hf_c, hb_c, gx_c, 0, d_c.reshape(batch * clen, 1024), dummy_nw, w_out,
                              m_ctx, "ab")
        else:
            o = l // 2
            lb = lb_cum[:, l] - lb_cum[:, 0]
            w_in = cd_w_in[o]
            w_z = w_in[:, 0:5120].astype(BF16)
            w_qk = w_in[:, 5120:6400].astype(BF16)
            w_v = w_in[:, 6400:6656].astype(BF16)
            qscale = GQA_HEAD_DIM ** -0.5 * LOG2E
            chunk_w = jnp.concatenate([jnp.tile(gqa_q_norm_w[o] * qscale, GQA_HEADS),
                                       jnp.tile(gqa_k_norm_w[o], GQA_KV_HEADS)]).reshape(1, 1280)
            ropes = (_rope_tables(rows, GQA_HEAD_DIM), _identity_rope(clen, GQA_HEAD_DIM))
            proj = []
            for (hn, t), rp in zip(streams, ropes):
                z = _inproj(hn, t, w_z, F32, 1024)
                qk = _inproj(hn, t, w_qk, BF16, 1280, chunk_w=chunk_w, norm_chunks=10,
                             rope=rp[0:2], rope_half=GQA_HEAD_DIM // 2)
                v = _inproj(hn, t, w_v, BF16, 256)
                proj.append((z, qk.reshape(batch, t, 1280), v.reshape(batch, t, 256)))
            (z_l, qk_l, v_l), (z_c, qk_c, v_c) = proj
            s0 = jnp.zeros((batch, 2, HGRN_HEADS, LANES, LANES), F32)
            of_c, ob_c, s_ctx = _hgrn(z_c, clen, batch, lb, s0)
            of_l, ob_l, _ = _hgrn(z_l, seq, batch, lb, s_ctx)
            att_l = _gqa_attn(qk_l, v_l, qk_c, v_c, batch, True)
            w_out = cd_w_out[o].astype(BF16)
            hnw = hgrn_norm_w[o].reshape(1, LANES)
            xl = _outproj(xl, seq, of_l, ob_l, z_l, 4, att_l.reshape(batch * seq, 1024), hnw, w_out, m_lat, "cd")
            if not last:
                att_c = _gqa_attn(qk_c, v_c, qk_c, v_c, batch, False)
                xc = _outproj(xc, clen, of_c, ob_c, z_c, 4, att_c.reshape(batch * clen, 1024), hnw, w_out,
                              m_ctx, "cd")
        wgt, wup, wdn = ffn_w_gate[l].astype(BF16), ffn_w_up[l].astype(BF16), ffn_w_down[l].astype(BF16)
        xl = _ffn(xl, seq, norm_ffn_w[l], m_lat, wgt, wup, wdn, final_norm_w, last)
        if not last:
            xc = _ffn(xc, clen, norm_ffn_w[l], m_ctx, wgt, wup, wdn, final_norm_w, False)

    return xl.reshape(batch, seq, d)
```

```python
import functools
import math

import numpy as np
import jax
import jax.numpy as jnp
from jax import lax
from jax.experimental import pallas as pl
from jax.experimental.pallas import tpu as pltpu

F32 = jnp.float32
BF16 = jnp.bfloat16

GRID_W = 64
NORM_EPS = 1e-6
ROPE_THETA = 10000.0
N_MOD = 6
LRU_WIDTH = 1024
LRU_BLOCKS = 8
LRU_BLOCK = 128
LRU_C = 8.0
DIFF_HEADS = 8
DIFF_HEAD_DIM = 64
HGRN_HEADS = 8
HGRN_WIDTH = 1024
GQA_HEADS = 8
GQA_KV_HEADS = 2
GQA_REP = 4
GQA_HEAD_DIM = 128
LOG2E = 1.4426950408889634

LANES = 128
SUBLANES = 8
VMEM_LIMIT = 56 * 1024 * 1024

HGRN_CHUNK = 128


def _cparams(sem):
    return pltpu.CompilerParams(dimension_semantics=sem, vmem_limit_bytes=VMEM_LIMIT)


def _dot(a, b):
    return jnp.dot(a, b, preferred_element_type=F32)


def _dot_nt(a, b):
    return lax.dot_general(a, b, (((1,), (1,)), ((), ())), preferred_element_type=F32)


def _dot_tn(a, b):
    return lax.dot_general(a, b, (((0,), (0,)), ((), ())), preferred_element_type=F32)


def _neg_expm1(y):
    series = -y * (1.0 + 0.5 * y * (1.0 + (1.0 / 3.0) * y * (1.0 + 0.25 * y)))
    return jnp.where(y > -0.03, series, 1.0 - jnp.exp(y))


def _rms(x):
    return x * lax.rsqrt(jnp.mean(x * x, axis=-1, keepdims=True) + NORM_EPS)


def _mod_kernel(c_ref, w_ref, b_ref, o_ref):
    c = c_ref[...]
    a = c * jax.nn.sigmoid(c)
    o_ref[...] = jnp.dot(a, w_ref[...], preferred_element_type=F32,
                         precision=lax.Precision.HIGHEST) + b_ref[...]


def _modulation(cc, mod_w, mod_b):
    depth, d, n = mod_w.shape
    tn = 1024
    return pl.pallas_call(
        _mod_kernel,
        grid=(depth, n // tn),
        in_specs=[pl.BlockSpec((SUBLANES, d), lambda l, j: (0, 0)),
                  pl.BlockSpec((None, d, tn), lambda l, j: (l, 0, j)),
                  pl.BlockSpec((None, 1, tn), lambda l, j: (l, 0, j))],
        out_specs=pl.BlockSpec((None, SUBLANES, tn), lambda l, j: (l, 0, j)),
        out_shape=jax.ShapeDtypeStruct((depth, SUBLANES, n), F32),
        compiler_params=_cparams(("parallel", "parallel")),
        name="modulation",
    )(cc, mod_w, mod_b.reshape(depth, 1, n))


def _normmod_kernel(x_ref, nw_ref, sh_ref, sc_ref, o_ref):
    h = _rms(x_ref[...]) * nw_ref[...]
    o_ref[...] = (h * (1.0 + sc_ref[...]) + sh_ref[...]).astype(o_ref.dtype)


def _normmod(x2d, seq, norm_w, mod):
    m, d = x2d.shape
    tm = min(512, seq)
    tpb = seq // tm if mod.shape[0] > 1 else m
    return pl.pallas_call(
        _normmod_kernel,
        grid=(m // tm,),
        in_specs=[pl.BlockSpec((tm, d), lambda i: (i, 0)),
                  pl.BlockSpec((1, d), lambda i: (0, 0)),
                  pl.BlockSpec((None, None, 1, d), lambda i: (i // tpb, 0, 0, 0)),
                  pl.BlockSpec((None, None, 1, d), lambda i: (i // tpb, 1, 0, 0))],
        out_specs=pl.BlockSpec((tm, d), lambda i: (i, 0)),
        out_shape=jax.ShapeDtypeStruct((m, d), BF16),
        compiler_params=_cparams(("parallel",)),
        name="normmod",
    )(x2d, norm_w.reshape(1, d), mod, mod)


def _inproj_kernel(*refs, n_chunks, norm_chunks, rope_half):
    it = iter(refs)
    x_ref, w_ref = next(it), next(it)
    cw_ref = next(it) if norm_chunks else None
    if rope_half:
        cos_ref, sa_ref = next(it), next(it)
        sb_ref = next(it) if rope_half * 2 != LANES else None
    o_ref = next(it)

    acc = _dot(x_ref[...], w_ref[...])
    for c in range(n_chunks):
        sl = slice(c * LANES, (c + 1) * LANES)
        y = acc[:, sl]
        if c < norm_chunks:
            y = _rms(y) * cw_ref[:, sl]
        if rope_half:
            if rope_half * 2 == LANES:
                y = y * cos_ref[...] + pltpu.roll(y, rope_half, 1) * sa_ref[...]
            else:
                y = (y * cos_ref[...] + pltpu.roll(y, LANES - rope_half, 1) * sa_ref[...]
                     + pltpu.roll(y, rope_half, 1) * sb_ref[...])
        o_ref[:, sl] = y.astype(o_ref.dtype)


def _inproj(hn, seq, w, out_dtype, tn, chunk_w=None, norm_chunks=0, rope=None, rope_half=0):
    m, d = hn.shape
    n = w.shape[1]
    tm = min(1024, seq)
    in_specs = [pl.BlockSpec((tm, d), lambda i, j: (i, 0)),
                pl.BlockSpec((d, tn), lambda i, j: (0, j))]
    args = [hn, w]
    if norm_chunks:
        in_specs.append(pl.BlockSpec((1, tn), lambda i, j: (0, j)))
        args.append(chunk_w)
    if rope_half:
        spt = seq // tm
        for t in rope:
            in_specs.append(pl.BlockSpec((tm, LANES), lambda i, j: (i % spt, 0)))
            args.append(t)
    kern = functools.partial(_inproj_kernel, n_chunks=tn // LANES, norm_chunks=norm_chunks, rope_half=rope_half)
    return pl.pallas_call(
        kern,
        grid=(m // tm, n // tn),
        in_specs=in_specs,
        out_specs=pl.BlockSpec((tm, tn), lambda i, j: (i, j)),
        out_shape=jax.ShapeDtypeStruct((m, n), out_dtype),
        compiler_params=_cparams(("parallel", "parallel")),
        name="inproj",
    )(*args)


ACC_ROWS = LANES + 16


def _to_bf16_t(x):
    return x.astype(F32).T.astype(BF16)


def _transpose_values(v_ref, vt_ref, vT_ref, vtT_ref, *, tk):
    for c in range(v_ref.shape[0] // tk):
        vT_ref[c, 0:LANES, :] = _to_bf16_t(v_ref[c * tk:(c + 1) * tk, :])
        vT_ref[c, LANES:ACC_ROWS, :] = jnp.ones((ACC_ROWS - LANES, tk), BF16)
    vtT_ref[0:LANES, :] = _to_bf16_t(vt_ref[...])
    vtT_ref[LANES:ACC_ROWS, :] = jnp.ones((ACC_ROWS - LANES, vt_ref.shape[0]), BF16)


def _flash_tiles(prep_q, finalize, qT_ref, k_ref, vT_ref, kt_ref, vtT_ref, s0_ref, s1_ref, st_ref, m_ref, acc_ref,
                 *, nq, tk, n_main, has_tail):
    def qk(slot, c):
        off = pl.multiple_of(c * tk, tk)
        return _dot(k_ref[pl.ds(off, tk), :], qT_ref[slot])

    def update(s_ref, vT):
        m_prev = m_ref[...]
        m_new = jnp.maximum(m_prev, jnp.max(s_ref[...], axis=0, keepdims=True))
        alpha = jnp.exp2(m_prev - m_new)
        p = jnp.exp2(s_ref[...] - m_new).astype(BF16)
        acc_ref[...] = alpha * acc_ref[...] + _dot(vT, p)
        m_ref[...] = m_new

    def start():
        m_ref[...] = jnp.full(m_ref.shape, -jnp.inf, F32)
        acc_ref[...] = jnp.zeros(acc_ref.shape, F32)

    def result():
        return acc_ref[0:LANES, :] / acc_ref[LANES:LANES + 1, :]

    if not (has_tail and n_main >= 4 and n_main % 2 == 0):
        def simple_tile(i, carry):
            prep_q(i, 0)
            start()
            if has_tail:
                st_ref[...] = _dot(kt_ref[...], qT_ref[0])
                update(st_ref, vtT_ref[...])

            def body(c, carry2):
                s0_ref[...] = qk(0, c)
                update(s0_ref, vT_ref[c])
                return carry2

            lax.fori_loop(0, n_main, body, 0)
            finalize(i, result())
            return carry

        lax.fori_loop(0, nq, simple_tile, 0)
        return

    pairs = (n_main - 4) // 2
    prep_q(0, 0)
    s0_ref[...] = qk(0, 0)

    def tile(i, carry):
        cur = i % 2
        start()

        def pair(c):
            s1_ref[...] = qk(cur, c + 1)
            update(s0_ref, vT_ref[c])
            s0_ref[...] = qk(cur, c + 2)
            update(s1_ref, vT_ref[c + 1])

        def body(p, carry2):
            pair(2 * p)
            return carry2

        if pairs:
            lax.fori_loop(0, pairs, body, 0, unroll=3 if pairs % 3 == 0 else 1)
        c = n_main - 4
        pair(c)
        s1_ref[...] = qk(cur, c + 3)
        st_ref[...] = _dot(kt_ref[...], qT_ref[cur])
        update(s0_ref, vT_ref[c + 2])
        prep_q(jnp.minimum(i + 1, nq - 1), 1 - cur)
        s0_ref[...] = qk(1 - cur, 0)
        update(s1_ref, vT_ref[c + 3])
        update(st_ref, vtT_ref[...])
        finalize(i, result())
        return carry

    lax.fori_loop(0, nq, tile, 0)


def _flash_scratch(rows, tk, t, tt):
    return [pltpu.VMEM((2, LANES, rows), BF16), pltpu.VMEM((t // tk, ACC_ROWS, tk), BF16),
            pltpu.VMEM((ACC_ROWS, tt), BF16), pltpu.VMEM((tk, rows), F32), pltpu.VMEM((tk, rows), F32),
            pltpu.VMEM((tt, rows), F32), pltpu.VMEM((1, rows), F32), pltpu.VMEM((ACC_ROWS, rows), F32)]


def _query_blocks_per_head(t, tq, want):
    return max(1, min(want, t // tq))


def _diff_attn_kernel(q_ref, k_ref, v_ref, kt_ref, vt_ref, lam_ref, sw_ref, o_ref,
                      qT_ref, vT_ref, vtT_ref, s0_ref, s1_ref, st_ref, m_ref, acc_ref,
                      *, tq, tk, n_main, has_tail, post_scale):
    @pl.when(pl.program_id(2) == 0)
    def _():
        _transpose_values(v_ref, vt_ref, vT_ref, vtT_ref, tk=tk)

    def rows_of(i):
        return pl.ds(pl.multiple_of(i * tq, tq), tq)

    def prep_q(i, slot):
        q = q_ref[rows_of(i), :].astype(F32)
        lane = lax.broadcasted_iota(jnp.int32, q.shape, 1)
        qT_ref[slot, :, 0:tq] = jnp.where(lane < DIFF_HEAD_DIM, q, 0.0).T.astype(BF16)
        qT_ref[slot, :, tq:2 * tq] = jnp.where(lane >= DIFF_HEAD_DIM, q, 0.0).T.astype(BF16)

    def finalize(i, o_t):
        o = o_t.T
        y = o[0:tq, :] - lam_ref[...] * o[tq:2 * tq, :]
        y = _rms(y) * sw_ref[...] * post_scale
        o_ref[rows_of(i), :] = y.astype(o_ref.dtype)

    _flash_tiles(prep_q, finalize, qT_ref, k_ref, vT_ref, kt_ref, vtT_ref, s0_ref, s1_ref, st_ref, m_ref, acc_ref,
                 nq=q_ref.shape[0] // tq, tk=tk, n_main=n_main, has_tail=has_tail)


def _diff_attn(qk, v, qk_tail, v_tail, lam_vec, subln_w, post_scale, batch, has_tail):
    t = qk.shape[1]
    tq = min(512, t)
    tk = min(1024, t)
    nqb = _query_blocks_per_head(t, tq, 4)
    tqb = t // nqb
    kern = functools.partial(_diff_attn_kernel, tq=tq, tk=tk, n_main=t // tk, has_tail=has_tail,
                             post_scale=post_scale)
    tt = qk_tail.shape[1]
    return pl.pallas_call(
        kern,
        grid=(batch, DIFF_HEADS, nqb),
        in_specs=[pl.BlockSpec((None, tqb, LANES), lambda b, h, i: (b, i, h)),
                  pl.BlockSpec((None, t, LANES), lambda b, h, i: (b, 0, DIFF_HEADS + h)),
                  pl.BlockSpec((None, t, LANES), lambda b, h, i: (b, 0, h)),
                  pl.BlockSpec((None, tt, LANES), lambda b, h, i: (b, 0, DIFF_HEADS + h)),
                  pl.BlockSpec((None, tt, LANES), lambda b, h, i: (b, 0, h)),
                  pl.BlockSpec((1, LANES), lambda b, h, i: (0, 0)),
                  pl.BlockSpec((1, LANES), lambda b, h, i: (0, 0))],
        out_specs=pl.BlockSpec((None, tqb, LANES), lambda b, h, i: (b, i, h)),
        out_shape=jax.ShapeDtypeStruct((batch, t, DIFF_HEADS * LANES), BF16),
        scratch_shapes=_flash_scratch(2 * tq, tk, t, tt),
        compiler_params=_cparams(("parallel", "parallel", "arbitrary")),
        name="diff_attn",
    )(qk, qk, v, qk_tail, v_tail, lam_vec, subln_w)


def _gqa_kernel(q_ref, k_ref, v_ref, kt_ref, vt_ref, o_ref,
                qT_ref, vT_ref, vtT_ref, s0_ref, s1_ref, st_ref, m_ref, acc_ref, *, tq, tk, n_main, has_tail):
    @pl.when(pl.program_id(2) == 0)
    def _():
        _transpose_values(v_ref, vt_ref, vT_ref, vtT_ref, tk=tk)

    def rows_of(i):
        return pl.ds(pl.multiple_of(i * tq, tq), tq)

    def prep_q(i, slot):
        for r in range(GQA_REP):
            qT_ref[slot, :, r * tq:(r + 1) * tq] = _to_bf16_t(q_ref[rows_of(i), r * LANES:(r + 1) * LANES])

    def finalize(i, o_t):
        o = o_t.T
        for r in range(GQA_REP):
            o_ref[rows_of(i), r * LANES:(r + 1) * LANES] = o[r * tq:(r + 1) * tq, :].astype(o_ref.dtype)

    _flash_tiles(prep_q, finalize, qT_ref, k_ref, vT_ref, kt_ref, vtT_ref, s0_ref, s1_ref, st_ref, m_ref, acc_ref,
                 nq=q_ref.shape[0] // tq, tk=tk, n_main=n_main, has_tail=has_tail)


def _gqa_attn(qk, v, qk_tail, v_tail, batch, has_tail):
    t = qk.shape[1]
    tq = min(256, t)
    tk = min(1024, t)
    gw = GQA_REP * LANES
    nqb = _query_blocks_per_head(t, tq, 8)
    tqb = t // nqb
    kern = functools.partial(_gqa_kernel, tq=tq, tk=tk, n_main=t // tk, has_tail=has_tail)
    tt = qk_tail.shape[1]
    return pl.pallas_call(
        kern,
        grid=(batch, GQA_KV_HEADS, nqb),
        in_specs=[pl.BlockSpec((None, tqb, gw), lambda b, g, i: (b, i, g)),
                  pl.BlockSpec((None, t, LANES), lambda b, g, i: (b, 0, GQA_HEADS + g)),
                  pl.BlockSpec((None, t, LANES), lambda b, g, i: (b, 0, g)),
                  pl.BlockSpec((None, tt, LANES), lambda b, g, i: (b, 0, GQA_HEADS + g)),
                  pl.BlockSpec((None, tt, LANES), lambda b, g, i: (b, 0, g))],
        out_specs=pl.BlockSpec((None, tqb, gw), lambda b, g, i: (b, i, g)),
        out_shape=jax.ShapeDtypeStruct((batch, t, GQA_HEADS * LANES), BF16),
        scratch_shapes=_flash_scratch(GQA_REP * tq, tk, t, tt),
        compiler_params=_cparams(("parallel", "parallel", "arbitrary")),
        name="gqa_attn",
    )(qk, qk, v, qk_tail, v_tail)


def _rglru_kernel(xf_ref, xfp_ref, xfn_ref, xb_ref, xbp_ref, xbn_ref, cw_ref, cb_ref, wg_ref, bg_ref, cv_ref,
                  h0_ref, hf_ref, hb_ref, ht_ref, a_scr, b_scr, st_scr, *, tb, nblk):
    i = pl.program_id(1)

    @pl.when(i == 0)
    def _():
        st_scr[...] = h0_ref[...]

    row = lax.broadcasted_iota(jnp.int32, (tb, LRU_WIDTH), 0)

    def gates(d, x_ref, xp_ref, xn_ref, blk):
        x = x_ref[...]
        prev = xp_ref[SUBLANES - 1:SUBLANES, :] * (blk > 0).astype(F32)
        has_next = (blk < nblk - 1).astype(F32)
        nxt0 = xn_ref[0:1, :] * has_next
        nxt1 = xn_ref[1:2, :] * has_next
        xm1 = jnp.where(row == 0, prev, pltpu.roll(x, 1, 0))
        xp1 = jnp.where(row == tb - 1, nxt0, pltpu.roll(x, tb - 1, 0))
        xp2 = jnp.where(row == tb - 2, nxt0, jnp.where(row == tb - 1, nxt1, pltpu.roll(x, tb - 2, 0)))
        y = xm1 * cw_ref[0:1, :] + x * cw_ref[1:2, :] + xp1 * cw_ref[2:3, :] + xp2 * cw_ref[3:4, :] + cb_ref[...]
        yb = y.astype(BF16)
        for c in range(LRU_BLOCKS):
            sl = slice(c * LRU_BLOCK, (c + 1) * LRU_BLOCK)
            z = _dot(yb[:, sl], wg_ref[d, c]) + bg_ref[d, c]
            r = jax.nn.sigmoid(z[:, 0:LRU_BLOCK])
            g = jax.nn.sigmoid(z[:, LRU_BLOCK:2 * LRU_BLOCK])
            log_a = r * cv_ref[d:d + 1, sl]
            a_scr[d, :, sl] = jnp.exp(log_a)
            b_scr[d, :, sl] = jnp.sqrt(_neg_expm1(2.0 * log_a)) * (g * y[:, sl])

    gates(0, xf_ref, xfp_ref, xfn_ref, i)
    gates(1, xb_ref, xbp_ref, xbn_ref, nblk - 1 - i)

    row8 = lax.broadcasted_iota(jnp.int32, (SUBLANES, LRU_WIDTH), 0)
    nt = tb // SUBLANES

    def scan(d, out_ref):
        rev = d == 1

        def body(r, h):
            off = pl.multiple_of((nt - 1 - r if rev else r) * SUBLANES, SUBLANES)
            a8 = a_scr[d, pl.ds(off, SUBLANES), :]
            b8 = b_scr[d, pl.ds(off, SUBLANES), :]
            for s in (1, 2, 4):
                if rev:
                    ok = row8 < SUBLANES - s
                    sh = SUBLANES - s
                else:
                    ok = row8 >= s
                    sh = s
                a_sh = jnp.where(ok, pltpu.roll(a8, sh, 0), 1.0)
                b_sh = jnp.where(ok, pltpu.roll(b8, sh, 0), 0.0)
                b8 = a8 * b_sh + b8
                a8 = a8 * a_sh
            h8 = a8 * h + b8
            out_ref[pl.ds(off, SUBLANES), :] = h8
            return h8[0:1, :] if rev else h8[SUBLANES - 1:SUBLANES, :]

        st_scr[d:d + 1, :] = lax.fori_loop(0, nt, body, st_scr[d:d + 1, :])

    scan(0, hf_ref)
    scan(1, hb_ref)

    @pl.when(i == nblk - 1)
    def _():
        ht_ref[...] = st_scr[...]


def _rglru(gx, seq, batch, conv_w, conv_b, wg, bg, cv, h0):
    m = gx.shape[0]
    tb = min(256, seq)
    nblk = seq // tb
    hb8 = tb // SUBLANES
    last8 = m // SUBLANES - 1
    w = LRU_WIDTH

    def fidx(b, i):
        return b * nblk + i

    def bidx(b, i):
        return b * nblk + nblk - 1 - i

    def specs(idx):
        return [pl.BlockSpec((tb, w), lambda b, i: (idx(b, i), 1)),
                pl.BlockSpec((SUBLANES, w), lambda b, i: (jnp.maximum(idx(b, i) * hb8 - 1, 0), 1)),
                pl.BlockSpec((SUBLANES, w), lambda b, i: (jnp.minimum((idx(b, i) + 1) * hb8, last8), 1))]

    full = lambda shape: pl.BlockSpec(shape, lambda b, i: (0,) * len(shape))
    kern = functools.partial(_rglru_kernel, tb=tb, nblk=nblk)
    return pl.pallas_call(
        kern,
        grid=(batch, nblk),
        in_specs=specs(fidx) + specs(bidx) + [full(conv_w.shape), full((1, w)), full(wg.shape), full(bg.shape),
                                              full(cv.shape), pl.BlockSpec((None, 2, w), lambda b, i: (b, 0, 0))],
        out_specs=[pl.BlockSpec((tb, w), lambda b, i: (fidx(b, i), 0)),
                   pl.BlockSpec((tb, w), lambda b, i: (bidx(b, i), 0)),
                   pl.BlockSpec((None, 2, w), lambda b, i: (b, 0, 0))],
        out_shape=[jax.ShapeDtypeStruct((m, w), F32), jax.ShapeDtypeStruct((m, w), F32),
                   jax.ShapeDtypeStruct((batch, 2, w), F32)],
        scratch_shapes=[pltpu.VMEM((2, tb, w), F32), pltpu.VMEM((2, tb, w), F32), pltpu.VMEM((2, w), F32)],
        compiler_params=_cparams(("parallel", "arbitrary")),
        name="rglru",
    )(gx, gx, gx, gx, gx, gx, conv_w, conv_b.reshape(1, w), wg, bg, cv, h0)


def _hgrn_consts(c):
    t = np.arange(c)
    tinc = (t[None, :] <= t[:, None]).astype(np.float32)
    urev = (t[None, :] > t[:, None]).astype(np.float32)
    blocks = [tinc, urev]
    masks = []
    m = c // 2
    while m >= 1:
        mid = (t // (2 * m)) * (2 * m) + m
        right = t >= mid
        u = t[None, :]
        g = np.where(right[:, None], (u >= mid[:, None]) & (u <= t[:, None]), (u > t[:, None]) & (u < mid[:, None]))
        blocks.append(g.astype(np.float32))
        same = (t[:, None] // (2 * m)) == (t[None, :] // (2 * m))
        masks.append((same & right[:, None] & (~right)[None, :]).astype(np.float32))
        m //= 2
    masks.append(np.eye(c, dtype=np.float32))
    flip = lambda a: a[::-1, ::-1]
    ones = np.ones((16, c), np.float32)
    w = np.stack([np.concatenate(blocks + [ones], 0), np.concatenate([flip(b) for b in blocks] + [ones], 0)])
    cm = np.stack([np.stack(masks), np.stack([flip(a) for a in masks])])
    return w, cm


def _hgrn_kernel(qf_ref, ff_ref, vf_ref, qb_ref, fb_ref, vb_ref, lb_ref, wc_ref, cm_ref, s0_ref,
                 of_ref, ob_ref, st_ref, st_scr, *, c, levels, nchunk):
    i = pl.program_id(1)

    @pl.when(i == 0)
    def _():
        st_scr[...] = s0_ref[...]

    def prep(d, q_ref, f_ref, v_ref):
        q = q_ref[...]
        q = q * jax.nn.sigmoid(q)
        lb = lb_ref[d:d + 1, :]
        f = lb + (1.0 - lb) * jax.nn.sigmoid(f_ref[...])
        kk = 1.0 - f
        g = jnp.log(f)
        g1 = g.astype(BF16)
        g2 = (g - g1.astype(F32)).astype(BF16)
        w = wc_ref[d]
        e = jnp.exp(_dot(w, g1) + _dot(w, g2))
        return q, kk, v_ref[...].astype(BF16), e

    def head(d, h, q, kk, v, e, o_ref):
        sl = slice(h * LANES, (h + 1) * LANES)
        st = st_scr[d, h]
        qh, kh, vh = q[:, sl], kk[:, sl], v[:, sl]
        o = _dot_nt((qh * e[0:c, sl]).astype(BF16), st.astype(BF16))
        sc = cm_ref[d, levels] * _dot_nt(qh.astype(BF16), kh.astype(BF16))
        for l in range(levels):
            el = e[(2 + l) * c:(3 + l) * c, sl]
            sc = sc + cm_ref[d, l] * _dot_nt((qh * el).astype(BF16), (kh * el).astype(BF16))
        o_ref[:, sl] = o + _dot(sc.astype(BF16), vh)
        etot = e[(2 + levels) * c:(2 + levels) * c + 1, sl]
        st_scr[d, h] = st * etot + _dot_tn(vh, (kh * e[c:2 * c, sl]).astype(BF16))

    fwd = prep(0, qf_ref, ff_ref, vf_ref)
    bwd = prep(1, qb_ref, fb_ref, vb_ref)
    for h in range(HGRN_HEADS):
        head(0, h, *fwd, of_ref)
        head(1, h, *bwd, ob_ref)

    @pl.when(i == nchunk - 1)
    def _():
        st_ref[...] = st_scr[...]


def _hgrn(z, seq, batch, lb, s0):
    m = z.shape[0]
    c = min(HGRN_CHUNK, seq)
    nchunk = seq // c
    levels = int(math.log2(c))
    wnp, cmnp = _hgrn_consts(c)
    wc = jnp.asarray(wnp, BF16)
    cm = jnp.asarray(cmnp, F32)
    w = HGRN_WIDTH

    def fidx(b, i):
        return b * nchunk + i

    def bidx(b, i):
        return b * nchunk + nchunk - 1 - i

    blk = lambda idx, col: pl.BlockSpec((c, w), lambda b, i: (idx(b, i), col))
    full = lambda shape: pl.BlockSpec(shape, lambda b, i: (0,) * len(shape))
    st_spec = pl.BlockSpec((None, 2, HGRN_HEADS, LANES, LANES), lambda b, i: (b, 0, 0, 0, 0))
    kern = functools.partial(_hgrn_kernel, c=c, levels=levels, nchunk=nchunk)
    return pl.pallas_call(
        kern,
        grid=(batch, nchunk),
        in_specs=[blk(fidx, 0), blk(fidx, 1), blk(fidx, 3), blk(bidx, 0), blk(bidx, 2), blk(bidx, 3),
                  full(lb.shape), full(wc.shape), full(cm.shape), st_spec],
        out_specs=[pl.BlockSpec((c, w), lambda b, i: (fidx(b, i), 0)),
                   pl.BlockSpec((c, w), lambda b, i: (bidx(b, i), 0)), st_spec],
        out_shape=[jax.ShapeDtypeStruct((m, w), F32), jax.ShapeDtypeStruct((m, w), F32),
                   jax.ShapeDtypeStruct((batch, 2, HGRN_HEADS, LANES, LANES), F32)],
        scratch_shapes=[pltpu.VMEM((2, HGRN_HEADS, LANES, LANES), F32)],
        compiler_params=_cparams(("parallel", "arbitrary")),
        name="hgrn2",
    )(z, z, z, z, z, z, lb, wc, cm, s0)


def _outproj_kernel(x_ref, p0_ref, p1_ref, g_ref, att_ref, nw_ref, w_ref, gt_ref, o_ref, *, mode):
    half = w_ref.shape[0] // 2
    s = p0_ref[...] + p1_ref[...]
    g = g_ref[...]
    if mode == "ab":
        a = s * jax.nn.gelu(g, approximate=True)
    else:
        parts = []
        for h in range(HGRN_HEADS):
            sl = slice(h * LANES, (h + 1) * LANES)
            parts.append(_rms(s[:, sl]) * nw_ref[...])
        a = jnp.concatenate(parts, axis=-1) * (g * jax.nn.sigmoid(g))
    acc = _dot(a.astype(BF16), w_ref[0:half, :]) + _dot(att_ref[...], w_ref[half:2 * half, :])
    o_ref[...] = x_ref[...] + gt_ref[...] * acc


def _outproj(x2d, seq, p0, p1, gsrc, gcol, att, head_norm_w, w_out, mod, mode):
    m, d = x2d.shape
    tm = min(256, seq)
    tpb = seq // tm if mod.shape[0] > 1 else m
    hw = w_out.shape[0] // 2
    kern = functools.partial(_outproj_kernel, mode=mode)
    return pl.pallas_call(
        kern,
        grid=(m // tm,),
        in_specs=[pl.BlockSpec((tm, d), lambda i: (i, 0)),
                  pl.BlockSpec((tm, hw), lambda i: (i, 0)),
                  pl.BlockSpec((tm, hw), lambda i: (i, 0)),
                  pl.BlockSpec((tm, hw), lambda i: (i, gcol)),
                  pl.BlockSpec((tm, hw), lambda i: (i, 0)),
                  pl.BlockSpec((1, LANES), lambda i: (0, 0)),
                  pl.BlockSpec(w_out.shape, lambda i: (0, 0)),
                  pl.BlockSpec((None, None, 1, d), lambda i: (i // tpb, 2, 0, 0))],
        out_specs=pl.BlockSpec((tm, d), lambda i: (i, 0)),
        out_shape=jax.ShapeDtypeStruct((m, d), F32),
        compiler_params=_cparams(("parallel",)),
        name="outproj_" + mode,
    )(x2d, p0, p1, gsrc, att, head_norm_w, w_out, mod)


def _ffn_kernel(x_ref, nw_ref, sh_ref, sc_ref, gt_ref, wg_ref, wu_ref, wd_ref, fw_ref, o_ref, hn_ref, *, final):
    j = pl.program_id(1)

    @pl.when(j == 0)
    def _():
        h = _rms(x_ref[...]) * nw_ref[...]
        hn_ref[...] = (h * (1.0 + sc_ref[...]) + sh_ref[...]).astype(BF16)
        o_ref[...] = jnp.zeros(o_ref.shape, F32)

    hn = hn_ref[...]
    g = _dot(hn, wg_ref[...])
    u = _dot(hn, wu_ref[...])
    a = (g * jax.nn.sigmoid(g) * u).astype(BF16)
    o_ref[...] += _dot(a, wd_ref[...])

    @pl.when(j == pl.num_programs(1) - 1)
    def _():
        y = x_ref[...] + gt_ref[...] * o_ref[...]
        if final:
            y = _rms(y) * fw_ref[...]
        o_ref[...] = y


def _ffn(x2d, seq, norm_w, mod, w_gate, w_up, w_down, final_w, final):
    m, d = x2d.shape
    f = w_gate.shape[1]
    tm = min(512, seq)
    tf = 512
    tpb = seq // tm if mod.shape[0] > 1 else m
    mspec = lambda k: pl.BlockSpec((None, None, 1, d), lambda i, j: (i // tpb, k, 0, 0))
    kern = functools.partial(_ffn_kernel, final=final)
    return pl.pallas_call(
        kern,
        grid=(m // tm, f // tf),
        in_specs=[pl.BlockSpec((tm, d), lambda i, j: (i, 0)),
                  pl.BlockSpec((1, d), lambda i, j: (0, 0)),
                  mspec(3), mspec(4), mspec(5),
                  pl.BlockSpec((d, tf), lambda i, j: (0, j)),
                  pl.BlockSpec((d, tf), lambda i, j: (0, j)),
                  pl.BlockSpec((tf, d), lambda i, j: (j, 0)),
                  pl.BlockSpec((1, d), lambda i, j: (0, 0))],
        out_specs=pl.BlockSpec((tm, d), lambda i, j: (i, 0)),
        out_shape=jax.ShapeDtypeStruct((m, d), F32),
        scratch_shapes=[pltpu.VMEM((tm, d), BF16)],
        compiler_params=_cparams(("parallel", "arbitrary")),
        name="ffn",
    )(x2d, norm_w.reshape(1, d), mod, mod, mod, w_gate, w_up, w_down, final_w.reshape(1, d))


def _rope_tables(rows, head_dim):
    n_freq = head_dim // 4
    half = head_dim // 2
    row = jnp.repeat(jnp.arange(rows, dtype=F32), GRID_W)
    col = jnp.tile(jnp.arange(GRID_W, dtype=F32), rows)
    inv = ROPE_THETA ** (-jnp.arange(n_freq, dtype=F32) / n_freq)
    ang = jnp.concatenate([row[:, None] * inv, col[:, None] * inv], axis=-1)
    cos, sin = jnp.cos(ang), jnp.sin(ang)
    reps = LANES // head_dim
    zero = jnp.zeros_like(sin)
    cos_t = jnp.tile(jnp.concatenate([cos, cos], -1), (1, reps))
    if half * 2 == LANES:
        return cos_t, jnp.concatenate([-sin, sin], -1), None
    sin_a = jnp.tile(jnp.concatenate([-sin, zero], -1), (1, reps))
    sin_b = jnp.tile(jnp.concatenate([zero, sin], -1), (1, reps))
    return cos_t, sin_a, sin_b


def _identity_rope(t, head_dim):
    one = jnp.ones((t, LANES), F32)
    zero = jnp.zeros((t, LANES), F32)
    return (one, zero, None) if head_dim == LANES else (one, zero, zero)


def kernel(x, c, ctx, c_ctx, mod_w, mod_b, norm_mix_w, norm_ffn_w, ffn_w_gate, ffn_w_up, ffn_w_down, ab_w_in, ab_w_out, lru_conv_w, lru_conv_b, lru_wa, lru_ba, lru_wx, lru_bx, lru_lambda, diff_lq1, diff_lk1, diff_lq2, diff_lk2, diff_subln_w, cd_w_in, cd_w_out, hgrn_lb_logits, hgrn_norm_w, gqa_q_norm_w, gqa_k_norm_w, final_norm_w):
    batch, seq, d = x.shape
    clen = ctx.shape[1]
    depth = mod_w.shape[0]
    rows = seq // GRID_W

    cc = jnp.zeros((SUBLANES, d), F32).at[0:batch].set(c).at[batch].set(c_ctx)
    mods = _modulation(cc, mod_w, mod_b)
    lb_cum = jnp.cumsum(jax.nn.softmax(hgrn_lb_logits.astype(F32), axis=1), axis=1)

    xl = x.reshape(batch * seq, d)
    xc = ctx.reshape(batch * clen, d)

    for l in range(depth):
        last = l == depth - 1
        m_lat = mods[l, 0:batch].reshape(batch, N_MOD, 1, d)
        m_ctx = mods[l, batch:batch + 1].reshape(1, N_MOD, 1, d)
        streams = ((_normmod(xl, seq, norm_mix_w[l], m_lat), seq), (_normmod(xc, clen, norm_mix_w[l], m_ctx), clen))
        if l % 2 == 0:
            e = l // 2
            lambda_init = 0.8 - 0.6 * math.exp(-0.3 * l)
            w_in = ab_w_in[e]
            qscale = DIFF_HEAD_DIM ** -0.5 * LOG2E
            w_gx = w_in[:, 0:2048].astype(BF16)
            w_qk = jnp.concatenate([w_in[:, 2048:3072] * qscale, w_in[:, 3072:4096]], axis=1).astype(BF16)
            w_v = w_in[:, 4096:5120].astype(BF16)
            ropes = (_rope_tables(rows, DIFF_HEAD_DIM), _identity_rope(clen, DIFF_HEAD_DIM))
            proj = []
            for (hn, t), rp in zip(streams, ropes):
                gx = _inproj(hn, t, w_gx, F32, 1024)
                qk = _inproj(hn, t, w_qk, BF16, 1024, rope=rp, rope_half=DIFF_HEAD_DIM // 2)
                v = _inproj(hn, t, w_v, BF16, 1024)
                proj.append((gx, qk.reshape(batch, t, 2048), v.reshape(batch, t, 1024)))
            (gx_l, qk_l, v_l), (gx_c, qk_c, v_c) = proj
            wg = jnp.concatenate([lru_wa[e], lru_wx[e]], axis=-1).astype(BF16)
            bg = jnp.concatenate([lru_ba[e].reshape(2, LRU_BLOCKS, 1, LRU_BLOCK),
                                  lru_bx[e].reshape(2, LRU_BLOCKS, 1, LRU_BLOCK)], axis=-1)
            cv = -LRU_C * jax.nn.softplus(-lru_lambda[e].astype(F32))
            h0 = jnp.zeros((batch, 2, LRU_WIDTH), F32)
            hf_c, hb_c, h_ctx = _rglru(gx_c, clen, batch, lru_conv_w[e], lru_conv_b[e], wg, bg, cv, h0)
            hf_l, hb_l, _ = _rglru(gx_l, seq, batch, lru_conv_w[e], lru_conv_b[e], wg, bg, cv, h_ctx)
            lam = (jnp.exp(jnp.sum(diff_lq1[e].astype(F32) * diff_lk1[e].astype(F32)))
                   - jnp.exp(jnp.sum(diff_lq2[e].astype(F32) * diff_lk2[e].astype(F32))) + lambda_init)
            lam_vec = jnp.full((1, LANES), lam, F32)
            sw = diff_subln_w[e].reshape(1, LANES)
            d_l = _diff_attn(qk_l, v_l, qk_c, v_c, lam_vec, sw, 1.0 - lambda_init, batch, True)
            w_out = ab_w_out[e].astype(BF16)
            dummy_nw = jnp.ones((1, LANES), F32)
            xl = _outproj(xl, seq, hf_l, hb_l, gx_l, 0, d_l.reshape(batch * seq, 1024), dummy_nw, w_out, m_lat, "ab")
            if not last:
                d_c = _diff_attn(qk_c, v_c, qk_c, v_c, lam_vec, sw, 1.0 - lambda_init, batch, False)
                xc = _outproj(xc, clen, hf_c, hb_c, gx_c, 0, d_c.reshape(batch * clen, 1024), dummy_nw, w_out,
                              m_ctx, "ab")
        else:
            o = l // 2
            lb = lb_cum[:, l] - lb_cum[:, 0]
            w_in = cd_w_in[o]
            w_z = w_in[:, 0:5120].astype(BF16)
            w_qk = w_in[:, 5120:6400].astype(BF16)
            w_v = w_in[:, 6400:6656].astype(BF16)
            qscale = GQA_HEAD_DIM ** -0.5 * LOG2E
            chunk_w = jnp.concatenate([jnp.tile(gqa_q_norm_w[o] * qscale, GQA_HEADS),
                                       jnp.tile(gqa_k_norm_w[o], GQA_KV_HEADS)]).reshape(1, 1280)
            ropes = (_rope_tables(rows, GQA_HEAD_DIM), _identity_rope(clen, GQA_HEAD_DIM))
            proj = []
            for (hn, t), rp in zip(streams, ropes):
                z = _inproj(hn, t, w_z, F32, 1024)
                qk = _inproj(hn, t, w_qk, BF16, 1280, chunk_w=chunk_w, norm_chunks=10,
                             rope=rp[0:2], rope_half=GQA_HEAD_DIM // 2)
                v = _inproj(hn, t, w_v, BF16, 256)
                proj.append((z, qk.reshape(batch, t, 1280), v.reshape(batch, t, 256)))
            (z_l, qk_l, v_l), (z_c, qk_c, v_c) = proj
            s0 = jnp.zeros((batch, 2, HGRN_HEADS, LANES, LANES), F32)
            of_c, ob_c, s_ctx = _hgrn(z_c, clen, batch, lb, s0)
            of_l, ob_l, _ = _hgrn(z_l, seq, batch, lb, s_ctx)
            att_l = _gqa_attn(qk_l, v_l, qk_c, v_c, batch, True)
            w_out = cd_w_out[o].astype(BF16)
            hnw = hgrn_norm_w[o].reshape(1, LANES)
            xl = _outproj(xl, seq, of_l, ob_l, z_l, 4, att_l.reshape(batch * seq, 1024), hnw, w_out, m_lat, "cd")
            if not last:
                att_c = _gqa_attn(qk_c, v_c, qk_c, v_c, batch, False)
                xc = _outproj(xc, clen, of_c, ob_c, z_c, 4, att_c.reshape(batch * clen, 1024), hnw, w_out,
                              m_ctx, "cd")
        wgt, wup, wdn = ffn_w_gate[l].astype(BF16), ffn_w_up[l].astype(BF16), ffn_w_down[l].astype(BF16)
        xl = _ffn(xl, seq, norm_ffn_w[l], m_lat, wgt, wup, wdn, final_norm_w, last)
        if not last:
            xc = _ffn(xc, clen, norm_ffn_w[l], m_ctx, wgt, wup, wdn, final_norm_w, False)

    return xl.reshape(batch, seq, d)
```

```python
import functools
import math

import numpy as np
import jax
import jax.numpy as jnp
from jax import lax
from jax.experimental import pallas as pl
from jax.experimental.pallas import tpu as pltpu

F32 = jnp.float32
BF16 = jnp.bfloat16

GRID_W = 64
NORM_EPS = 1e-6
ROPE_THETA = 10000.0
N_MOD = 6
LRU_WIDTH = 1024
LRU_BLOCKS = 8
LRU_BLOCK = 128
LRU_C = 8.0
DIFF_HEADS = 8
DIFF_HEAD_DIM = 64
HGRN_HEADS = 8
HGRN_WIDTH = 1024
GQA_HEADS = 8
GQA_KV_HEADS = 2
GQA_REP = 4
GQA_HEAD_DIM = 128
LOG2E = 1.4426950408889634

LANES = 128
SUBLANES = 8
VMEM_LIMIT = 56 * 1024 * 1024

HGRN_CHUNK = 128


def _cparams(sem):
    return pltpu.CompilerParams(dimension_semantics=sem, vmem_limit_bytes=VMEM_LIMIT)


def _dot(a, b):
    return jnp.dot(a, b, preferred_element_type=F32)


def _dot_nt(a, b):
    return lax.dot_general(a, b, (((1,), (1,)), ((), ())), preferred_element_type=F32)


def _dot_tn(a, b):
    return lax.dot_general(a, b, (((0,), (0,)), ((), ())), preferred_element_type=F32)


def _rms(x):
    return x * lax.rsqrt(jnp.mean(x * x, axis=-1, keepdims=True) + NORM_EPS)


def _mod_kernel(c_ref, w_ref, b_ref, o_ref):
    c = c_ref[...]
    a = c * jax.nn.sigmoid(c)
    o_ref[...] = jnp.dot(a, w_ref[...], preferred_element_type=F32,
                         precision=lax.Precision.HIGHEST) + b_ref[...]


def _modulation(cc, mod_w, mod_b):
    depth, d, n = mod_w.shape
    tn = 1024
    return pl.pallas_call(
        _mod_kernel,
        grid=(depth, n // tn),
        in_specs=[pl.BlockSpec((SUBLANES, d), lambda l, j: (0, 0)),
                  pl.BlockSpec((None, d, tn), lambda l, j: (l, 0, j)),
                  pl.BlockSpec((None, 1, tn), lambda l, j: (l, 0, j))],
        out_specs=pl.BlockSpec((None, SUBLANES, tn), lambda l, j: (l, 0, j)),
        out_shape=jax.ShapeDtypeStruct((depth, SUBLANES, n), F32),
        compiler_params=_cparams(("parallel", "parallel")),
        name="modulation",
    )(cc, mod_w, mod_b.reshape(depth, 1, n))


def _normmod_kernel(x_ref, nw_ref, sh_ref, sc_ref, o_ref):
    h = _rms(x_ref[...]) * nw_ref[...]
    o_ref[...] = (h * (1.0 + sc_ref[...]) + sh_ref[...]).astype(o_ref.dtype)


def _normmod(x2d, seq, norm_w, mod):
    m, d = x2d.shape
    tm = min(512, seq)
    tpb = seq // tm if mod.shape[0] > 1 else m
    return pl.pallas_call(
        _normmod_kernel,
        grid=(m // tm,),
        in_specs=[pl.BlockSpec((tm, d), lambda i: (i, 0)),
                  pl.BlockSpec((1, d), lambda i: (0, 0)),
                  pl.BlockSpec((None, None, 1, d), lambda i: (i // tpb, 0, 0, 0)),
                  pl.BlockSpec((None, None, 1, d), lambda i: (i // tpb, 1, 0, 0))],
        out_specs=pl.BlockSpec((tm, d), lambda i: (i, 0)),
        out_shape=jax.ShapeDtypeStruct((m, d), BF16),
        compiler_params=_cparams(("parallel",)),
        name="normmod",
    )(x2d, norm_w.reshape(1, d), mod, mod)


def _inproj_kernel(*refs, n_chunks, norm_chunks, rope_half):
    it = iter(refs)
    x_ref, w_ref = next(it), next(it)
    cw_ref = next(it) if norm_chunks else None
    if rope_half:
        cos_ref, sa_ref = next(it), next(it)
        sb_ref = next(it) if rope_half * 2 != LANES else None
    o_ref = next(it)

    acc = _dot(x_ref[...], w_ref[...])
    for c in range(n_chunks):
        sl = slice(c * LANES, (c + 1) * LANES)
        y = acc[:, sl]
        if c < norm_chunks:
            y = _rms(y) * cw_ref[:, sl]
        if rope_half:
            if rope_half * 2 == LANES:
                y = y * cos_ref[...] + pltpu.roll(y, rope_half, 1) * sa_ref[...]
            else:
                y = (y * cos_ref[...] + pltpu.roll(y, LANES - rope_half, 1) * sa_ref[...]
                     + pltpu.roll(y, rope_half, 1) * sb_ref[...])
        o_ref[:, sl] = y.astype(o_ref.dtype)


def _inproj(hn, seq, w, out_dtype, tn, chunk_w=None, norm_chunks=0, rope=None, rope_half=0):
    m, d = hn.shape
    n = w.shape[1]
    tm = min(1024, seq)
    in_specs = [pl.BlockSpec((tm, d), lambda i, j: (i, 0)),
                pl.BlockSpec((d, tn), lambda i, j: (0, j))]
    args = [hn, w]
    if norm_chunks:
        in_specs.append(pl.BlockSpec((1, tn), lambda i, j: (0, j)))
        args.append(chunk_w)
    if rope_half:
        spt = seq // tm
        for t in rope:
            in_specs.append(pl.BlockSpec((tm, LANES), lambda i, j: (i % spt, 0)))
            args.append(t)
    kern = functools.partial(_inproj_kernel, n_chunks=tn // LANES, norm_chunks=norm_chunks, rope_half=rope_half)
    return pl.pallas_call(
        kern,
        grid=(m // tm, n // tn),
        in_specs=in_specs,
        out_specs=pl.BlockSpec((tm, tn), lambda i, j: (i, j)),
        out_shape=jax.ShapeDtypeStruct((m, n), out_dtype),
        compiler_params=_cparams(("parallel", "parallel")),
        name="inproj",
    )(*args)


ACC_ROWS = LANES + 16


def _to_bf16_t(x):
    return x.astype(F32).T.astype(BF16)


def _transpose_values(v_ref, vt_ref, vT_ref, vtT_ref, *, tk):
    for c in range(v_ref.shape[0] // tk):
        vT_ref[c, 0:LANES, :] = _to_bf16_t(v_ref[c * tk:(c + 1) * tk, :])
        vT_ref[c, LANES:ACC_ROWS, :] = jnp.ones((ACC_ROWS - LANES, tk), BF16)
    vtT_ref[0:LANES, :] = _to_bf16_t(vt_ref[...])
    vtT_ref[LANES:ACC_ROWS, :] = jnp.ones((ACC_ROWS - LANES, vt_ref.shape[0]), BF16)


def _flash_tiles(prep_q, finalize, qT_ref, k_ref, vT_ref, kt_ref, vtT_ref, s0_ref, s1_ref, st_ref, m_ref, acc_ref,
                 *, nq, tk, n_main, has_tail):
    def qk(slot, c):
        off = pl.multiple_of(c * tk, tk)
        return _dot(k_ref[pl.ds(off, tk), :], qT_ref[slot])

    def update(s_ref, vT):
        m_prev = m_ref[...]
        m_new = jnp.maximum(m_prev, jnp.max(s_ref[...], axis=0, keepdims=True))
        alpha = jnp.exp2(m_prev - m_new)
        p = jnp.exp2(s_ref[...] - m_new).astype(BF16)
        acc_ref[...] = alpha * acc_ref[...] + _dot(vT, p)
        m_ref[...] = m_new

    def start():
        m_ref[...] = jnp.full(m_ref.shape, -jnp.inf, F32)
        acc_ref[...] = jnp.zeros(acc_ref.shape, F32)

    def result():
        return acc_ref[0:LANES, :] / acc_ref[LANES:LANES + 1, :]

    if not (has_tail and n_main >= 4 and n_main % 2 == 0):
        def simple_tile(i, carry):
            prep_q(i, 0)
            start()
            if has_tail:
                st_ref[...] = _dot(kt_ref[...], qT_ref[0])
                update(st_ref, vtT_ref[...])

            def body(c, carry2):
                s0_ref[...] = qk(0, c)
                update(s0_ref, vT_ref[c])
                return carry2

            lax.fori_loop(0, n_main, body, 0)
            finalize(i, result())
            return carry

        lax.fori_loop(0, nq, simple_tile, 0)
        return

    pairs = (n_main - 4) // 2
    prep_q(0, 0)
    s0_ref[...] = qk(0, 0)

    def tile(i, carry):
        cur = i % 2
        start()

        def pair(c):
            s1_ref[...] = qk(cur, c + 1)
            update(s0_ref, vT_ref[c])
            s0_ref[...] = qk(cur, c + 2)
            update(s1_ref, vT_ref[c + 1])

        def body(p, carry2):
            pair(2 * p)
            return carry2

        if pairs:
            lax.fori_loop(0, pairs, body, 0, unroll=3 if pairs % 3 == 0 else 1)
        c = n_main - 4
        pair(c)
        s1_ref[...] = qk(cur, c + 3)
        st_ref[...] = _dot(kt_ref[...], qT_ref[cur])
        update(s0_ref, vT_ref[c + 2])
        prep_q(jnp.minimum(i + 1, nq - 1), 1 - cur)
        s0_ref[...] = qk(1 - cur, 0)
        update(s1_ref, vT_ref[c + 3])
        update(st_ref, vtT_ref[...])
        finalize(i, result())
        return carry

    lax.fori_loop(0, nq, tile, 0)


def _flash_scratch(rows, tk, t, tt):
    return [pltpu.VMEM((2, LANES, rows), BF16), pltpu.VMEM((t // tk, ACC_ROWS, tk), BF16),
            pltpu.VMEM((ACC_ROWS, tt), BF16), pltpu.VMEM((tk, rows), F32), pltpu.VMEM((tk, rows), F32),
            pltpu.VMEM((tt, rows), F32), pltpu.VMEM((1, rows), F32), pltpu.VMEM((ACC_ROWS, rows), F32)]


def _query_blocks_per_head(t, tq, want):
    return max(1, min(want, t // tq))


def _diff_attn_kernel(q_ref, k_ref, v_ref, kt_ref, vt_ref, lam_ref, sw_ref, o_ref,
                      qT_ref, vT_ref, vtT_ref, s0_ref, s1_ref, st_ref, m_ref, acc_ref,
                      *, tq, tk, n_main, has_tail, post_scale):
    @pl.when(pl.program_id(2) == 0)
    def _():
        _transpose_values(v_ref, vt_ref, vT_ref, vtT_ref, tk=tk)

    def rows_of(i):
        return pl.ds(pl.multiple_of(i * tq, tq), tq)

    def prep_q(i, slot):
        q = q_ref[rows_of(i), :].astype(F32)
        lane = lax.broadcasted_iota(jnp.int32, q.shape, 1)
        qT_ref[slot, :, 0:tq] = jnp.where(lane < DIFF_HEAD_DIM, q, 0.0).T.astype(BF16)
        qT_ref[slot, :, tq:2 * tq] = jnp.where(lane >= DIFF_HEAD_DIM, q, 0.0).T.astype(BF16)

    def finalize(i, o_t):
        o = o_t.T
        y = o[0:tq, :] - lam_ref[...] * o[tq:2 * tq, :]
        y = _rms(y) * sw_ref[...] * post_scale
        o_ref[rows_of(i), :] = y.astype(o_ref.dtype)

    _flash_tiles(prep_q, finalize, qT_ref, k_ref, vT_ref, kt_ref, vtT_ref, s0_ref, s1_ref, st_ref, m_ref, acc_ref,
                 nq=q_ref.shape[0] // tq, tk=tk, n_main=n_main, has_tail=has_tail)


def _diff_attn(qk, v, qk_tail, v_tail, lam_vec, subln_w, post_scale, batch, has_tail):
    t = qk.shape[1]
    tq = min(512, t)
    tk = min(1024, t)
    nqb = _query_blocks_per_head(t, tq, 4)
    tqb = t // nqb
    kern = functools.partial(_diff_attn_kernel, tq=tq, tk=tk, n_main=t // tk, has_tail=has_tail,
                             post_scale=post_scale)
    tt = qk_tail.shape[1]
    return pl.pallas_call(
        kern,
        grid=(batch, DIFF_HEADS, nqb),
        in_specs=[pl.BlockSpec((None, tqb, LANES), lambda b, h, i: (b, i, h)),
                  pl.BlockSpec((None, t, LANES), lambda b, h, i: (b, 0, DIFF_HEADS + h)),
                  pl.BlockSpec((None, t, LANES), lambda b, h, i: (b, 0, h)),
                  pl.BlockSpec((None, tt, LANES), lambda b, h, i: (b, 0, DIFF_HEADS + h)),
                  pl.BlockSpec((None, tt, LANES), lambda b, h, i: (b, 0, h)),
                  pl.BlockSpec((1, LANES), lambda b, h, i: (0, 0)),
                  pl.BlockSpec((1, LANES), lambda b, h, i: (0, 0))],
        out_specs=pl.BlockSpec((None, tqb, LANES), lambda b, h, i: (b, i, h)),
        out_shape=jax.ShapeDtypeStruct((batch, t, DIFF_HEADS * LANES), BF16),
        scratch_shapes=_flash_scratch(2 * tq, tk, t, tt),
        compiler_params=_cparams(("parallel", "parallel", "arbitrary")),
        name="diff_attn",
    )(qk, qk, v, qk_tail, v_tail, lam_vec, subln_w)


def _gqa_kernel(q_ref, k_ref, v_ref, kt_ref, vt_ref, o_ref,
                qT_ref, vT_ref, vtT_ref, s0_ref, s1_ref, st_ref, m_ref, acc_ref, *, tq, tk, n_main, has_tail):
    @pl.when(pl.program_id(2) == 0)
    def _():
        _transpose_values(v_ref, vt_ref, vT_ref, vtT_ref, tk=tk)

    def rows_of(i):
        return pl.ds(pl.multiple_of(i * tq, tq), tq)

    def prep_q(i, slot):
        for r in range(GQA_REP):
            qT_ref[slot, :, r * tq:(r + 1) * tq] = _to_bf16_t(q_ref[rows_of(i), r * LANES:(r + 1) * LANES])

    def finalize(i, o_t):
        o = o_t.T
        for r in range(GQA_REP):
            o_ref[rows_of(i), r * LANES:(r + 1) * LANES] = o[r * tq:(r + 1) * tq, :].astype(o_ref.dtype)

    _flash_tiles(prep_q, finalize, qT_ref, k_ref, vT_ref, kt_ref, vtT_ref, s0_ref, s1_ref, st_ref, m_ref, acc_ref,
                 nq=q_ref.shape[0] // tq, tk=tk, n_main=n_main, has_tail=has_tail)


def _gqa_attn(qk, v, qk_tail, v_tail, batch, has_tail):
    t = qk.shape[1]
    tq = min(256, t)
    tk = min(1024, t)
    gw = GQA_REP * LANES
    nqb = _query_blocks_per_head(t, tq, 8)
    tqb = t // nqb
    kern = functools.partial(_gqa_kernel, tq=tq, tk=tk, n_main=t // tk, has_tail=has_tail)
    tt = qk_tail.shape[1]
    return pl.pallas_call(
        kern,
        grid=(batch, GQA_KV_HEADS, nqb),
        in_specs=[pl.BlockSpec((None, tqb, gw), lambda b, g, i: (b, i, g)),
                  pl.BlockSpec((None, t, LANES), lambda b, g, i: (b, 0, GQA_HEADS + g)),
                  pl.BlockSpec((None, t, LANES), lambda b, g, i: (b, 0, g)),
                  pl.BlockSpec((None, tt, LANES), lambda b, g, i: (b, 0, GQA_HEADS + g)),
                  pl.BlockSpec((None, tt, LANES), lambda b, g, i: (b, 0, g))],
        out_specs=pl.BlockSpec((None, tqb, gw), lambda b, g, i: (b, i, g)),
        out_shape=jax.ShapeDtypeStruct((batch, t, GQA_HEADS * LANES), BF16),
        scratch_shapes=_flash_scratch(GQA_REP * tq, tk, t, tt),
        compiler_params=_cparams(("parallel", "parallel", "arbitrary")),
        name="gqa_attn",
    )(qk, qk, v, qk_tail, v_tail)


def _rglru_kernel(xf_ref, xfp_ref, xfn_ref, xb_ref, xbp_ref, xbn_ref, cw_ref, cb_ref, wg_ref, bg_ref, cv_ref,
                  h0_ref, hf_ref, hb_ref, ht_ref, a_scr, b_scr, st_scr, *, tb, nblk):
    i = pl.program_id(1)

    @pl.when(i == 0)
    def _():
        st_scr[...] = h0_ref[...]

    row = lax.broadcasted_iota(jnp.int32, (tb, LRU_WIDTH), 0)

    def gates(d, x_ref, xp_ref, xn_ref, blk):
        x = x_ref[...]
        prev = xp_ref[SUBLANES - 1:SUBLANES, :] * (blk > 0).astype(F32)
        has_next = (blk < nblk - 1).astype(F32)
        nxt0 = xn_ref[0:1, :] * has_next
        nxt1 = xn_ref[1:2, :] * has_next
        xm1 = jnp.where(row == 0, prev, pltpu.roll(x, 1, 0))
        xp1 = jnp.where(row == tb - 1, nxt0, pltpu.roll(x, tb - 1, 0))
        xp2 = jnp.where(row == tb - 2, nxt0, jnp.where(row == tb - 1, nxt1, pltpu.roll(x, tb - 2, 0)))
        y = xm1 * cw_ref[0:1, :] + x * cw_ref[1:2, :] + xp1 * cw_ref[2:3, :] + xp2 * cw_ref[3:4, :] + cb_ref[...]
        yb = y.astype(BF16)
        for c in range(LRU_BLOCKS):
            sl = slice(c * LRU_BLOCK, (c + 1) * LRU_BLOCK)
            z = _dot(yb[:, sl], wg_ref[d, c]) + bg_ref[d, c]
            r = jax.nn.sigmoid(z[:, 0:LRU_BLOCK])
            g = jax.nn.sigmoid(z[:, LRU_BLOCK:2 * LRU_BLOCK])
            log_a = r * cv_ref[d:d + 1, sl]
            a = jnp.exp(log_a)
            a_scr[d, :, sl] = a
            b_scr[d, :, sl] = jnp.sqrt(-jnp.tanh(log_a) * (1.0 + a * a)) * (g * y[:, sl])

    gates(0, xf_ref, xfp_ref, xfn_ref, i)
    gates(1, xb_ref, xbp_ref, xbn_ref, nblk - 1 - i)

    row8 = lax.broadcasted_iota(jnp.int32, (SUBLANES, LRU_WIDTH), 0)
    nt = tb // SUBLANES

    def scan(d, out_ref):
        rev = d == 1

        def body(r, h):
            off = pl.multiple_of((nt - 1 - r if rev else r) * SUBLANES, SUBLANES)
            a8 = a_scr[d, pl.ds(off, SUBLANES), :]
            b8 = b_scr[d, pl.ds(off, SUBLANES), :]
            for s in (1, 2, 4):
                if rev:
                    ok = row8 < SUBLANES - s
                    sh = SUBLANES - s
                else:
                    ok = row8 >= s
                    sh = s
                a_sh = jnp.where(ok, pltpu.roll(a8, sh, 0), 1.0)
                b_sh = jnp.where(ok, pltpu.roll(b8, sh, 0), 0.0)
                b8 = a8 * b_sh + b8
                a8 = a8 * a_sh
            h8 = a8 * h + b8
            out_ref[pl.ds(off, SUBLANES), :] = h8
            return h8[0:1, :] if rev else h8[SUBLANES - 1:SUBLANES, :]

        st_scr[d:d + 1, :] = lax.fori_loop(0, nt, body, st_scr[d:d + 1, :])

    scan(0, hf_ref)
    scan(1, hb_ref)

    @pl.when(i == nblk - 1)
    def _():
        ht_ref[...] = st_scr[...]


def _rglru(gx, seq, batch, conv_w, conv_b, wg, bg, cv, h0):
    m = gx.shape[0]
    tb = min(256, seq)
    nblk = seq // tb
    hb8 = tb // SUBLANES
    last8 = m // SUBLANES - 1
    w = LRU_WIDTH

    def fidx(b, i):
        return b * nblk + i

    def bidx(b, i):
        return b * nblk + nblk - 1 - i

    def specs(idx):
        return [pl.BlockSpec((tb, w), lambda b, i: (idx(b, i), 1)),
                pl.BlockSpec((SUBLANES, w), lambda b, i: (jnp.maximum(idx(b, i) * hb8 - 1, 0), 1)),
                pl.BlockSpec((SUBLANES, w), lambda b, i: (jnp.minimum((idx(b, i) + 1) * hb8, last8), 1))]

    full = lambda shape: pl.BlockSpec(shape, lambda b, i: (0,) * len(shape))
    kern = functools.partial(_rglru_kernel, tb=tb, nblk=nblk)
    return pl.pallas_call(
        kern,
        grid=(batch, nblk),
        in_specs=specs(fidx) + specs(bidx) + [full(conv_w.shape), full((1, w)), full(wg.shape), full(bg.shape),
                                              full(cv.shape), pl.BlockSpec((None, 2, w), lambda b, i: (b, 0, 0))],
        out_specs=[pl.BlockSpec((tb, w), lambda b, i: (fidx(b, i), 0)),
                   pl.BlockSpec((tb, w), lambda b, i: (bidx(b, i), 0)),
                   pl.BlockSpec((None, 2, w), lambda b, i: (b, 0, 0))],
        out_shape=[jax.ShapeDtypeStruct((m, w), F32), jax.ShapeDtypeStruct((m, w), F32),
                   jax.ShapeDtypeStruct((batch, 2, w), F32)],
        scratch_shapes=[pltpu.VMEM((2, tb, w), F32), pltpu.VMEM((2, tb, w), F32), pltpu.VMEM((2, w), F32)],
        compiler_params=_cparams(("parallel", "arbitrary")),
        name="rglru",
    )(gx, gx, gx, gx, gx, gx, conv_w, conv_b.reshape(1, w), wg, bg, cv, h0)


def _hgrn_consts(c):
    t = np.arange(c)
    blocks = [(t[None, :] <= t[:, None]).astype(np.float32)]
    masks = []
    m = c // 2
    while m >= 1:
        mid = (t // (2 * m)) * (2 * m) + m
        right = t >= mid
        if m < SUBLANES:
            u = t[None, :]
            g = np.where(right[:, None], (u >= mid[:, None]) & (u <= t[:, None]),
                         (u > t[:, None]) & (u < mid[:, None]))
            blocks.append(g.astype(np.float32))
        same = (t[:, None] // (2 * m)) == (t[None, :] // (2 * m))
        masks.append((same & right[:, None] & (~right)[None, :]).astype(np.float32))
        m //= 2
    masks.append(np.eye(c, dtype=np.float32))
    flip = lambda a: a[::-1, ::-1]
    ones = np.ones((16, c), np.float32)
    w = np.stack([np.concatenate(blocks + [ones], 0), np.concatenate([flip(b) for b in blocks] + [ones], 0)])
    cm = np.stack([np.stack(masks), np.stack([flip(a) for a in masks])])
    return w, cm


def _hgrn_kernel(qf_ref, ff_ref, vf_ref, qb_ref, fb_ref, vb_ref, lb_ref, wc_ref, cm_ref, s0_ref,
                 of_ref, ob_ref, st_ref, st_scr, *, c, levels, nchunk):
    i = pl.program_id(1)

    @pl.when(i == 0)
    def _():
        st_scr[...] = s0_ref[...]

    def prep(d, q_ref, f_ref, v_ref):
        q = q_ref[...]
        q = q * jax.nn.sigmoid(q)
        lb = lb_ref[d:d + 1, :]
        f = lb + (1.0 - lb) * jax.nn.sigmoid(f_ref[...])
        kk = 1.0 - f
        g = jnp.log(f)
        g1 = g.astype(BF16)
        g2 = (g - g1.astype(F32)).astype(BF16)
        w = wc_ref[d]
        x = _dot(w, g1) + _dot(w, g2)
        cum = x[0:c]
        n_small = (w.shape[0] - 16) // c - 1
        tot = x[(1 + n_small) * c:(1 + n_small) * c + 1]
        e_lev = []
        m, small = c // 2, 0
        while m >= 1:
            if m >= SUBLANES:
                xr = cum.reshape(c // (2 * m), 2 * m, HGRN_WIDTH)
                ref = xr[:, m - 1 + d:m + d, :]
                e_lev.append(jnp.exp(-jnp.abs(xr - ref)).reshape(c, HGRN_WIDTH))
            else:
                e_lev.append(jnp.exp(x[(1 + small) * c:(2 + small) * c]))
                small += 1
            m //= 2
        return q, kk, v_ref[...].astype(BF16), jnp.exp(cum), jnp.exp(tot - cum), jnp.exp(tot), e_lev

    def head(d, h, q, kk, v, e_in, e_out, e_tot, e_lev, o_ref):
        sl = slice(h * LANES, (h + 1) * LANES)
        st = st_scr[d, h]
        qh, kh, vh = q[:, sl], kk[:, sl], v[:, sl]
        o = _dot_nt((qh * e_in[:, sl]).astype(BF16), st.astype(BF16))
        sc = cm_ref[d, levels] * _dot_nt(qh.astype(BF16), kh.astype(BF16))
        for l in range(levels):
            el = e_lev[l][:, sl]
            sc = sc + cm_ref[d, l] * _dot_nt((qh * el).astype(BF16), (kh * el).astype(BF16))
        o_ref[:, sl] = o + _dot(sc.astype(BF16), vh)
        st_scr[d, h] = st * e_tot[:, sl] + _dot_tn(vh, (kh * e_out[:, sl]).astype(BF16))

    fwd = prep(0, qf_ref, ff_ref, vf_ref)
    bwd = prep(1, qb_ref, fb_ref, vb_ref)
    for h in range(HGRN_HEADS):
        head(0, h, *fwd, of_ref)
        head(1, h, *bwd, ob_ref)

    @pl.when(i == nchunk - 1)
    def _():
        st_ref[...] = st_scr[...]


def _hgrn(z, seq, batch, lb, s0):
    m = z.shape[0]
    c = min(HGRN_CHUNK, seq)
    nchunk = seq // c
    levels = int(math.log2(c))
    wnp, cmnp = _hgrn_consts(c)
    wc = jnp.asarray(wnp, BF16)
    cm = jnp.asarray(cmnp, F32)
    w = HGRN_WIDTH

    def fidx(b, i):
        return b * nchunk + i

    def bidx(b, i):
        return b * nchunk + nchunk - 1 - i

    blk = lambda idx, col: pl.BlockSpec((c, w), lambda b, i: (idx(b, i), col))
    full = lambda shape: pl.BlockSpec(shape, lambda b, i: (0,) * len(shape))
    st_spec = pl.BlockSpec((None, 2, HGRN_HEADS, LANES, LANES), lambda b, i: (b, 0, 0, 0, 0))
    kern = functools.partial(_hgrn_kernel, c=c, levels=levels, nchunk=nchunk)
    return pl.pallas_call(
        kern,
        grid=(batch, nchunk),
        in_specs=[blk(fidx, 0), blk(fidx, 1), blk(fidx, 3), blk(bidx, 0), blk(bidx, 2), blk(bidx, 3),
                  full(lb.shape), full(wc.shape), full(cm.shape), st_spec],
        out_specs=[pl.BlockSpec((c, w), lambda b, i: (fidx(b, i), 0)),
                   pl.BlockSpec((c, w), lambda b, i: (bidx(b, i), 0)), st_spec],
        out_shape=[jax.ShapeDtypeStruct((m, w), F32), jax.ShapeDtypeStruct((m, w), F32),
                   jax.ShapeDtypeStruct((batch, 2, HGRN_HEADS, LANES, LANES), F32)],
        scratch_shapes=[pltpu.VMEM((2, HGRN_HEADS, LANES, LANES), F32)],
        compiler_params=_cparams(("parallel", "arbitrary")),
        name="hgrn2",
    )(z, z, z, z, z, z, lb, wc, cm, s0)


def _outproj_kernel(x_ref, p0_ref, p1_ref, g_ref, att_ref, nw_ref, w_ref, gt_ref, o_ref, *, mode):
    half = w_ref.shape[0] // 2
    s = p0_ref[...] + p1_ref[...]
    g = g_ref[...]
    if mode == "ab":
        a = s * jax.nn.gelu(g, approximate=True)
    else:
        parts = []
        for h in range(HGRN_HEADS):
            sl = slice(h * LANES, (h + 1) * LANES)
            parts.append(_rms(s[:, sl]) * nw_ref[...])
        a = jnp.concatenate(parts, axis=-1) * (g * jax.nn.sigmoid(g))
    acc = _dot(a.astype(BF16), w_ref[0:half, :]) + _dot(att_ref[...], w_ref[half:2 * half, :])
    o_ref[...] = x_ref[...] + gt_ref[...] * acc


def _outproj(x2d, seq, p0, p1, gsrc, gcol, att, head_norm_w, w_out, mod, mode):
    m, d = x2d.shape
    tm = min(256, seq)
    tpb = seq // tm if mod.shape[0] > 1 else m
    hw = w_out.shape[0] // 2
    kern = functools.partial(_outproj_kernel, mode=mode)
    return pl.pallas_call(
        kern,
        grid=(m // tm,),
        in_specs=[pl.BlockSpec((tm, d), lambda i: (i, 0)),
                  pl.BlockSpec((tm, hw), lambda i: (i, 0)),
                  pl.BlockSpec((tm, hw), lambda i: (i, 0)),
                  pl.BlockSpec((tm, hw), lambda i: (i, gcol)),
                  pl.BlockSpec((tm, hw), lambda i: (i, 0)),
                  pl.BlockSpec((1, LANES), lambda i: (0, 0)),
                  pl.BlockSpec(w_out.shape, lambda i: (0, 0)),
                  pl.BlockSpec((None, None, 1, d), lambda i: (i // tpb, 2, 0, 0))],
        out_specs=pl.BlockSpec((tm, d), lambda i: (i, 0)),
        out_shape=jax.ShapeDtypeStruct((m, d), F32),
        compiler_params=_cparams(("parallel",)),
        name="outproj_" + mode,
    )(x2d, p0, p1, gsrc, att, head_norm_w, w_out, mod)


def _ffn_kernel(x_ref, nw_ref, sh_ref, sc_ref, gt_ref, wg_ref, wu_ref, wd_ref, fw_ref, o_ref, hn_ref, *, final):
    j = pl.program_id(1)

    @pl.when(j == 0)
    def _():
        h = _rms(x_ref[...]) * nw_ref[...]
        hn_ref[...] = (h * (1.0 + sc_ref[...]) + sh_ref[...]).astype(BF16)
        o_ref[...] = jnp.zeros(o_ref.shape, F32)

    hn = hn_ref[...]
    g = _dot(hn, wg_ref[...])
    u = _dot(hn, wu_ref[...])
    a = (g * jax.nn.sigmoid(g) * u).astype(BF16)
    o_ref[...] += _dot(a, wd_ref[...])

    @pl.when(j == pl.num_programs(1) - 1)
    def _():
        y = x_ref[...] + gt_ref[...] * o_ref[...]
        if final:
            y = _rms(y) * fw_ref[...]
        o_ref[...] = y


def _ffn(x2d, seq, norm_w, mod, w_gate, w_up, w_down, final_w, final):
    m, d = x2d.shape
    f = w_gate.shape[1]
    tm = min(1024, seq)
    tf = 256
    tpb = seq // tm if mod.shape[0] > 1 else m
    mspec = lambda k: pl.BlockSpec((None, None, 1, d), lambda i, j: (i // tpb, k, 0, 0))
    kern = functools.partial(_ffn_kernel, final=final)
    return pl.pallas_call(
        kern,
        grid=(m // tm, f // tf),
        in_specs=[pl.BlockSpec((tm, d), lambda i, j: (i, 0)),
                  pl.BlockSpec((1, d), lambda i, j: (0, 0)),
                  mspec(3), mspec(4), mspec(5),
                  pl.BlockSpec((d, tf), lambda i, j: (0, j)),
                  pl.BlockSpec((d, tf), lambda i, j: (0, j)),
                  pl.BlockSpec((tf, d), lambda i, j: (j, 0)),
                  pl.BlockSpec((1, d), lambda i, j: (0, 0))],
        out_specs=pl.BlockSpec((tm, d), lambda i, j: (i, 0)),
        out_shape=jax.ShapeDtypeStruct((m, d), F32),
        scratch_shapes=[pltpu.VMEM((tm, d), BF16)],
        compiler_params=_cparams(("parallel", "arbitrary")),
        name="ffn",
    )(x2d, norm_w.reshape(1, d), mod, mod, mod, w_gate, w_up, w_down, final_w.reshape(1, d))


def _rope_tables(rows, head_dim):
    n_freq = head_dim // 4
    half = head_dim // 2
    row = jnp.repeat(jnp.arange(rows, dtype=F32), GRID_W)
    col = jnp.tile(jnp.arange(GRID_W, dtype=F32), rows)
    inv = ROPE_THETA ** (-jnp.arange(n_freq, dtype=F32) / n_freq)
    ang = jnp.concatenate([row[:, None] * inv, col[:, None] * inv], axis=-1)
    cos, sin = jnp.cos(ang), jnp.sin(ang)
    reps = LANES // head_dim
    zero = jnp.zeros_like(sin)
    cos_t = jnp.tile(jnp.concatenate([cos, cos], -1), (1, reps))
    if half * 2 == LANES:
        return cos_t, jnp.concatenate([-sin, sin], -1), None
    sin_a = jnp.tile(jnp.concatenate([-sin, zero], -1), (1, reps))
    sin_b = jnp.tile(jnp.concatenate([zero, sin], -1), (1, reps))
    return cos_t, sin_a, sin_b


def _identity_rope(t, head_dim):
    one = jnp.ones((t, LANES), F32)
    zero = jnp.zeros((t, LANES), F32)
    return (one, zero, None) if head_dim == LANES else (one, zero, zero)


def kernel(x, c, ctx, c_ctx, mod_w, mod_b, norm_mix_w, norm_ffn_w, ffn_w_gate, ffn_w_up, ffn_w_down, ab_w_in, ab_w_out, lru_conv_w, lru_conv_b, lru_wa, lru_ba, lru_wx, lru_bx, lru_lambda, diff_lq1, diff_lk1, diff_lq2, diff_lk2, diff_subln_w, cd_w_in, cd_w_out, hgrn_lb_logits, hgrn_norm_w, gqa_q_norm_w, gqa_k_norm_w, final_norm_w):
    batch, seq, d = x.shape
    clen = ctx.shape[1]
    depth = mod_w.shape[0]
    rows = seq // GRID_W

    cc = jnp.zeros((SUBLANES, d), F32).at[0:batch].set(c).at[batch].set(c_ctx)
    mods = _modulation(cc, mod_w, mod_b)
    lb_cum = jnp.cumsum(jax.nn.softmax(hgrn_lb_logits.astype(F32), axis=1), axis=1)

    xl = x.reshape(batch * seq, d)
    xc = ctx.reshape(batch * clen, d)

    for l in range(depth):
        last = l == depth - 1
        m_lat = mods[l, 0:batch].reshape(batch, N_MOD, 1, d)
        m_ctx = mods[l, batch:batch + 1].reshape(1, N_MOD, 1, d)
        streams = ((_normmod(xl, seq, norm_mix_w[l], m_lat), seq), (_normmod(xc, clen, norm_mix_w[l], m_ctx), clen))
        if l % 2 == 0:
            e = l // 2
            lambda_init = 0.8 - 0.6 * math.exp(-0.3 * l)
            w_in = ab_w_in[e]
            qscale = DIFF_HEAD_DIM ** -0.5 * LOG2E
            w_gx = w_in[:, 0:2048].astype(BF16)
            w_qk = jnp.concatenate([w_in[:, 2048:3072] * qscale, w_in[:, 3072:4096]], axis=1).astype(BF16)
            w_v = w_in[:, 4096:5120].astype(BF16)
            ropes = (_rope_tables(rows, DIFF_HEAD_DIM), _identity_rope(clen, DIFF_HEAD_DIM))
            proj = []
            for (hn, t), rp in zip(streams, ropes):
                gx = _inproj(hn, t, w_gx, F32, 1024)
                qk = _inproj(hn, t, w_qk, BF16, 1024, rope=rp, rope_half=DIFF_HEAD_DIM // 2)
                v = _inproj(hn, t, w_v, BF16, 1024)
                proj.append((gx, qk.reshape(batch, t, 2048), v.reshape(batch, t, 1024)))
            (gx_l, qk_l, v_l), (gx_c, qk_c, v_c) = proj
            wg = jnp.concatenate([lru_wa[e], lru_wx[e]], axis=-1).astype(BF16)
            bg = jnp.concatenate([lru_ba[e].reshape(2, LRU_BLOCKS, 1, LRU_BLOCK),
                                  lru_bx[e].reshape(2, LRU_BLOCKS, 1, LRU_BLOCK)], axis=-1)
            cv = -LRU_C * jax.nn.softplus(-lru_lambda[e].astype(F32))
            h0 = jnp.zeros((batch, 2, LRU_WIDTH), F32)
            hf_c, hb_c, h_ctx = _rglru(gx_c, clen, batch, lru_conv_w[e], lru_conv_b[e], wg, bg, cv, h0)
            hf_l, hb_l, _ = _rglru(gx_l, seq, batch, lru_conv_w[e], lru_conv_b[e], wg, bg, cv, h_ctx)
            lam = (jnp.exp(jnp.sum(diff_lq1[e].astype(F32) * diff_lk1[e].astype(F32)))
                   - jnp.exp(jnp.sum(diff_lq2[e].astype(F32) * diff_lk2[e].astype(F32))) + lambda_init)
            lam_vec = jnp.full((1, LANES), lam, F32)
            sw = diff_subln_w[e].reshape(1, LANES)
            d_l = _diff_attn(qk_l, v_l, qk_c, v_c, lam_vec, sw, 1.0 - lambda_init, batch, True)
            w_out = ab_w_out[e].astype(BF16)
            dummy_nw = jnp.ones((1, LANES), F32)
            xl = _outproj(xl, seq, hf_l, hb_l, gx_l, 0, d_l.reshape(batch * seq, 1024), dummy_nw, w_out, m_lat, "ab")
            if not last:
                d_c = _diff_attn(qk_c, v_c, qk_c, v_c, lam_vec, sw, 1.0 - lambda_init, batch, False)
                xc = _outproj(xc, clen, hf_c, hb_c, gx_c, 0, d_c.reshape(batch * clen, 1024), dummy_nw, w_out,
                              m_ctx, "ab")
        else:
            o = l // 2
            lb = lb_cum[:, l] - lb_cum[:, 0]
            w_in = cd_w_in[o]
            w_z = w_in[:, 0:5120].astype(BF16)
            w_qk = w_in[:, 5120:6400].astype(BF16)
            w_v = w_in[:, 6400:6656].astype(BF16)
            qscale = GQA_HEAD_DIM ** -0.5 * LOG2E
            chunk_w = jnp.concatenate([jnp.tile(gqa_q_norm_w[o] * qscale, GQA_HEADS),
                                       jnp.tile(gqa_k_norm_w[o], GQA_KV_HEADS)]).reshape(1, 1280)
            ropes = (_rope_tables(rows, GQA_HEAD_DIM), _identity_rope(clen, GQA_HEAD_DIM))
            proj = []
            for (hn, t), rp in zip(streams, ropes):
                z = _inproj(hn, t, w_z, F32, 1024)
                qk = _inproj(hn, t, w_qk, BF16, 1280, chunk_w=chunk_w, norm_chunks=10,
                             rope=rp[0:2], rope_half=GQA_HEAD_DIM // 2)
                v = _inproj(hn, t, w_v, BF16, 256)
                proj.append((z, qk.reshape(batch, t, 1280), v.reshape(batch, t, 256)))
            (z_l, qk_l, v_l), (z_c, qk_c, v_c) = proj
            s0 = jnp.zeros((batch, 2, HGRN_HEADS, LANES, LANES), F32)
            of_c, ob_c, s_ctx = _hgrn(z_c, clen, batch, lb, s0)
            of_l, ob_l, _ = _hgrn(z_l, seq, batch, lb, s_ctx)
            att_l = _gqa_attn(qk_l, v_l, qk_c, v_c, batch, True)
            w_out = cd_w_out[o].astype(BF16)
            hnw = hgrn_norm_w[o].reshape(1, LANES)
            xl = _outproj(xl, seq, of_l, ob_l, z_l, 4, att_l.reshape(batch * seq, 1024), hnw, w_out, m_lat, "cd")
            if not last:
                att_c = _gqa_attn(qk_c, v_c, qk_c, v_c, batch, False)
                xc = _outproj(xc, clen, of_c, ob_c, z_c, 4, att_c.reshape(batch * clen, 1024), hnw, w_out,
                              m_ctx, "cd")
        wgt, wup, wdn = ffn_w_gate[l].astype(BF16), ffn_w_up[l].astype(BF16), ffn_w_down[l].astype(BF16)
        xl = _ffn(xl, seq, norm_ffn_w[l], m_lat, wgt, wup, wdn, final_norm_w, last)
        if not last:
            xc = _ffn(xc, clen, norm_ffn_w[l], m_ctx, wgt, wup, wdn, final_norm_w, False)

    return xl.reshape(batch, seq, d)
```

```python
import functools
import math

import numpy as np
import jax
import jax.numpy as jnp
from jax import lax
from jax.experimental import pallas as pl
from jax.experimental.pallas import tpu as pltpu

F32 = jnp.float32
BF16 = jnp.bfloat16

GRID_W = 64
NORM_EPS = 1e-6
ROPE_THETA = 10000.0
N_MOD = 6
LRU_WIDTH = 1024
LRU_BLOCKS = 8
LRU_BLOCK = 128
LRU_C = 8.0
DIFF_HEADS = 8
DIFF_HEAD_DIM = 64
HGRN_HEADS = 8
HGRN_WIDTH = 1024
GQA_HEADS = 8
GQA_KV_HEADS = 2
GQA_REP = 4
GQA_HEAD_DIM = 128
LOG2E = 1.4426950408889634

LANES = 128
SUBLANES = 8
VMEM_LIMIT = 56 * 1024 * 1024

HGRN_CHUNK = 128


def _cparams(sem):
    return pltpu.CompilerParams(dimension_semantics=sem, vmem_limit_bytes=VMEM_LIMIT)


def _dot(a, b):
    return jnp.dot(a, b, preferred_element_type=F32)


def _dot_nt(a, b):
    return lax.dot_general(a, b, (((1,), (1,)), ((), ())), preferred_element_type=F32)


def _dot_tn(a, b):
    return lax.dot_general(a, b, (((0,), (0,)), ((), ())), preferred_element_type=F32)


def _rms(x):
    return x * lax.rsqrt(jnp.mean(x * x, axis=-1, keepdims=True) + NORM_EPS)


def _mod_kernel(c_ref, w_ref, b_ref, o_ref):
    c = c_ref[...]
    a = c * jax.nn.sigmoid(c)
    o_ref[...] = jnp.dot(a, w_ref[...], preferred_element_type=F32,
                         precision=lax.Precision.HIGHEST) + b_ref[...]


def _modulation(cc, mod_w, mod_b):
    depth, d, n = mod_w.shape
    tn = 1024
    return pl.pallas_call(
        _mod_kernel,
        grid=(depth, n // tn),
        in_specs=[pl.BlockSpec((SUBLANES, d), lambda l, j: (0, 0)),
                  pl.BlockSpec((None, d, tn), lambda l, j: (l, 0, j)),
                  pl.BlockSpec((None, 1, tn), lambda l, j: (l, 0, j))],
        out_specs=pl.BlockSpec((None, SUBLANES, tn), lambda l, j: (l, 0, j)),
        out_shape=jax.ShapeDtypeStruct((depth, SUBLANES, n), F32),
        compiler_params=_cparams(("parallel", "parallel")),
        name="modulation",
    )(cc, mod_w, mod_b.reshape(depth, 1, n))


def _normmod_kernel(x_ref, nw_ref, sh_ref, sc_ref, o_ref):
    h = _rms(x_ref[...]) * nw_ref[...]
    o_ref[...] = (h * (1.0 + sc_ref[...]) + sh_ref[...]).astype(o_ref.dtype)


def _normmod(x2d, seq, norm_w, mod):
    m, d = x2d.shape
    tm = min(512, seq)
    tpb = seq // tm if mod.shape[0] > 1 else m
    return pl.pallas_call(
        _normmod_kernel,
        grid=(m // tm,),
        in_specs=[pl.BlockSpec((tm, d), lambda i: (i, 0)),
                  pl.BlockSpec((1, d), lambda i: (0, 0)),
                  pl.BlockSpec((None, None, 1, d), lambda i: (i // tpb, 0, 0, 0)),
                  pl.BlockSpec((None, None, 1, d), lambda i: (i // tpb, 1, 0, 0))],
        out_specs=pl.BlockSpec((tm, d), lambda i: (i, 0)),
        out_shape=jax.ShapeDtypeStruct((m, d), BF16),
        compiler_params=_cparams(("parallel",)),
        name="normmod",
    )(x2d, norm_w.reshape(1, d), mod, mod)


def _inproj_kernel(*refs, n_chunks, norm_chunks, rope_half):
    it = iter(refs)
    x_ref, w_ref = next(it), next(it)
    cw_ref = next(it) if norm_chunks else None
    if rope_half:
        cos_ref, sa_ref = next(it), next(it)
        sb_ref = next(it) if rope_half * 2 != LANES else None
    o_ref = next(it)

    acc = _dot(x_ref[...], w_ref[...])
    for c in range(n_chunks):
        sl = slice(c * LANES, (c + 1) * LANES)
        y = acc[:, sl]
        if c < norm_chunks:
            y = _rms(y) * cw_ref[:, sl]
        if rope_half:
            if rope_half * 2 == LANES:
                y = y * cos_ref[...] + pltpu.roll(y, rope_half, 1) * sa_ref[...]
            else:
                y = (y * cos_ref[...] + pltpu.roll(y, LANES - rope_half, 1) * sa_ref[...]
                     + pltpu.roll(y, rope_half, 1) * sb_ref[...])
        o_ref[:, sl] = y.astype(o_ref.dtype)


def _inproj(hn, seq, w, out_dtype, tn, chunk_w=None, norm_chunks=0, rope=None, rope_half=0):
    m, d = hn.shape
    n = w.shape[1]
    tm = min(1024, seq)
    in_specs = [pl.BlockSpec((tm, d), lambda i, j: (i, 0)),
                pl.BlockSpec((d, tn), lambda i, j: (0, j))]
    args = [hn, w]
    if norm_chunks:
        in_specs.append(pl.BlockSpec((1, tn), lambda i, j: (0, j)))
        args.append(chunk_w)
    if rope_half:
        spt = seq // tm
        for t in rope:
            in_specs.append(pl.BlockSpec((tm, LANES), lambda i, j: (i % spt, 0)))
            args.append(t)
    kern = functools.partial(_inproj_kernel, n_chunks=tn // LANES, norm_chunks=norm_chunks, rope_half=rope_half)
    return pl.pallas_call(
        kern,
        grid=(m // tm, n // tn),
        in_specs=in_specs,
        out_specs=pl.BlockSpec((tm, tn), lambda i, j: (i, j)),
        out_shape=jax.ShapeDtypeStruct((m, n), out_dtype),
        compiler_params=_cparams(("parallel", "parallel")),
        name="inproj",
    )(*args)


ACC_ROWS = LANES + 16


def _to_bf16_t(x):
    return x.astype(F32).T.astype(BF16)


def _transpose_values(v_ref, vt_ref, vT_ref, vtT_ref, *, tk):
    for c in range(v_ref.shape[0] // tk):
        vT_ref[c, 0:LANES, :] = _to_bf16_t(v_ref[c * tk:(c + 1) * tk, :])
        vT_ref[c, LANES:ACC_ROWS, :] = jnp.ones((ACC_ROWS - LANES, tk), BF16)
    vtT_ref[0:LANES, :] = _to_bf16_t(vt_ref[...])
    vtT_ref[LANES:ACC_ROWS, :] = jnp.ones((ACC_ROWS - LANES, vt_ref.shape[0]), BF16)


def _flash_tiles(prep_q, finalize, qT_ref, k_ref, vT_ref, kt_ref, vtT_ref, s0_ref, s1_ref, st_ref, m_ref, acc_ref,
                 *, nq, tk, n_main, has_tail):
    def qk(slot, c):
        off = pl.multiple_of(c * tk, tk)
        return _dot(k_ref[pl.ds(off, tk), :], qT_ref[slot])

    def update(s_ref, vT):
        m_prev = m_ref[...]
        m_new = jnp.maximum(m_prev, jnp.max(s_ref[...], axis=0, keepdims=True))
        alpha = jnp.exp2(m_prev - m_new)
        p = jnp.exp2(s_ref[...] - m_new).astype(BF16)
        acc_ref[...] = alpha * acc_ref[...] + _dot(vT, p)
        m_ref[...] = m_new

    def start():
        m_ref[...] = jnp.full(m_ref.shape, -jnp.inf, F32)
        acc_ref[...] = jnp.zeros(acc_ref.shape, F32)

    def result():
        return acc_ref[0:LANES, :] / acc_ref[LANES:LANES + 1, :]

    if not (has_tail and n_main >= 4 and n_main % 2 == 0):
        def simple_tile(i, carry):
            prep_q(i, 0)
            start()
            if has_tail:
                st_ref[...] = _dot(kt_ref[...], qT_ref[0])
                update(st_ref, vtT_ref[...])

            def body(c, carry2):
                s0_ref[...] = qk(0, c)
                update(s0_ref, vT_ref[c])
                return carry2

            lax.fori_loop(0, n_main, body, 0)
            finalize(i, result())
            return carry

        lax.fori_loop(0, nq, simple_tile, 0)
        return

    pairs = (n_main - 4) // 2
    prep_q(0, 0)
    s0_ref[...] = qk(0, 0)

    def tile(i, carry):
        cur = i % 2
        start()

        def pair(c):
            s1_ref[...] = qk(cur, c + 1)
            update(s0_ref, vT_ref[c])
            s0_ref[...] = qk(cur, c + 2)
            update(s1_ref, vT_ref[c + 1])

        def body(p, carry2):
            pair(2 * p)
            return carry2

        if pairs:
            lax.fori_loop(0, pairs, body, 0, unroll=3 if pairs % 3 == 0 else 1)
        c = n_main - 4
        pair(c)
        s1_ref[...] = qk(cur, c + 3)
        st_ref[...] = _dot(kt_ref[...], qT_ref[cur])
        update(s0_ref, vT_ref[c + 2])
        prep_q(jnp.minimum(i + 1, nq - 1), 1 - cur)
        s0_ref[...] = qk(1 - cur, 0)
        update(s1_ref, vT_ref[c + 3])
        update(st_ref, vtT_ref[...])
        finalize(i, result())
        return carry

    lax.fori_loop(0, nq, tile, 0)


def _flash_scratch(rows, tk, t, tt):
    return [pltpu.VMEM((2, LANES, rows), BF16), pltpu.VMEM((t // tk, ACC_ROWS, tk), BF16),
            pltpu.VMEM((ACC_ROWS, tt), BF16), pltpu.VMEM((tk, rows), F32), pltpu.VMEM((tk, rows), F32),
            pltpu.VMEM((tt, rows), F32), pltpu.VMEM((1, rows), F32), pltpu.VMEM((ACC_ROWS, rows), F32)]


def _query_blocks_per_head(t, tq, want):
    return max(1, min(want, t // tq))


def _diff_attn_kernel(q_ref, k_ref, v_ref, kt_ref, vt_ref, lam_ref, sw_ref, o_ref,
                      qT_ref, vT_ref, vtT_ref, s0_ref, s1_ref, st_ref, m_ref, acc_ref,
                      *, tq, tk, n_main, has_tail, post_scale):
    @pl.when(pl.program_id(2) == 0)
    def _():
        _transpose_values(v_ref, vt_ref, vT_ref, vtT_ref, tk=tk)

    def rows_of(i):
        return pl.ds(pl.multiple_of(i * tq, tq), tq)

    def prep_q(i, slot):
        q = q_ref[rows_of(i), :].astype(F32)
        lane = lax.broadcasted_iota(jnp.int32, q.shape, 1)
        qT_ref[slot, :, 0:tq] = jnp.where(lane < DIFF_HEAD_DIM, q, 0.0).T.astype(BF16)
        qT_ref[slot, :, tq:2 * tq] = jnp.where(lane >= DIFF_HEAD_DIM, q, 0.0).T.astype(BF16)

    def finalize(i, o_t):
        o = o_t.T
        y = o[0:tq, :] - lam_ref[...] * o[tq:2 * tq, :]
        y = _rms(y) * sw_ref[...] * post_scale
        o_ref[rows_of(i), :] = y.astype(o_ref.dtype)

    _flash_tiles(prep_q, finalize, qT_ref, k_ref, vT_ref, kt_ref, vtT_ref, s0_ref, s1_ref, st_ref, m_ref, acc_ref,
                 nq=q_ref.shape[0] // tq, tk=tk, n_main=n_main, has_tail=has_tail)


def _diff_attn(qk, v, qk_tail, v_tail, lam_vec, subln_w, post_scale, batch, has_tail):
    t = qk.shape[1]
    tq = min(512, t)
    tk = min(1024, t)
    nqb = _query_blocks_per_head(t, tq, 4)
    tqb = t // nqb
    kern = functools.partial(_diff_attn_kernel, tq=tq, tk=tk, n_main=t // tk, has_tail=has_tail,
                             post_scale=post_scale)
    tt = qk_tail.shape[1]
    return pl.pallas_call(
        kern,
        grid=(batch, DIFF_HEADS, nqb),
        in_specs=[pl.BlockSpec((None, tqb, LANES), lambda b, h, i: (b, i, h)),
                  pl.BlockSpec((None, t, LANES), lambda b, h, i: (b, 0, DIFF_HEADS + h)),
                  pl.BlockSpec((None, t, LANES), lambda b, h, i: (b, 0, h)),
                  pl.BlockSpec((None, tt, LANES), lambda b, h, i: (b, 0, DIFF_HEADS + h)),
                  pl.BlockSpec((None, tt, LANES), lambda b, h, i: (b, 0, h)),
                  pl.BlockSpec((1, LANES), lambda b, h, i: (0, 0)),
                  pl.BlockSpec((1, LANES), lambda b, h, i: (0, 0))],
        out_specs=pl.BlockSpec((None, tqb, LANES), lambda b, h, i: (b, i, h)),
        out_shape=jax.ShapeDtypeStruct((batch, t, DIFF_HEADS * LANES), BF16),
        scratch_shapes=_flash_scratch(2 * tq, tk, t, tt),
        compiler_params=_cparams(("parallel", "parallel", "arbitrary")),
        name="diff_attn",
    )(qk, qk, v, qk_tail, v_tail, lam_vec, subln_w)


def _gqa_kernel(q_ref, k_ref, v_ref, kt_ref, vt_ref, o_ref,
                qT_ref, vT_ref, vtT_ref, s0_ref, s1_ref, st_ref, m_ref, acc_ref, *, tq, tk, n_main, has_tail):
    @pl.when(pl.program_id(2) == 0)
    def _():
        _transpose_values(v_ref, vt_ref, vT_ref, vtT_ref, tk=tk)

    def rows_of(i):
        return pl.ds(pl.multiple_of(i * tq, tq), tq)

    def prep_q(i, slot):
        for r in range(GQA_REP):
            qT_ref[slot, :, r * tq:(r + 1) * tq] = _to_bf16_t(q_ref[rows_of(i), r * LANES:(r + 1) * LANES])

    def finalize(i, o_t):
        o = o_t.T
        for r in range(GQA_REP):
            o_ref[rows_of(i), r * LANES:(r + 1) * LANES] = o[r * tq:(r + 1) * tq, :].astype(o_ref.dtype)

    _flash_tiles(prep_q, finalize, qT_ref, k_ref, vT_ref, kt_ref, vtT_ref, s0_ref, s1_ref, st_ref, m_ref, acc_ref,
                 nq=q_ref.shape[0] // tq, tk=tk, n_main=n_main, has_tail=has_tail)


def _gqa_attn(qk, v, qk_tail, v_tail, batch, has_tail):
    t = qk.shape[1]
    tq = min(256, t)
    tk = min(1024, t)
    gw = GQA_REP * LANES
    nqb = _query_blocks_per_head(t, tq, 8)
    tqb = t // nqb
    kern = functools.partial(_gqa_kernel, tq=tq, tk=tk, n_main=t // tk, has_tail=has_tail)
    tt = qk_tail.shape[1]
    return pl.pallas_call(
        kern,
        grid=(batch, GQA_KV_HEADS, nqb),
        in_specs=[pl.BlockSpec((None, tqb, gw), lambda b, g, i: (b, i, g)),
                  pl.BlockSpec((None, t, LANES), lambda b, g, i: (b, 0, GQA_HEADS + g)),
                  pl.BlockSpec((None, t, LANES), lambda b, g, i: (b, 0, g)),
                  pl.BlockSpec((None, tt, LANES), lambda b, g, i: (b, 0, GQA_HEADS + g)),
                  pl.BlockSpec((None, tt, LANES), lambda b, g, i: (b, 0, g))],
        out_specs=pl.BlockSpec((None, tqb, gw), lambda b, g, i: (b, i, g)),
        out_shape=jax.ShapeDtypeStruct((batch, t, GQA_HEADS * LANES), BF16),
        scratch_shapes=_flash_scratch(GQA_REP * tq, tk, t, tt),
        compiler_params=_cparams(("parallel", "parallel", "arbitrary")),
        name="gqa_attn",
    )(qk, qk, v, qk_tail, v_tail)


def _rglru_kernel(xf_ref, xfp_ref, xfn_ref, xb_ref, xbp_ref, xbn_ref, cw_ref, cb_ref, wg_ref, bg_ref, cv_ref,
                  h0_ref, hf_ref, hb_ref, ht_ref, a_scr, b_scr, st_scr, *, tb, nblk):
    i = pl.program_id(1)

    @pl.when(i == 0)
    def _():
        st_scr[...] = h0_ref[...]

    row = lax.broadcasted_iota(jnp.int32, (tb, LRU_WIDTH), 0)

    def gates(d, x_ref, xp_ref, xn_ref, blk):
        x = x_ref[...]
        prev = xp_ref[SUBLANES - 1:SUBLANES, :] * (blk > 0).astype(F32)
        has_next = (blk < nblk - 1).astype(F32)
        nxt0 = xn_ref[0:1, :] * has_next
        nxt1 = xn_ref[1:2, :] * has_next
        xm1 = jnp.where(row == 0, prev, pltpu.roll(x, 1, 0))
        xp1 = jnp.where(row == tb - 1, nxt0, pltpu.roll(x, tb - 1, 0))
        xp2 = jnp.where(row == tb - 2, nxt0, jnp.where(row == tb - 1, nxt1, pltpu.roll(x, tb - 2, 0)))
        y = xm1 * cw_ref[0:1, :] + x * cw_ref[1:2, :] + xp1 * cw_ref[2:3, :] + xp2 * cw_ref[3:4, :] + cb_ref[...]
        yb = y.astype(BF16)
        for c in range(LRU_BLOCKS):
            sl = slice(c * LRU_BLOCK, (c + 1) * LRU_BLOCK)
            z = _dot(yb[:, sl], wg_ref[d, c]) + bg_ref[d, c]
            r = jax.nn.sigmoid(z[:, 0:LRU_BLOCK])
            g = jax.nn.sigmoid(z[:, LRU_BLOCK:2 * LRU_BLOCK])
            log_a = r * cv_ref[d:d + 1, sl]
            a = jnp.exp(log_a)
            a_scr[d, :, sl] = a
            b_scr[d, :, sl] = jnp.sqrt(-jnp.tanh(log_a) * (1.0 + a * a)) * (g * y[:, sl])

    gates(0, xf_ref, xfp_ref, xfn_ref, i)
    gates(1, xb_ref, xbp_ref, xbn_ref, nblk - 1 - i)

    row8 = lax.broadcasted_iota(jnp.int32, (SUBLANES, LRU_WIDTH), 0)
    nt = tb // SUBLANES

    def scan(d, out_ref):
        rev = d == 1

        def body(r, h):
            off = pl.multiple_of((nt - 1 - r if rev else r) * SUBLANES, SUBLANES)
            a8 = a_scr[d, pl.ds(off, SUBLANES), :]
            b8 = b_scr[d, pl.ds(off, SUBLANES), :]
            for s in (1, 2, 4):
                if rev:
                    ok = row8 < SUBLANES - s
                    sh = SUBLANES - s
                else:
                    ok = row8 >= s
                    sh = s
                a_sh = jnp.where(ok, pltpu.roll(a8, sh, 0), 1.0)
                b_sh = jnp.where(ok, pltpu.roll(b8, sh, 0), 0.0)
                b8 = a8 * b_sh + b8
                a8 = a8 * a_sh
            h8 = a8 * h + b8
            out_ref[pl.ds(off, SUBLANES), :] = h8
            return h8[0:1, :] if rev else h8[SUBLANES - 1:SUBLANES, :]

        st_scr[d:d + 1, :] = lax.fori_loop(0, nt, body, st_scr[d:d + 1, :])

    scan(0, hf_ref)
    scan(1, hb_ref)

    @pl.when(i == nblk - 1)
    def _():
        ht_ref[...] = st_scr[...]


def _rglru(gx, seq, batch, conv_w, conv_b, wg, bg, cv, h0):
    m = gx.shape[0]
    tb = min(256, seq)
    nblk = seq // tb
    hb8 = tb // SUBLANES
    last8 = m // SUBLANES - 1
    w = LRU_WIDTH

    def fidx(b, i):
        return b * nblk + i

    def bidx(b, i):
        return b * nblk + nblk - 1 - i

    def specs(idx):
        return [pl.BlockSpec((tb, w), lambda b, i: (idx(b, i), 1)),
                pl.BlockSpec((SUBLANES, w), lambda b, i: (jnp.maximum(idx(b, i) * hb8 - 1, 0), 1)),
                pl.BlockSpec((SUBLANES, w), lambda b, i: (jnp.minimum((idx(b, i) + 1) * hb8, last8), 1))]

    full = lambda shape: pl.BlockSpec(shape, lambda b, i: (0,) * len(shape))
    kern = functools.partial(_rglru_kernel, tb=tb, nblk=nblk)
    return pl.pallas_call(
        kern,
        grid=(batch, nblk),
        in_specs=specs(fidx) + specs(bidx) + [full(conv_w.shape), full((1, w)), full(wg.shape), full(bg.shape),
                                              full(cv.shape), pl.BlockSpec((None, 2, w), lambda b, i: (b, 0, 0))],
        out_specs=[pl.BlockSpec((tb, w), lambda b, i: (fidx(b, i), 0)),
                   pl.BlockSpec((tb, w), lambda b, i: (bidx(b, i), 0)),
                   pl.BlockSpec((None, 2, w), lambda b, i: (b, 0, 0))],
        out_shape=[jax.ShapeDtypeStruct((m, w), F32), jax.ShapeDtypeStruct((m, w), F32),
                   jax.ShapeDtypeStruct((batch, 2, w), F32)],
        scratch_shapes=[pltpu.VMEM((2, tb, w), F32), pltpu.VMEM((2, tb, w), F32), pltpu.VMEM((2, w), F32)],
        compiler_params=_cparams(("parallel", "arbitrary")),
        name="rglru",
    )(gx, gx, gx, gx, gx, gx, conv_w, conv_b.reshape(1, w), wg, bg, cv, h0)


def _hgrn_consts(c):
    t = np.arange(c)
    blocks = [(t[None, :] <= t[:, None]).astype(np.float32)]
    masks = []
    m = c // 2
    while m >= 1:
        mid = (t // (2 * m)) * (2 * m) + m
        right = t >= mid
        if m < SUBLANES:
            u = t[None, :]
            g = np.where(right[:, None], (u >= mid[:, None]) & (u <= t[:, None]),
                         (u > t[:, None]) & (u < mid[:, None]))
            blocks.append(g.astype(np.float32))
        same = (t[:, None] // (2 * m)) == (t[None, :] // (2 * m))
        masks.append((same & right[:, None] & (~right)[None, :]).astype(np.float32))
        m //= 2
    masks.append(np.eye(c, dtype=np.float32))
    flip = lambda a: a[::-1, ::-1]
    ones = np.ones((16, c), np.float32)
    w = np.stack([np.concatenate(blocks + [ones], 0), np.concatenate([flip(b) for b in blocks] + [ones], 0)])
    cm = np.stack([np.stack(masks), np.stack([flip(a) for a in masks])])
    return w, cm


def _hgrn_kernel(qf_ref, ff_ref, vf_ref, qb_ref, fb_ref, vb_ref, lb_ref, wc_ref, cm_ref, s0_ref,
                 of_ref, ob_ref, st_ref, st_scr, *, c, levels, nchunk):
    i = pl.program_id(1)

    @pl.when(i == 0)
    def _():
        st_scr[...] = s0_ref[...]

    def prep(d, q_ref, f_ref, v_ref):
        q = q_ref[...]
        q = q * jax.nn.sigmoid(q)
        lb = lb_ref[d:d + 1, :]
        f = lb + (1.0 - lb) * jax.nn.sigmoid(f_ref[...])
        kk = 1.0 - f
        g = jnp.log(f)
        g1 = g.astype(BF16)
        g2 = (g - g1.astype(F32)).astype(BF16)
        w = wc_ref[d]
        x = _dot(w, g1) + _dot(w, g2)
        cum = x[0:c]
        n_small = (w.shape[0] - 16) // c - 1
        tot = x[(1 + n_small) * c:(1 + n_small) * c + 1]
        e_lev = []
        m, small = c // 2, 0
        while m >= 1:
            if m >= SUBLANES:
                xr = cum.reshape(c // (2 * m), 2 * m, HGRN_WIDTH)
                ref = xr[:, m - 1 + d:m + d, :]
                e_lev.append(jnp.exp(-jnp.abs(xr - ref)).reshape(c, HGRN_WIDTH))
            else:
                e_lev.append(jnp.exp(x[(1 + small) * c:(2 + small) * c]))
                small += 1
            m //= 2
        return q, kk, v_ref[...].astype(BF16), jnp.exp(cum), jnp.exp(tot - cum), jnp.exp(tot), e_lev

    def head(d, h, q, kk, v, e_in, e_out, e_tot, e_lev, o_ref):
        sl = slice(h * LANES, (h + 1) * LANES)
        st = st_scr[d, h]
        qh, kh, vh = q[:, sl], kk[:, sl], v[:, sl]
        o = _dot_nt((qh * e_in[:, sl]).astype(BF16), st.astype(BF16))
        sc = cm_ref[d, levels] * _dot_nt(qh.astype(BF16), kh.astype(BF16))
        for l in range(levels):
            el = e_lev[l][:, sl]
            sc = sc + cm_ref[d, l] * _dot_nt((qh * el).astype(BF16), (kh * el).astype(BF16))
        o_ref[:, sl] = o + _dot(sc.astype(BF16), vh)
        st_scr[d, h] = st * e_tot[:, sl] + _dot_tn(vh, (kh * e_out[:, sl]).astype(BF16))

    fwd = prep(0, qf_ref, ff_ref, vf_ref)
    bwd = prep(1, qb_ref, fb_ref, vb_ref)
    for h in range(HGRN_HEADS):
        head(0, h, *fwd, of_ref)
        head(1, h, *bwd, ob_ref)

    @pl.when(i == nchunk - 1)
    def _():
        st_ref[...] = st_scr[...]


def _hgrn(z, seq, batch, lb, s0):
    m = z.shape[0]
    c = min(HGRN_CHUNK, seq)
    nchunk = seq // c
    levels = int(math.log2(c))
    wnp, cmnp = _hgrn_consts(c)
    wc = jnp.asarray(wnp, BF16)
    cm = jnp.asarray(cmnp, F32)
    w = HGRN_WIDTH

    def fidx(b, i):
        return b * nchunk + i

    def bidx(b, i):
        return b * nchunk + nchunk - 1 - i

    blk = lambda idx, col: pl.BlockSpec((c, w), lambda b, i: (idx(b, i), col))
    full = lambda shape: pl.BlockSpec(shape, lambda b, i: (0,) * len(shape))
    st_spec = pl.BlockSpec((None, 2, HGRN_HEADS, LANES, LANES), lambda b, i: (b, 0, 0, 0, 0))
    kern = functools.partial(_hgrn_kernel, c=c, levels=levels, nchunk=nchunk)
    return pl.pallas_call(
        kern,
        grid=(batch, nchunk),
        in_specs=[blk(fidx, 0), blk(fidx, 1), blk(fidx, 3), blk(bidx, 0), blk(bidx, 2), blk(bidx, 3),
                  full(lb.shape), full(wc.shape), full(cm.shape), st_spec],
        out_specs=[pl.BlockSpec((c, w), lambda b, i: (fidx(b, i), 0)),
                   pl.BlockSpec((c, w), lambda b, i: (bidx(b, i), 0)), st_spec],
        out_shape=[jax.ShapeDtypeStruct((m, w), F32), jax.ShapeDtypeStruct((m, w), F32),
                   jax.ShapeDtypeStruct((batch, 2, HGRN_HEADS, LANES, LANES), F32)],
        scratch_shapes=[pltpu.VMEM((2, HGRN_HEADS, LANES, LANES), F32)],
        compiler_params=_cparams(("parallel", "arbitrary")),
        name="hgrn2",
    )(z, z, z, z, z, z, lb, wc, cm, s0)


def _outproj_kernel(x_ref, p0_ref, p1_ref, g_ref, att_ref, nw_ref, w_ref, gt_ref, o_ref, *, mode):
    half = w_ref.shape[0] // 2
    s = p0_ref[...] + p1_ref[...]
    g = g_ref[...]
    if mode == "ab":
        a = s * jax.nn.gelu(g, approximate=True)
    else:
        parts = []
        for h in range(HGRN_HEADS):
            sl = slice(h * LANES, (h + 1) * LANES)
            parts.append(_rms(s[:, sl]) * nw_ref[...])
        a = jnp.concatenate(parts, axis=-1) * (g * jax.nn.sigmoid(g))
    acc = _dot(a.astype(BF16), w_ref[0:half, :]) + _dot(att_ref[...], w_ref[half:2 * half, :])
    o_ref[...] = x_ref[...] + gt_ref[...] * acc


def _outproj(x2d, seq, p0, p1, gsrc, gcol, att, head_norm_w, w_out, mod, mode):
    m, d = x2d.shape
    tm = min(256, seq)
    tpb = seq // tm if mod.shape[0] > 1 else m
    hw = w_out.shape[0] // 2
    kern = functools.partial(_outproj_kernel, mode=mode)
    return pl.pallas_call(
        kern,
        grid=(m // tm,),
        in_specs=[pl.BlockSpec((tm, d), lambda i: (i, 0)),
                  pl.BlockSpec((tm, hw), lambda i: (i, 0)),
                  pl.BlockSpec((tm, hw), lambda i: (i, 0)),
                  pl.BlockSpec((tm, hw), lambda i: (i, gcol)),
                  pl.BlockSpec((tm, hw), lambda i: (i, 0)),
                  pl.BlockSpec((1, LANES), lambda i: (0, 0)),
                  pl.BlockSpec(w_out.shape, lambda i: (0, 0)),
                  pl.BlockSpec((None, None, 1, d), lambda i: (i // tpb, 2, 0, 0))],
        out_specs=pl.BlockSpec((tm, d), lambda i: (i, 0)),
        out_shape=jax.ShapeDtypeStruct((m, d), F32),
        compiler_params=_cparams(("parallel",)),
        name="outproj_" + mode,
    )(x2d, p0, p1, gsrc, att, head_norm_w, w_out, mod)


def _ffn_kernel(x_ref, nw_ref, sh_ref, sc_ref, gt_ref, wg_ref, wu_ref, wd_ref, fw_ref, o_ref, hn_ref, *, final):
    j = pl.program_id(1)

    @pl.when(j == 0)
    def _():
        h = _rms(x_ref[...]) * nw_ref[...]
        hn_ref[...] = (h * (1.0 + sc_ref[...]) + sh_ref[...]).astype(BF16)
        o_ref[...] = jnp.zeros(o_ref.shape, F32)

    hn = hn_ref[...]
    g = _dot(hn, wg_ref[...])
    u = _dot(hn, wu_ref[...])
    a = (g * jax.nn.sigmoid(g) * u).astype(BF16)
    o_ref[...] += _dot(a, wd_ref[...])

    @pl.when(j == pl.num_programs(1) - 1)
    def _():
        y = x_ref[...] + gt_ref[...] * o_ref[...]
        if final:
            y = _rms(y) * fw_ref[...]
        o_ref[...] = y


def _ffn(x2d, seq, norm_w, mod, w_gate, w_up, w_down, final_w, final):
    m, d = x2d.shape
    f = w_gate.shape[1]
    tm = min(1024, seq)
    tf = 512
    tpb = seq // tm if mod.shape[0] > 1 else m
    mspec = lambda k: pl.BlockSpec((None, None, 1, d), lambda i, j: (i // tpb, k, 0, 0))
    kern = functools.partial(_ffn_kernel, final=final)
    return pl.pallas_call(
        kern,
        grid=(m // tm, f // tf),
        in_specs=[pl.BlockSpec((tm, d), lambda i, j: (i, 0), pipeline_mode=pl.Buffered(1)),
                  pl.BlockSpec((1, d), lambda i, j: (0, 0)),
                  mspec(3), mspec(4), mspec(5),
                  pl.BlockSpec((d, tf), lambda i, j: (0, j)),
                  pl.BlockSpec((d, tf), lambda i, j: (0, j)),
                  pl.BlockSpec((tf, d), lambda i, j: (j, 0)),
                  pl.BlockSpec((1, d), lambda i, j: (0, 0))],
        out_specs=pl.BlockSpec((tm, d), lambda i, j: (i, 0)),
        out_shape=jax.ShapeDtypeStruct((m, d), F32),
        scratch_shapes=[pltpu.VMEM((tm, d), BF16)],
        compiler_params=_cparams(("parallel", "arbitrary")),
        name="ffn",
    )(x2d, norm_w.reshape(1, d), mod, mod, mod, w_gate, w_up, w_down, final_w.reshape(1, d))


def _rope_tables(rows, head_dim):
    n_freq = head_dim // 4
    half = head_dim // 2
    row = jnp.repeat(jnp.arange(rows, dtype=F32), GRID_W)
    col = jnp.tile(jnp.arange(GRID_W, dtype=F32), rows)
    inv = ROPE_THETA ** (-jnp.arange(n_freq, dtype=F32) / n_freq)
    ang = jnp.concatenate([row[:, None] * inv, col[:, None] * inv], axis=-1)
    cos, sin = jnp.cos(ang), jnp.sin(ang)
    reps = LANES // head_dim
    zero = jnp.zeros_like(sin)
    cos_t = jnp.tile(jnp.concatenate([cos, cos], -1), (1, reps))
    if half * 2 == LANES:
        return cos_t, jnp.concatenate([-sin, sin], -1), None
    sin_a = jnp.tile(jnp.concatenate([-sin, zero], -1), (1, reps))
    sin_b = jnp.tile(jnp.concatenate([zero, sin], -1), (1, reps))
    return cos_t, sin_a, sin_b


def _identity_rope(t, head_dim):
    one = jnp.ones((t, LANES), F32)
    zero = jnp.zeros((t, LANES), F32)
    return (one, zero, None) if head_dim == LANES else (one, zero, zero)


def kernel(x, c, ctx, c_ctx, mod_w, mod_b, norm_mix_w, norm_ffn_w, ffn_w_gate, ffn_w_up, ffn_w_down, ab_w_in, ab_w_out, lru_conv_w, lru_conv_b, lru_wa, lru_ba, lru_wx, lru_bx, lru_lambda, diff_lq1, diff_lk1, diff_lq2, diff_lk2, diff_subln_w, cd_w_in, cd_w_out, hgrn_lb_logits, hgrn_norm_w, gqa_q_norm_w, gqa_k_norm_w, final_norm_w):
    batch, seq, d = x.shape
    clen = ctx.shape[1]
    depth = mod_w.shape[0]
    rows = seq // GRID_W

    cc = jnp.zeros((SUBLANES, d), F32).at[0:batch].set(c).at[batch].set(c_ctx)
    mods = _modulation(cc, mod_w, mod_b)
    lb_cum = jnp.cumsum(jax.nn.softmax(hgrn_lb_logits.astype(F32), axis=1), axis=1)

    xl = x.reshape(batch * seq, d)
    xc = ctx.reshape(batch * clen, d)

    for l in range(depth):
        last = l == depth - 1
        m_lat = mods[l, 0:batch].reshape(batch, N_MOD, 1, d)
        m_ctx = mods[l, batch:batch + 1].reshape(1, N_MOD, 1, d)
        streams = ((_normmod(xl, seq, norm_mix_w[l], m_lat), seq), (_normmod(xc, clen, norm_mix_w[l], m_ctx), clen))
        if l % 2 == 0:
            e = l // 2
            lambda_init = 0.8 - 0.6 * math.exp(-0.3 * l)
            w_in = ab_w_in[e]
            qscale = DIFF_HEAD_DIM ** -0.5 * LOG2E
            w_gx = w_in[:, 0:2048].astype(BF16)
            w_qk = jnp.concatenate([w_in[:, 2048:3072] * qscale, w_in[:, 3072:4096]], axis=1).astype(BF16)
            w_v = w_in[:, 4096:5120].astype(BF16)
            ropes = (_rope_tables(rows, DIFF_HEAD_DIM), _identity_rope(clen, DIFF_HEAD_DIM))
            proj = []
            for (hn, t), rp in zip(streams, ropes):
                gx = _inproj(hn, t, w_gx, F32, 1024)
                qk = _inproj(hn, t, w_qk, BF16, 1024, rope=rp, rope_half=DIFF_HEAD_DIM // 2)
                v = _inproj(hn, t, w_v, BF16, 1024)
                proj.append((gx, qk.reshape(batch, t, 2048), v.reshape(batch, t, 1024)))
            (gx_l, qk_l, v_l), (gx_c, qk_c, v_c) = proj
            wg = jnp.concatenate([lru_wa[e], lru_wx[e]], axis=-1).astype(BF16)
            bg = jnp.concatenate([lru_ba[e].reshape(2, LRU_BLOCKS, 1, LRU_BLOCK),
                                  lru_bx[e].reshape(2, LRU_BLOCKS, 1, LRU_BLOCK)], axis=-1)
            cv = -LRU_C * jax.nn.softplus(-lru_lambda[e].astype(F32))
            h0 = jnp.zeros((batch, 2, LRU_WIDTH), F32)
            hf_c, hb_c, h_ctx = _rglru(gx_c, clen, batch, lru_conv_w[e], lru_conv_b[e], wg, bg, cv, h0)
            hf_l, hb_l, _ = _rglru(gx_l, seq, batch, lru_conv_w[e], lru_conv_b[e], wg, bg, cv, h_ctx)
            lam = (jnp.exp(jnp.sum(diff_lq1[e].astype(F32) * diff_lk1[e].astype(F32)))
                   - jnp.exp(jnp.sum(diff_lq2[e].astype(F32) * diff_lk2[e].astype(F32))) + lambda_init)
            lam_vec = jnp.full((1, LANES), lam, F32)
            sw = diff_subln_w[e].reshape(1, LANES)
            d_l = _diff_attn(qk_l, v_l, qk_c, v_c, lam_vec, sw, 1.0 - lambda_init, batch, True)
            w_out = ab_w_out[e].astype(BF16)
            dummy_nw = jnp.ones((1, LANES), F32)
            xl = _outproj(xl, seq, hf_l, hb_l, gx_l, 0, d_l.reshape(batch * seq, 1024), dummy_nw, w_out, m_lat, "ab")
            if not last:
                d_c = _diff_attn(qk_c, v_c, qk_c, v_c, lam_vec, sw, 1.0 - lambda_init, batch, False)
                xc = _outproj(xc, clen, hf_c, hb_c, gx_c, 0, d_c.reshape(batch * clen, 1024), dummy_nw, w_out,
                              m_ctx, "ab")
        else:
            o = l // 2
            lb = lb_cum[:, l] - lb_cum[:, 0]
            w_in = cd_w_in[o]
            w_z = w_in[:, 0:5120].astype(BF16)
            w_qk = w_in[:, 5120:6400].astype(BF16)
            w_v = w_in[:, 6400:6656].astype(BF16)
            qscale = GQA_HEAD_DIM ** -0.5 * LOG2E
            chunk_w = jnp.concatenate([jnp.tile(gqa_q_norm_w[o] * qscale, GQA_HEADS),
                                       jnp.tile(gqa_k_norm_w[o], GQA_KV_HEADS)]).reshape(1, 1280)
            ropes = (_rope_tables(rows, GQA_HEAD_DIM), _identity_rope(clen, GQA_HEAD_DIM))
            proj = []
            for (hn, t), rp in zip(streams, ropes):
                z = _inproj(hn, t, w_z, F32, 1024)
                qk = _inproj(hn, t, w_qk, BF16, 1280, chunk_w=chunk_w, norm_chunks=10,
                             rope=rp[0:2], rope_half=GQA_HEAD_DIM // 2)
                v = _inproj(hn, t, w_v, BF16, 256)
                proj.append((z, qk.reshape(batch, t, 1280), v.reshape(batch, t, 256)))
            (z_l, qk_l, v_l), (z_c, qk_c, v_c) = proj
            s0 = jnp.zeros((batch, 2, HGRN_HEADS, LANES, LANES), F32)
            of_c, ob_c, s_ctx = _hgrn(z_c, clen, batch, lb, s0)
            of_l, ob_l, _ = _hgrn(z_l, seq, batch, lb, s_ctx)
            att_l = _gqa_attn(qk_l, v_l, qk_c, v_c, batch, True)
            w_out = cd_w_out[o].astype(BF16)
            hnw = hgrn_norm_w[o].reshape(1, LANES)
            xl = _outproj(xl, seq, of_l, ob_l, z_l, 4, att_l.reshape(batch * seq, 1024), hnw, w_out, m_lat, "cd")
            if not last:
                att_c = _gqa_attn(qk_c, v_c, qk_c, v_c, batch, False)
                xc = _outproj(xc, clen, of_c, ob_c, z_c, 4, att_c.reshape(batch * clen, 1024), hnw, w_out,
                              m_ctx, "cd")
        wgt, wup, wdn = ffn_w_gate[l].astype(BF16), ffn_w_up[l].astype(BF16), ffn_w_down[l].astype(BF16)
        xl = _ffn(xl, seq, norm_ffn_w[l], m_lat, wgt, wup, wdn, final_norm_w, last)
        if not last:
            xc = _ffn(xc, clen, norm_ffn_w[l], m_ctx, wgt, wup, wdn, final_norm_w, False)

    return xl.reshape(batch, seq, d)
```

```python
import functools
import math

import numpy as np
import jax
import jax.numpy as jnp
from jax import lax
from jax.experimental import pallas as pl
from jax.experimental.pallas import tpu as pltpu

F32 = jnp.float32
BF16 = jnp.bfloat16

GRID_W = 64
NORM_EPS = 1e-6
ROPE_THETA = 10000.0
N_MOD = 6
LRU_WIDTH = 1024
LRU_BLOCKS = 8
LRU_BLOCK = 128
LRU_C = 8.0
DIFF_HEADS = 8
DIFF_HEAD_DIM = 64
HGRN_HEADS = 8
HGRN_WIDTH = 1024
GQA_HEADS = 8
GQA_KV_HEADS = 2
GQA_REP = 4
GQA_HEAD_DIM = 128
LOG2E = 1.4426950408889634

LANES = 128
SUBLANES = 8
VMEM_LIMIT = 56 * 1024 * 1024

HGRN_CHUNK = 128


def _cparams(sem):
    return pltpu.CompilerParams(dimension_semantics=sem, vmem_limit_bytes=VMEM_LIMIT)


def _dot(a, b):
    return jnp.dot(a, b, preferred_element_type=F32)


def _dot_nt(a, b):
    return lax.dot_general(a, b, (((1,), (1,)), ((), ())), preferred_element_type=F32)


def _dot_tn(a, b):
    return lax.dot_general(a, b, (((0,), (0,)), ((), ())), preferred_element_type=F32)


def _rms(x):
    return x * lax.rsqrt(jnp.mean(x * x, axis=-1, keepdims=True) + NORM_EPS)


def _mod_kernel(c_ref, w_ref, b_ref, o_ref):
    c = c_ref[...]
    a = c * jax.nn.sigmoid(c)
    o_ref[...] = jnp.dot(a, w_ref[...], preferred_element_type=F32,
                         precision=lax.Precision.HIGHEST) + b_ref[...]


def _modulation(cc, mod_w, mod_b):
    depth, d, n = mod_w.shape
    tn = 1024
    return pl.pallas_call(
        _mod_kernel,
        grid=(depth, n // tn),
        in_specs=[pl.BlockSpec((SUBLANES, d), lambda l, j: (0, 0)),
                  pl.BlockSpec((None, d, tn), lambda l, j: (l, 0, j)),
                  pl.BlockSpec((None, 1, tn), lambda l, j: (l, 0, j))],
        out_specs=pl.BlockSpec((None, SUBLANES, tn), lambda l, j: (l, 0, j)),
        out_shape=jax.ShapeDtypeStruct((depth, SUBLANES, n), F32),
        compiler_params=_cparams(("parallel", "parallel")),
        name="modulation",
    )(cc, mod_w, mod_b.reshape(depth, 1, n))


def _normmod_kernel(x_ref, nw_ref, sh_ref, sc_ref, o_ref):
    h = _rms(x_ref[...]) * nw_ref[...]
    o_ref[...] = (h * (1.0 + sc_ref[...]) + sh_ref[...]).astype(o_ref.dtype)


def _normmod(x2d, seq, norm_w, mod):
    m, d = x2d.shape
    tm = min(512, seq)
    tpb = seq // tm if mod.shape[0] > 1 else m
    return pl.pallas_call(
        _normmod_kernel,
        grid=(m // tm,),
        in_specs=[pl.BlockSpec((tm, d), lambda i: (i, 0)),
                  pl.BlockSpec((1, d), lambda i: (0, 0)),
                  pl.BlockSpec((None, None, 1, d), lambda i: (i // tpb, 0, 0, 0)),
                  pl.BlockSpec((None, None, 1, d), lambda i: (i // tpb, 1, 0, 0))],
        out_specs=pl.BlockSpec((tm, d), lambda i: (i, 0)),
        out_shape=jax.ShapeDtypeStruct((m, d), BF16),
        compiler_params=_cparams(("parallel",)),
        name="normmod",
    )(x2d, norm_w.reshape(1, d), mod, mod)


def _inproj_kernel(*refs, n_chunks, norm_chunks, rope_half):
    it = iter(refs)
    x_ref, w_ref = next(it), next(it)
    cw_ref = next(it) if norm_chunks else None
    if rope_half:
        cos_ref, sa_ref = next(it), next(it)
        sb_ref = next(it) if rope_half * 2 != LANES else None
    o_ref = next(it)

    acc = _dot(x_ref[...], w_ref[...])
    for c in range(n_chunks):
        sl = slice(c * LANES, (c + 1) * LANES)
        y = acc[:, sl]
        if c < norm_chunks:
            y = _rms(y) * cw_ref[:, sl]
        if rope_half:
            if rope_half * 2 == LANES:
                y = y * cos_ref[...] + pltpu.roll(y, rope_half, 1) * sa_ref[...]
            else:
                y = (y * cos_ref[...] + pltpu.roll(y, LANES - rope_half, 1) * sa_ref[...]
                     + pltpu.roll(y, rope_half, 1) * sb_ref[...])
        o_ref[:, sl] = y.astype(o_ref.dtype)


def _inproj(hn, seq, w, out_dtype, tn, chunk_w=None, norm_chunks=0, rope=None, rope_half=0):
    m, d = hn.shape
    n = w.shape[1]
    tm = min(1024, seq)
    in_specs = [pl.BlockSpec((tm, d), lambda i, j: (i, 0)),
                pl.BlockSpec((d, tn), lambda i, j: (0, j))]
    args = [hn, w]
    if norm_chunks:
        in_specs.append(pl.BlockSpec((1, tn), lambda i, j: (0, j)))
        args.append(chunk_w)
    if rope_half:
        spt = seq // tm
        for t in rope:
            in_specs.append(pl.BlockSpec((tm, LANES), lambda i, j: (i % spt, 0)))
            args.append(t)
    kern = functools.partial(_inproj_kernel, n_chunks=tn // LANES, norm_chunks=norm_chunks, rope_half=rope_half)
    return pl.pallas_call(
        kern,
        grid=(m // tm, n // tn),
        in_specs=in_specs,
        out_specs=pl.BlockSpec((tm, tn), lambda i, j: (i, j)),
        out_shape=jax.ShapeDtypeStruct((m, n), out_dtype),
        compiler_params=_cparams(("parallel", "parallel")),
        name="inproj",
    )(*args)


ACC_ROWS = LANES + 16


def _to_bf16_t(x):
    return x.astype(F32).T.astype(BF16)


def _transpose_values(v_ref, vt_ref, vT_ref, vtT_ref, *, tk):
    for c in range(v_ref.shape[0] // tk):
        vT_ref[c, 0:LANES, :] = _to_bf16_t(v_ref[c * tk:(c + 1) * tk, :])
        vT_ref[c, LANES:ACC_ROWS, :] = jnp.ones((ACC_ROWS - LANES, tk), BF16)
    vtT_ref[0:LANES, :] = _to_bf16_t(vt_ref[...])
    vtT_ref[LANES:ACC_ROWS, :] = jnp.ones((ACC_ROWS - LANES, vt_ref.shape[0]), BF16)


def _flash_tiles(prep_q, finalize, qT_ref, k_ref, vT_ref, kt_ref, vtT_ref, s0_ref, s1_ref, st_ref, m_ref, acc_ref,
                 *, nq, tk, n_main, has_tail):
    def qk(slot, c):
        off = pl.multiple_of(c * tk, tk)
        return _dot(k_ref[pl.ds(off, tk), :], qT_ref[slot])

    def update(s_ref, vT):
        m_prev = m_ref[...]
        m_new = jnp.maximum(m_prev, jnp.max(s_ref[...], axis=0, keepdims=True))
        alpha = jnp.exp2(m_prev - m_new)
        p = jnp.exp2(s_ref[...] - m_new).astype(BF16)
        acc_ref[...] = alpha * acc_ref[...] + _dot(vT, p)
        m_ref[...] = m_new

    def start():
        m_ref[...] = jnp.full(m_ref.shape, -jnp.inf, F32)
        acc_ref[...] = jnp.zeros(acc_ref.shape, F32)

    def result():
        return acc_ref[0:LANES, :] / acc_ref[LANES:LANES + 1, :]

    if not (has_tail and n_main >= 4 and n_main % 2 == 0):
        def simple_tile(i, carry):
            prep_q(i, 0)
            start()
            if has_tail:
                st_ref[...] = _dot(kt_ref[...], qT_ref[0])
                update(st_ref, vtT_ref[...])

            def body(c, carry2):
                s0_ref[...] = qk(0, c)
                update(s0_ref, vT_ref[c])
                return carry2

            lax.fori_loop(0, n_main, body, 0)
            finalize(i, result())
            return carry

        lax.fori_loop(0, nq, simple_tile, 0)
        return

    pairs = (n_main - 4) // 2
    prep_q(0, 0)
    s0_ref[...] = qk(0, 0)

    def tile(i, carry):
        cur = i % 2
        start()

        def pair(c):
            s1_ref[...] = qk(cur, c + 1)
            update(s0_ref, vT_ref[c])
            s0_ref[...] = qk(cur, c + 2)
            update(s1_ref, vT_ref[c + 1])

        def body(p, carry2):
            pair(2 * p)
            return carry2

        if pairs:
            lax.fori_loop(0, pairs, body, 0, unroll=next(u for u in (7, 3, 2, 1) if pairs % u == 0))
        c = n_main - 4
        pair(c)
        s1_ref[...] = qk(cur, c + 3)
        st_ref[...] = _dot(kt_ref[...], qT_ref[cur])
        update(s0_ref, vT_ref[c + 2])
        prep_q(jnp.minimum(i + 1, nq - 1), 1 - cur)
        s0_ref[...] = qk(1 - cur, 0)
        update(s1_ref, vT_ref[c + 3])
        update(st_ref, vtT_ref[...])
        finalize(i, result())
        return carry

    lax.fori_loop(0, nq, tile, 0)


def _flash_scratch(rows, tk, t, tt):
    return [pltpu.VMEM((2, LANES, rows), BF16), pltpu.VMEM((t // tk, ACC_ROWS, tk), BF16),
            pltpu.VMEM((ACC_ROWS, tt), BF16), pltpu.VMEM((tk, rows), F32), pltpu.VMEM((tk, rows), F32),
            pltpu.VMEM((tt, rows), F32), pltpu.VMEM((1, rows), F32), pltpu.VMEM((ACC_ROWS, rows), F32)]


def _query_blocks_per_head(t, tq, want):
    return max(n for n in range(1, want + 1) if (t // tq) % n == 0)


def _diff_attn_kernel(q_ref, k_ref, v_ref, kt_ref, vt_ref, lam_ref, sw_ref, o_ref,
                      qT_ref, vT_ref, vtT_ref, s0_ref, s1_ref, st_ref, m_ref, acc_ref,
                      *, tq, tk, n_main, has_tail, post_scale):
    @pl.when(pl.program_id(2) == 0)
    def _():
        _transpose_values(v_ref, vt_ref, vT_ref, vtT_ref, tk=tk)

    def rows_of(i):
        return pl.ds(pl.multiple_of(i * tq, tq), tq)

    def prep_q(i, slot):
        q = q_ref[rows_of(i), :].astype(F32)
        lane = lax.broadcasted_iota(jnp.int32, q.shape, 1)
        qT_ref[slot, :, 0:tq] = jnp.where(lane < DIFF_HEAD_DIM, q, 0.0).T.astype(BF16)
        qT_ref[slot, :, tq:2 * tq] = jnp.where(lane >= DIFF_HEAD_DIM, q, 0.0).T.astype(BF16)

    def finalize(i, o_t):
        o = o_t.T
        y = o[0:tq, :] - lam_ref[...] * o[tq:2 * tq, :]
        y = _rms(y) * sw_ref[...] * post_scale
        o_ref[rows_of(i), :] = y.astype(o_ref.dtype)

    _flash_tiles(prep_q, finalize, qT_ref, k_ref, vT_ref, kt_ref, vtT_ref, s0_ref, s1_ref, st_ref, m_ref, acc_ref,
                 nq=q_ref.shape[0] // tq, tk=tk, n_main=n_main, has_tail=has_tail)


def _diff_attn(qk, v, qk_tail, v_tail, lam_vec, subln_w, post_scale, batch, has_tail):
    t = qk.shape[1]
    tq = min(512, t)
    tk = min(512, t)
    nqb = _query_blocks_per_head(t, tq, 4)
    tqb = t // nqb
    kern = functools.partial(_diff_attn_kernel, tq=tq, tk=tk, n_main=t // tk, has_tail=has_tail,
                             post_scale=post_scale)
    tt = qk_tail.shape[1]
    return pl.pallas_call(
        kern,
        grid=(batch, DIFF_HEADS, nqb),
        in_specs=[pl.BlockSpec((None, tqb, LANES), lambda b, h, i: (b, i, h)),
                  pl.BlockSpec((None, t, LANES), lambda b, h, i: (b, 0, DIFF_HEADS + h)),
                  pl.BlockSpec((None, t, LANES), lambda b, h, i: (b, 0, h)),
                  pl.BlockSpec((None, tt, LANES), lambda b, h, i: (b, 0, DIFF_HEADS + h)),
                  pl.BlockSpec((None, tt, LANES), lambda b, h, i: (b, 0, h)),
                  pl.BlockSpec((1, LANES), lambda b, h, i: (0, 0)),
                  pl.BlockSpec((1, LANES), lambda b, h, i: (0, 0))],
        out_specs=pl.BlockSpec((None, tqb, LANES), lambda b, h, i: (b, i, h)),
        out_shape=jax.ShapeDtypeStruct((batch, t, DIFF_HEADS * LANES), BF16),
        scratch_shapes=_flash_scratch(2 * tq, tk, t, tt),
        compiler_params=_cparams(("parallel", "parallel", "arbitrary")),
        name="diff_attn",
    )(qk, qk, v, qk_tail, v_tail, lam_vec, subln_w)


def _gqa_kernel(q_ref, k_ref, v_ref, kt_ref, vt_ref, o_ref,
                qT_ref, vT_ref, vtT_ref, s0_ref, s1_ref, st_ref, m_ref, acc_ref, *, tq, tk, n_main, has_tail):
    @pl.when(pl.program_id(2) == 0)
    def _():
        _transpose_values(v_ref, vt_ref, vT_ref, vtT_ref, tk=tk)

    def rows_of(i):
        return pl.ds(pl.multiple_of(i * tq, tq), tq)

    def prep_q(i, slot):
        for r in range(GQA_REP):
            qT_ref[slot, :, r * tq:(r + 1) * tq] = _to_bf16_t(q_ref[rows_of(i), r * LANES:(r + 1) * LANES])

    def finalize(i, o_t):
        o = o_t.T
        for r in range(GQA_REP):
            o_ref[rows_of(i), r * LANES:(r + 1) * LANES] = o[r * tq:(r + 1) * tq, :].astype(o_ref.dtype)

    _flash_tiles(prep_q, finalize, qT_ref, k_ref, vT_ref, kt_ref, vtT_ref, s0_ref, s1_ref, st_ref, m_ref, acc_ref,
                 nq=q_ref.shape[0] // tq, tk=tk, n_main=n_main, has_tail=has_tail)


def _gqa_attn(qk, v, qk_tail, v_tail, batch, has_tail):
    t = qk.shape[1]
    tq = min(256, t)
    tk = min(512, t)
    gw = GQA_REP * LANES
    nqb = _query_blocks_per_head(t, tq, 8)
    tqb = t // nqb
    kern = functools.partial(_gqa_kernel, tq=tq, tk=tk, n_main=t // tk, has_tail=has_tail)
    tt = qk_tail.shape[1]
    return pl.pallas_call(
        kern,
        grid=(batch, GQA_KV_HEADS, nqb),
        in_specs=[pl.BlockSpec((None, tqb, gw), lambda b, g, i: (b, i, g)),
                  pl.BlockSpec((None, t, LANES), lambda b, g, i: (b, 0, GQA_HEADS + g)),
                  pl.BlockSpec((None, t, LANES), lambda b, g, i: (b, 0, g)),
                  pl.BlockSpec((None, tt, LANES), lambda b, g, i: (b, 0, GQA_HEADS + g)),
                  pl.BlockSpec((None, tt, LANES), lambda b, g, i: (b, 0, g))],
        out_specs=pl.BlockSpec((None, tqb, gw), lambda b, g, i: (b, i, g)),
        out_shape=jax.ShapeDtypeStruct((batch, t, GQA_HEADS * LANES), BF16),
        scratch_shapes=_flash_scratch(GQA_REP * tq, tk, t, tt),
        compiler_params=_cparams(("parallel", "parallel", "arbitrary")),
        name="gqa_attn",
    )(qk, qk, v, qk_tail, v_tail)


def _rglru_kernel(xf_ref, xfp_ref, xfn_ref, xb_ref, xbp_ref, xbn_ref, cw_ref, cb_ref, wg_ref, bg_ref, cv_ref,
                  h0_ref, hf_ref, hb_ref, ht_ref, a_scr, b_scr, st_scr, *, tb, nblk):
    i = pl.program_id(1)

    @pl.when(i == 0)
    def _():
        st_scr[...] = h0_ref[...]

    row = lax.broadcasted_iota(jnp.int32, (tb, LRU_WIDTH), 0)

    def gates(d, x_ref, xp_ref, xn_ref, blk):
        x = x_ref[...]
        prev = xp_ref[SUBLANES - 1:SUBLANES, :] * (blk > 0).astype(F32)
        has_next = (blk < nblk - 1).astype(F32)
        nxt0 = xn_ref[0:1, :] * has_next
        nxt1 = xn_ref[1:2, :] * has_next
        xm1 = jnp.where(row == 0, prev, pltpu.roll(x, 1, 0))
        xp1 = jnp.where(row == tb - 1, nxt0, pltpu.roll(x, tb - 1, 0))
        xp2 = jnp.where(row == tb - 2, nxt0, jnp.where(row == tb - 1, nxt1, pltpu.roll(x, tb - 2, 0)))
        y = xm1 * cw_ref[0:1, :] + x * cw_ref[1:2, :] + xp1 * cw_ref[2:3, :] + xp2 * cw_ref[3:4, :] + cb_ref[...]
        yb = y.astype(BF16)
        for c in range(LRU_BLOCKS):
            sl = slice(c * LRU_BLOCK, (c + 1) * LRU_BLOCK)
            z = _dot(yb[:, sl], wg_ref[d, c]) + bg_ref[d, c]
            r = jax.nn.sigmoid(z[:, 0:LRU_BLOCK])
            g = jax.nn.sigmoid(z[:, LRU_BLOCK:2 * LRU_BLOCK])
            log_a = r * cv_ref[d:d + 1, sl]
            a = jnp.exp(log_a)
            a_scr[d, :, sl] = a
            b_scr[d, :, sl] = jnp.sqrt(-jnp.tanh(log_a) * (1.0 + a * a)) * (g * y[:, sl])

    gates(0, xf_ref, xfp_ref, xfn_ref, i)
    gates(1, xb_ref, xbp_ref, xbn_ref, nblk - 1 - i)

    row8 = lax.broadcasted_iota(jnp.int32, (SUBLANES, LRU_WIDTH), 0)
    nt = tb // SUBLANES

    def scan(d, out_ref):
        rev = d == 1

        def body(r, h):
            off = pl.multiple_of((nt - 1 - r if rev else r) * SUBLANES, SUBLANES)
            a8 = a_scr[d, pl.ds(off, SUBLANES), :]
            b8 = b_scr[d, pl.ds(off, SUBLANES), :]
            for s in (1, 2, 4):
                if rev:
                    ok = row8 < SUBLANES - s
                    sh = SUBLANES - s
                else:
                    ok = row8 >= s
                    sh = s
                a_sh = jnp.where(ok, pltpu.roll(a8, sh, 0), 1.0)
                b_sh = jnp.where(ok, pltpu.roll(b8, sh, 0), 0.0)
                b8 = a8 * b_sh + b8
                a8 = a8 * a_sh
            h8 = a8 * h + b8
            out_ref[pl.ds(off, SUBLANES), :] = h8
            return h8[0:1, :] if rev else h8[SUBLANES - 1:SUBLANES, :]

        st_scr[d:d + 1, :] = lax.fori_loop(0, nt, body, st_scr[d:d + 1, :])

    scan(0, hf_ref)
    scan(1, hb_ref)

    @pl.when(i == nblk - 1)
    def _():
        ht_ref[...] = st_scr[...]


def _rglru(gx, seq, batch, conv_w, conv_b, wg, bg, cv, h0):
    m = gx.shape[0]
    tb = min(256, seq)
    nblk = seq // tb
    hb8 = tb // SUBLANES
    last8 = m // SUBLANES - 1
    w = LRU_WIDTH

    def fidx(b, i):
        return b * nblk + i

    def bidx(b, i):
        return b * nblk + nblk - 1 - i

    def specs(idx):
        return [pl.BlockSpec((tb, w), lambda b, i: (idx(b, i), 1)),
                pl.BlockSpec((SUBLANES, w), lambda b, i: (jnp.maximum(idx(b, i) * hb8 - 1, 0), 1)),
                pl.BlockSpec((SUBLANES, w), lambda b, i: (jnp.minimum((idx(b, i) + 1) * hb8, last8), 1))]

    full = lambda shape: pl.BlockSpec(shape, lambda b, i: (0,) * len(shape))
    kern = functools.partial(_rglru_kernel, tb=tb, nblk=nblk)
    return pl.pallas_call(
        kern,
        grid=(batch, nblk),
        in_specs=specs(fidx) + specs(bidx) + [full(conv_w.shape), full((1, w)), full(wg.shape), full(bg.shape),
                                              full(cv.shape), pl.BlockSpec((None, 2, w), lambda b, i: (b, 0, 0))],
        out_specs=[pl.BlockSpec((tb, w), lambda b, i: (fidx(b, i), 0)),
                   pl.BlockSpec((tb, w), lambda b, i: (bidx(b, i), 0)),
                   pl.BlockSpec((None, 2, w), lambda b, i: (b, 0, 0))],
        out_shape=[jax.ShapeDtypeStruct((m, w), F32), jax.ShapeDtypeStruct((m, w), F32),
                   jax.ShapeDtypeStruct((batch, 2, w), F32)],
        scratch_shapes=[pltpu.VMEM((2, tb, w), F32), pltpu.VMEM((2, tb, w), F32), pltpu.VMEM((2, w), F32)],
        compiler_params=_cparams(("parallel", "arbitrary")),
        name="rglru",
    )(gx, gx, gx, gx, gx, gx, conv_w, conv_b.reshape(1, w), wg, bg, cv, h0)


def _hgrn_consts(c):
    t = np.arange(c)
    blocks = [(t[None, :] <= t[:, None]).astype(np.float32)]
    masks = []
    m = c // 2
    while m >= 1:
        mid = (t // (2 * m)) * (2 * m) + m
        right = t >= mid
        if m < SUBLANES:
            u = t[None, :]
            g = np.where(right[:, None], (u >= mid[:, None]) & (u <= t[:, None]),
                         (u > t[:, None]) & (u < mid[:, None]))
            blocks.append(g.astype(np.float32))
        same = (t[:, None] // (2 * m)) == (t[None, :] // (2 * m))
        masks.append((same & right[:, None] & (~right)[None, :]).astype(np.float32))
        m //= 2
    masks.append(np.eye(c, dtype=np.float32))
    flip = lambda a: a[::-1, ::-1]
    ones = np.ones((16, c), np.float32)
    w = np.stack([np.concatenate(blocks + [ones], 0), np.concatenate([flip(b) for b in blocks] + [ones], 0)])
    cm = np.stack([np.stack(masks), np.stack([flip(a) for a in masks])])
    return w, cm


def _hgrn_kernel(qf_ref, ff_ref, vf_ref, qb_ref, fb_ref, vb_ref, lb_ref, wc_ref, cm_ref, s0_ref,
                 of_ref, ob_ref, st_ref, st_scr, *, c, levels, nchunk):
    i = pl.program_id(1)

    @pl.when(i == 0)
    def _():
        st_scr[...] = s0_ref[...]

    def prep(d, q_ref, f_ref, v_ref):
        q = q_ref[...]
        q = q * jax.nn.sigmoid(q)
        lb = lb_ref[d:d + 1, :]
        f = lb + (1.0 - lb) * jax.nn.sigmoid(f_ref[...])
        kk = 1.0 - f
        g = jnp.log(f)
        g1 = g.astype(BF16)
        g2 = (g - g1.astype(F32)).astype(BF16)
        w = wc_ref[d]
        x = _dot(w, g1) + _dot(w, g2)
        cum = x[0:c]
        n_small = (w.shape[0] - 16) // c - 1
        tot = x[(1 + n_small) * c:(1 + n_small) * c + 1]
        e_lev = []
        m, small = c // 2, 0
        while m >= 1:
            if m >= SUBLANES:
                xr = cum.reshape(c // (2 * m), 2 * m, HGRN_WIDTH)
                ref = xr[:, m - 1 + d:m + d, :]
                e_lev.append(jnp.exp(-jnp.abs(xr - ref)).reshape(c, HGRN_WIDTH))
            else:
                e_lev.append(jnp.exp(x[(1 + small) * c:(2 + small) * c]))
                small += 1
            m //= 2
        return q, kk, v_ref[...].astype(BF16), jnp.exp(cum), jnp.exp(tot - cum), jnp.exp(tot), e_lev

    def head(d, h, q, kk, v, e_in, e_out, e_tot, e_lev, o_ref):
        sl = slice(h * LANES, (h + 1) * LANES)
        st = st_scr[d, h]
        qh, kh, vh = q[:, sl], kk[:, sl], v[:, sl]
        o = _dot_nt((qh * e_in[:, sl]).astype(BF16), st.astype(BF16))
        sc = cm_ref[d, levels] * _dot_nt(qh.astype(BF16), kh.astype(BF16))
        for l in range(levels):
            el = e_lev[l][:, sl]
            sc = sc + cm_ref[d, l] * _dot_nt((qh * el).astype(BF16), (kh * el).astype(BF16))
        o_ref[:, sl] = o + _dot(sc.astype(BF16), vh)
        st_scr[d, h] = st * e_tot[:, sl] + _dot_tn(vh, (kh * e_out[:, sl]).astype(BF16))

    fwd = prep(0, qf_ref, ff_ref, vf_ref)
    bwd = prep(1, qb_ref, fb_ref, vb_ref)
    for h in range(HGRN_HEADS):
        head(0, h, *fwd, of_ref)
        head(1, h, *bwd, ob_ref)

    @pl.when(i == nchunk - 1)
    def _():
        st_ref[...] = st_scr[...]


def _hgrn(z, seq, batch, lb, s0):
    m = z.shape[0]
    c = min(HGRN_CHUNK, seq)
    nchunk = seq // c
    levels = int(math.log2(c))
    wnp, cmnp = _hgrn_consts(c)
    wc = jnp.asarray(wnp, BF16)
    cm = jnp.asarray(cmnp, F32)
    w = HGRN_WIDTH

    def fidx(b, i):
        return b * nchunk + i

    def bidx(b, i):
        return b * nchunk + nchunk - 1 - i

    blk = lambda idx, col: pl.BlockSpec((c, w), lambda b, i: (idx(b, i), col))
    full = lambda shape: pl.BlockSpec(shape, lambda b, i: (0,) * len(shape))
    st_spec = pl.BlockSpec((None, 2, HGRN_HEADS, LANES, LANES), lambda b, i: (b, 0, 0, 0, 0))
    kern = functools.partial(_hgrn_kernel, c=c, levels=levels, nchunk=nchunk)
    return pl.pallas_call(
        kern,
        grid=(batch, nchunk),
        in_specs=[blk(fidx, 0), blk(fidx, 1), blk(fidx, 3), blk(bidx, 0), blk(bidx, 2), blk(bidx, 3),
                  full(lb.shape), full(wc.shape), full(cm.shape), st_spec],
        out_specs=[pl.BlockSpec((c, w), lambda b, i: (fidx(b, i), 0)),
                   pl.BlockSpec((c, w), lambda b, i: (bidx(b, i), 0)), st_spec],
        out_shape=[jax.ShapeDtypeStruct((m, w), F32), jax.ShapeDtypeStruct((m, w), F32),
                   jax.ShapeDtypeStruct((batch, 2, HGRN_HEADS, LANES, LANES), F32)],
        scratch_shapes=[pltpu.VMEM((2, HGRN_HEADS, LANES, LANES), F32)],
        compiler_params=_cparams(("parallel", "arbitrary")),
        name="hgrn2",
    )(z, z, z, z, z, z, lb, wc, cm, s0)


def _outproj_kernel(x_ref, p0_ref, p1_ref, g_ref, att_ref, nw_ref, w_ref, gt_ref, o_ref, *, mode):
    half = w_ref.shape[0] // 2
    s = p0_ref[...] + p1_ref[...]
    g = g_ref[...]
    if mode == "ab":
        a = s * jax.nn.gelu(g, approximate=True)
    else:
        parts = []
        for h in range(HGRN_HEADS):
            sl = slice(h * LANES, (h + 1) * LANES)
            parts.append(_rms(s[:, sl]) * nw_ref[...])
        a = jnp.concatenate(parts, axis=-1) * (g * jax.nn.sigmoid(g))
    acc = _dot(a.astype(BF16), w_ref[0:half, :]) + _dot(att_ref[...], w_ref[half:2 * half, :])
    o_ref[...] = x_ref[...] + gt_ref[...] * acc


def _outproj(x2d, seq, p0, p1, gsrc, gcol, att, head_norm_w, w_out, mod, mode):
    m, d = x2d.shape
    tm = min(256, seq)
    tpb = seq // tm if mod.shape[0] > 1 else m
    hw = w_out.shape[0] // 2
    kern = functools.partial(_outproj_kernel, mode=mode)
    return pl.pallas_call(
        kern,
        grid=(m // tm,),
        in_specs=[pl.BlockSpec((tm, d), lambda i: (i, 0)),
                  pl.BlockSpec((tm, hw), lambda i: (i, 0)),
                  pl.BlockSpec((tm, hw), lambda i: (i, 0)),
                  pl.BlockSpec((tm, hw), lambda i: (i, gcol)),
                  pl.BlockSpec((tm, hw), lambda i: (i, 0)),
                  pl.BlockSpec((1, LANES), lambda i: (0, 0)),
                  pl.BlockSpec(w_out.shape, lambda i: (0, 0)),
                  pl.BlockSpec((None, None, 1, d), lambda i: (i // tpb, 2, 0, 0))],
        out_specs=pl.BlockSpec((tm, d), lambda i: (i, 0)),
        out_shape=jax.ShapeDtypeStruct((m, d), F32),
        compiler_params=_cparams(("parallel",)),
        name="outproj_" + mode,
    )(x2d, p0, p1, gsrc, att, head_norm_w, w_out, mod)


def _ffn_kernel(x_ref, nw_ref, sh_ref, sc_ref, gt_ref, wg_ref, wu_ref, wd_ref, fw_ref, o_ref, hn_ref, *, final):
    j = pl.program_id(1)

    @pl.when(j == 0)
    def _():
        h = _rms(x_ref[...]) * nw_ref[...]
        hn_ref[...] = (h * (1.0 + sc_ref[...]) + sh_ref[...]).astype(BF16)
        o_ref[...] = jnp.zeros(o_ref.shape, F32)

    hn = hn_ref[...]
    g = _dot(hn, wg_ref[...])
    u = _dot(hn, wu_ref[...])
    a = (g * jax.nn.sigmoid(g) * u).astype(BF16)
    o_ref[...] += _dot(a, wd_ref[...])

    @pl.when(j == pl.num_programs(1) - 1)
    def _():
        y = x_ref[...] + gt_ref[...] * o_ref[...]
        if final:
            y = _rms(y) * fw_ref[...]
        o_ref[...] = y


def _ffn(x2d, seq, norm_w, mod, w_gate, w_up, w_down, final_w, final):
    m, d = x2d.shape
    f = w_gate.shape[1]
    tm = min(512, seq)
    tf = 512
    tpb = seq // tm if mod.shape[0] > 1 else m
    mspec = lambda k: pl.BlockSpec((None, None, 1, d), lambda i, j: (i // tpb, k, 0, 0))
    kern = functools.partial(_ffn_kernel, final=final)
    return pl.pallas_call(
        kern,
        grid=(m // tm, f // tf),
        in_specs=[pl.BlockSpec((tm, d), lambda i, j: (i, 0)),
                  pl.BlockSpec((1, d), lambda i, j: (0, 0)),
                  mspec(3), mspec(4), mspec(5),
                  pl.BlockSpec((d, tf), lambda i, j: (0, j)),
                  pl.BlockSpec((d, tf), lambda i, j: (0, j)),
                  pl.BlockSpec((tf, d), lambda i, j: (j, 0)),
                  pl.BlockSpec((1, d), lambda i, j: (0, 0))],
        out_specs=pl.BlockSpec((tm, d), lambda i, j: (i, 0)),
        out_shape=jax.ShapeDtypeStruct((m, d), F32),
        scratch_shapes=[pltpu.VMEM((tm, d), BF16)],
        compiler_params=_cparams(("parallel", "arbitrary")),
        name="ffn",
    )(x2d, norm_w.reshape(1, d), mod, mod, mod, w_gate, w_up, w_down, final_w.reshape(1, d))


def _rope_tables(rows, head_dim):
    n_freq = head_dim // 4
    half = head_dim // 2
    row = jnp.repeat(jnp.arange(rows, dtype=F32), GRID_W)
    col = jnp.tile(jnp.arange(GRID_W, dtype=F32), rows)
    inv = ROPE_THETA ** (-jnp.arange(n_freq, dtype=F32) / n_freq)
    ang = jnp.concatenate([row[:, None] * inv, col[:, None] * inv], axis=-1)
    cos, sin = jnp.cos(ang), jnp.sin(ang)
    reps = LANES // head_dim
    zero = jnp.zeros_like(sin)
    cos_t = jnp.tile(jnp.concatenate([cos, cos], -1), (1, reps))
    if half * 2 == LANES:
        return cos_t, jnp.concatenate([-sin, sin], -1), None
    sin_a = jnp.tile(jnp.concatenate([-sin, zero], -1), (1, reps))
    sin_b = jnp.tile(jnp.concatenate([zero, sin], -1), (1, reps))
    return cos_t, sin_a, sin_b


def _identity_rope(t, head_dim):
    one = jnp.ones((t, LANES), F32)
    zero = jnp.zeros((t, LANES), F32)
    return (one, zero, None) if head_dim == LANES else (one, zero, zero)


def kernel(x, c, ctx, c_ctx, mod_w, mod_b, norm_mix_w, norm_ffn_w, ffn_w_gate, ffn_w_up, ffn_w_down, ab_w_in, ab_w_out, lru_conv_w, lru_conv_b, lru_wa, lru_ba, lru_wx, lru_bx, lru_lambda, diff_lq1, diff_lk1, diff_lq2, diff_lk2, diff_subln_w, cd_w_in, cd_w_out, hgrn_lb_logits, hgrn_norm_w, gqa_q_norm_w, gqa_k_norm_w, final_norm_w):
    batch, seq, d = x.shape
    clen = ctx.shape[1]
    depth = mod_w.shape[0]
    rows = seq // GRID_W

    cc = jnp.zeros((SUBLANES, d), F32).at[0:batch].set(c).at[batch].set(c_ctx)
    mods = _modulation(cc, mod_w, mod_b)
    lb_cum = jnp.cumsum(jax.nn.softmax(hgrn_lb_logits.astype(F32), axis=1), axis=1)

    xl = x.reshape(batch * seq, d)
    xc = ctx.reshape(batch * clen, d)

    for l in range(depth):
        last = l == depth - 1
        m_lat = mods[l, 0:batch].reshape(batch, N_MOD, 1, d)
        m_ctx = mods[l, batch:batch + 1].reshape(1, N_MOD, 1, d)
        streams = ((_normmod(xl, seq, norm_mix_w[l], m_lat), seq), (_normmod(xc, clen, norm_mix_w[l], m_ctx), clen))
        if l % 2 == 0:
            e = l // 2
            lambda_init = 0.8 - 0.6 * math.exp(-0.3 * l)
            w_in = ab_w_in[e]
            qscale = DIFF_HEAD_DIM ** -0.5 * LOG2E
            w_gx = w_in[:, 0:2048].astype(BF16)
            w_qk = jnp.concatenate([w_in[:, 2048:3072] * qscale, w_in[:, 3072:4096]], axis=1).astype(BF16)
            w_v = w_in[:, 4096:5120].astype(BF16)
            ropes = (_rope_tables(rows, DIFF_HEAD_DIM), _identity_rope(clen, DIFF_HEAD_DIM))
            proj = []
            for (hn, t), rp in zip(streams, ropes):
                gx = _inproj(hn, t, w_gx, F32, 1024)
                qk = _inproj(hn, t, w_qk, BF16, 1024, rope=rp, rope_half=DIFF_HEAD_DIM // 2)
                v = _inproj(hn, t, w_v, BF16, 1024)
                proj.append((gx, qk.reshape(batch, t, 2048), v.reshape(batch, t, 1024)))
            (gx_l, qk_l, v_l), (gx_c, qk_c, v_c) = proj
            wg = jnp.concatenate([lru_wa[e], lru_wx[e]], axis=-1).astype(BF16)
            bg = jnp.concatenate([lru_ba[e].reshape(2, LRU_BLOCKS, 1, LRU_BLOCK),
                                  lru_bx[e].reshape(2, LRU_BLOCKS, 1, LRU_BLOCK)], axis=-1)
            cv = -LRU_C * jax.nn.softplus(-lru_lambda[e].astype(F32))
            h0 = jnp.zeros((batch, 2, LRU_WIDTH), F32)
            hf_c, hb_c, h_ctx = _rglru(gx_c, clen, batch, lru_conv_w[e], lru_conv_b[e], wg, bg, cv, h0)
            hf_l, hb_l, _ = _rglru(gx_l, seq, batch, lru_conv_w[e], lru_conv_b[e], wg, bg, cv, h_ctx)
            lam = (jnp.exp(jnp.sum(diff_lq1[e].astype(F32) * diff_lk1[e].astype(F32)))
                   - jnp.exp(jnp.sum(diff_lq2[e].astype(F32) * diff_lk2[e].astype(F32))) + lambda_init)
            lam_vec = jnp.full((1, LANES), lam, F32)
            sw = diff_subln_w[e].reshape(1, LANES)
            d_l = _diff_attn(qk_l, v_l, qk_c, v_c, lam_vec, sw, 1.0 - lambda_init, batch, True)
            w_out = ab_w_out[e].astype(BF16)
            dummy_nw = jnp.ones((1, LANES), F32)
            xl = _outproj(xl, seq, hf_l, hb_l, gx_l, 0, d_l.reshape(batch * seq, 1024), dummy_nw, w_out, m_lat, "ab")
            if not last:
                d_c = _diff_attn(qk_c, v_c, qk_c, v_c, lam_vec, sw, 1.0 - lambda_init, batch, False)
                xc = _outproj(xc, clen, hf_c, hb_c, gx_c, 0, d_c.reshape(batch * clen, 1024), dummy_nw, w_out,
                              m_ctx, "ab")
        else:
            o = l // 2
            lb = lb_cum[:, l] - lb_cum[:, 0]
            w_in = cd_w_in[o]
            w_z = w_in[:, 0:5120].astype(BF16)
            w_qk = w_in[:, 5120:6400].astype(BF16)
            w_v = w_in[:, 6400:6656].astype(BF16)
            qscale = GQA_HEAD_DIM ** -0.5 * LOG2E
            chunk_w = jnp.concatenate([jnp.tile(gqa_q_norm_w[o] * qscale, GQA_HEADS),
                                       jnp.tile(gqa_k_norm_w[o], GQA_KV_HEADS)]).reshape(1, 1280)
            ropes = (_rope_tables(rows, GQA_HEAD_DIM), _identity_rope(clen, GQA_HEAD_DIM))
            proj = []
            for (hn, t), rp in zip(streams, ropes):
                z = _inproj(hn, t, w_z, F32, 1024)
                qk = _inproj(hn, t, w_qk, BF16, 1280, chunk_w=chunk_w, norm_chunks=10,
                             rope=rp[0:2], rope_half=GQA_HEAD_DIM // 2)
                v = _inproj(hn, t, w_v, BF16, 256)
                proj.append((z, qk.reshape(batch, t, 1280), v.reshape(batch, t, 256)))
            (z_l, qk_l, v_l), (z_c, qk_c, v_c) = proj
            s0 = jnp.zeros((batch, 2, HGRN_HEADS, LANES, LANES), F32)
            of_c, ob_c, s_ctx = _hgrn(z_c, clen, batch, lb, s0)
            of_l, ob_l, _ = _hgrn(z_l, seq, batch, lb, s_ctx)
            att_l = _gqa_attn(qk_l, v_l, qk_c, v_c, batch, True)
            w_out = cd_w_out[o].astype(BF16)
            hnw = hgrn_norm_w[o].reshape(1, LANES)
            xl = _outproj(xl, seq, of_l, ob_l, z_l, 4, att_l.reshape(batch * seq, 1024), hnw, w_out, m_lat, "cd")
            if not last:
                att_c = _gqa_attn(qk_c, v_c, qk_c, v_c, batch, False)
                xc = _outproj(xc, clen, of_c, ob_c, z_c, 4, att_c.reshape(batch * clen, 1024), hnw, w_out,
                              m_ctx, "cd")
        wgt, wup, wdn = ffn_w_gate[l].astype(BF16), ffn_w_up[l].astype(BF16), ffn_w_down[l].astype(BF16)
        xl = _ffn(xl, seq, norm_ffn_w[l], m_lat, wgt, wup, wdn, final_norm_w, last)
        if not last:
            xc = _ffn(xc, clen, norm_ffn_w[l], m_ctx, wgt, wup, wdn, final_norm_w, False)

    return xl.reshape(batch, seq, d)
```

```python
import functools
import math

import numpy as np
import jax
import jax.numpy as jnp
from jax import lax
from jax.experimental import pallas as pl
from jax.experimental.pallas import tpu as pltpu

F32 = jnp.float32
BF16 = jnp.bfloat16

GRID_W = 64
NORM_EPS = 1e-6
ROPE_THETA = 10000.0
N_MOD = 6
LRU_WIDTH = 1024
LRU_BLOCKS = 8
LRU_BLOCK = 128
LRU_C = 8.0
DIFF_HEADS = 8
DIFF_HEAD_DIM = 64
HGRN_HEADS = 8
HGRN_WIDTH = 1024
GQA_HEADS = 8
GQA_KV_HEADS = 2
GQA_REP = 4
GQA_HEAD_DIM = 128
LOG2E = 1.4426950408889634

LANES = 128
SUBLANES = 8
BF16_SUBLANES = 16
VMEM_LIMIT = 56 * 1024 * 1024

HGRN_CHUNK = 128


def _cparams(sem):
    return pltpu.CompilerParams(dimension_semantics=sem, vmem_limit_bytes=VMEM_LIMIT)


def _dot(a, b):
    return jnp.dot(a, b, preferred_element_type=F32)


def _dot_nt(a, b):
    return lax.dot_general(a, b, (((1,), (1,)), ((), ())), preferred_element_type=F32)


def _dot_tn(a, b):
    return lax.dot_general(a, b, (((0,), (0,)), ((), ())), preferred_element_type=F32)


def _rms(x):
    return x * lax.rsqrt(jnp.mean(x * x, axis=-1, keepdims=True) + NORM_EPS)


def _mod_kernel(c_ref, w_ref, b_ref, o_ref):
    c = c_ref[...]
    a = c * jax.nn.sigmoid(c)
    o_ref[...] = jnp.dot(a, w_ref[...], preferred_element_type=F32,
                         precision=lax.Precision.HIGHEST) + b_ref[...]


def _modulation(cc, mod_w, mod_b):
    depth, d, n = mod_w.shape
    tn = 1024
    return pl.pallas_call(
        _mod_kernel,
        grid=(depth, n // tn),
        in_specs=[pl.BlockSpec((SUBLANES, d), lambda l, j: (0, 0)),
                  pl.BlockSpec((None, d, tn), lambda l, j: (l, 0, j)),
                  pl.BlockSpec((None, 1, tn), lambda l, j: (l, 0, j))],
        out_specs=pl.BlockSpec((None, SUBLANES, tn), lambda l, j: (l, 0, j)),
        out_shape=jax.ShapeDtypeStruct((depth, SUBLANES, n), F32),
        compiler_params=_cparams(("parallel", "parallel")),
        name="modulation",
    )(cc, mod_w, mod_b.reshape(depth, 1, n))


def _normmod_kernel(x_ref, nw_ref, sh_ref, sc_ref, o_ref):
    h = _rms(x_ref[...]) * nw_ref[...]
    o_ref[...] = (h * (1.0 + sc_ref[...]) + sh_ref[...]).astype(o_ref.dtype)


def _normmod(x2d, seq, norm_w, mod):
    m, d = x2d.shape
    tm = min(512, seq)
    tpb = seq // tm if mod.shape[0] > 1 else m
    return pl.pallas_call(
        _normmod_kernel,
        grid=(m // tm,),
        in_specs=[pl.BlockSpec((tm, d), lambda i: (i, 0)),
                  pl.BlockSpec((1, d), lambda i: (0, 0)),
                  pl.BlockSpec((None, None, 1, d), lambda i: (i // tpb, 0, 0, 0)),
                  pl.BlockSpec((None, None, 1, d), lambda i: (i // tpb, 1, 0, 0))],
        out_specs=pl.BlockSpec((tm, d), lambda i: (i, 0)),
        out_shape=jax.ShapeDtypeStruct((m, d), BF16),
        compiler_params=_cparams(("parallel",)),
        name="normmod",
    )(x2d, norm_w.reshape(1, d), mod, mod)


def _inproj_kernel(*refs, n_chunks, norm_chunks, rope_half):
    it = iter(refs)
    x_ref, w_ref = next(it), next(it)
    cw_ref = next(it) if norm_chunks else None
    if rope_half:
        cos_ref, sa_ref = next(it), next(it)
        sb_ref = next(it) if rope_half * 2 != LANES else None
    o_ref = next(it)

    acc = _dot(x_ref[...], w_ref[...])
    for c in range(n_chunks):
        sl = slice(c * LANES, (c + 1) * LANES)
        y = acc[:, sl]
        if c < norm_chunks:
            y = _rms(y) * cw_ref[:, sl]
        if rope_half:
            if rope_half * 2 == LANES:
                y = y * cos_ref[...] + pltpu.roll(y, rope_half, 1) * sa_ref[...]
            else:
                y = (y * cos_ref[...] + pltpu.roll(y, LANES - rope_half, 1) * sa_ref[...]
                     + pltpu.roll(y, rope_half, 1) * sb_ref[...])
        o_ref[:, sl] = y.astype(o_ref.dtype)


def _inproj(hn, seq, w, out_dtype, tn, chunk_w=None, norm_chunks=0, rope=None, rope_half=0):
    m, d = hn.shape
    n = w.shape[1]
    tm = min(1024, seq)
    in_specs = [pl.BlockSpec((tm, d), lambda i, j: (i, 0)),
                pl.BlockSpec((d, tn), lambda i, j: (0, j))]
    args = [hn, w]
    if norm_chunks:
        in_specs.append(pl.BlockSpec((1, tn), lambda i, j: (0, j)))
        args.append(chunk_w)
    if rope_half:
        spt = seq // tm
        for t in rope:
            in_specs.append(pl.BlockSpec((tm, LANES), lambda i, j: (i % spt, 0)))
            args.append(t)
    kern = functools.partial(_inproj_kernel, n_chunks=tn // LANES, norm_chunks=norm_chunks, rope_half=rope_half)
    return pl.pallas_call(
        kern,
        grid=(m // tm, n // tn),
        in_specs=in_specs,
        out_specs=pl.BlockSpec((tm, tn), lambda i, j: (i, j)),
        out_shape=jax.ShapeDtypeStruct((m, n), out_dtype),
        compiler_params=_cparams(("parallel", "parallel")),
        name="inproj",
    )(*args)


ACC_ROWS = LANES + BF16_SUBLANES


def _to_bf16_t(x):
    return x.astype(F32).T.astype(BF16)


def _transpose_values(v_ref, vt_ref, vT_ref, vtT_ref, *, tk):
    for c in range(v_ref.shape[0] // tk):
        vT_ref[c, 0:LANES, :] = _to_bf16_t(v_ref[c * tk:(c + 1) * tk, :])
        vT_ref[c, LANES:ACC_ROWS, :] = jnp.ones((ACC_ROWS - LANES, tk), BF16)
    vtT_ref[0:LANES, :] = _to_bf16_t(vt_ref[...])
    vtT_ref[LANES:ACC_ROWS, :] = jnp.ones((ACC_ROWS - LANES, vt_ref.shape[0]), BF16)


def _flash_tiles(prep_q, finalize, qT_ref, k_ref, vT_ref, kt_ref, vtT_ref, s0_ref, s1_ref, st_ref, m_ref, acc_ref,
                 *, nq, tk, n_main, has_tail):
    def qk(slot, c):
        off = pl.multiple_of(c * tk, tk)
        return _dot(k_ref[pl.ds(off, tk), :], qT_ref[slot])

    def update(s_ref, vT):
        m_prev = m_ref[...]
        m_new = jnp.maximum(m_prev, jnp.max(s_ref[...], axis=0, keepdims=True))
        alpha = jnp.exp2(m_prev - m_new)
        p = jnp.exp2(s_ref[...] - m_new).astype(BF16)
        acc_ref[...] = alpha * acc_ref[...] + _dot(vT, p)
        m_ref[...] = m_new

    def start():
        m_ref[...] = jnp.full(m_ref.shape, -jnp.inf, F32)
        acc_ref[...] = jnp.zeros(acc_ref.shape, F32)

    def result():
        return acc_ref[0:LANES, :] / acc_ref[LANES:LANES + 1, :]

    if not (has_tail and n_main >= 4 and n_main % 2 == 0):
        def simple_tile(i, carry):
            prep_q(i, 0)
            start()
            if has_tail:
                st_ref[...] = _dot(kt_ref[...], qT_ref[0])
                update(st_ref, vtT_ref[...])

            def body(c, carry2):
                s0_ref[...] = qk(0, c)
                update(s0_ref, vT_ref[c])
                return carry2

            lax.fori_loop(0, n_main, body, 0)
            finalize(i, result())
            return carry

        lax.fori_loop(0, nq, simple_tile, 0)
        return

    pairs = (n_main - 4) // 2
    prep_q(0, 0)
    s0_ref[...] = qk(0, 0)

    def tile(i, carry):
        cur = i % 2
        start()

        def pair(c):
            s1_ref[...] = qk(cur, c + 1)
            update(s0_ref, vT_ref[c])
            s0_ref[...] = qk(cur, c + 2)
            update(s1_ref, vT_ref[c + 1])

        def body(p, carry2):
            pair(2 * p)
            return carry2

        if pairs:
            lax.fori_loop(0, pairs, body, 0, unroll=next(u for u in (7, 3, 2, 1) if pairs % u == 0))
        c = n_main - 4
        pair(c)
        s1_ref[...] = qk(cur, c + 3)
        st_ref[...] = _dot(kt_ref[...], qT_ref[cur])
        update(s0_ref, vT_ref[c + 2])
        prep_q(jnp.minimum(i + 1, nq - 1), 1 - cur)
        s0_ref[...] = qk(1 - cur, 0)
        update(s1_ref, vT_ref[c + 3])
        update(st_ref, vtT_ref[...])
        finalize(i, result())
        return carry

    lax.fori_loop(0, nq, tile, 0)


def _flash_scratch(rows, tk, t, tt):
    return [pltpu.VMEM((2, LANES, rows), BF16), pltpu.VMEM((t // tk, ACC_ROWS, tk), BF16),
            pltpu.VMEM((ACC_ROWS, tt), BF16), pltpu.VMEM((tk, rows), F32), pltpu.VMEM((tk, rows), F32),
            pltpu.VMEM((tt, rows), F32), pltpu.VMEM((1, rows), F32), pltpu.VMEM((ACC_ROWS, rows), F32)]


def _query_blocks_per_head(t, tq, want):
    return max(n for n in range(1, want + 1) if (t // tq) % n == 0)


def _diff_attn_kernel(q_ref, k_ref, v_ref, kt_ref, vt_ref, lam_ref, sw_ref, o_ref,
                      qT_ref, vT_ref, vtT_ref, s0_ref, s1_ref, st_ref, m_ref, acc_ref,
                      *, tq, tk, n_main, has_tail, post_scale):
    @pl.when(pl.program_id(2) == 0)
    def _():
        _transpose_values(v_ref, vt_ref, vT_ref, vtT_ref, tk=tk)

    def rows_of(i):
        return pl.ds(pl.multiple_of(i * tq, tq), tq)

    def prep_q(i, slot):
        q = q_ref[rows_of(i), :].astype(F32)
        lane = lax.broadcasted_iota(jnp.int32, q.shape, 1)
        qT_ref[slot, :, 0:tq] = jnp.where(lane < DIFF_HEAD_DIM, q, 0.0).T.astype(BF16)
        qT_ref[slot, :, tq:2 * tq] = jnp.where(lane >= DIFF_HEAD_DIM, q, 0.0).T.astype(BF16)

    def finalize(i, o_t):
        o = o_t.T
        y = o[0:tq, :] - lam_ref[...] * o[tq:2 * tq, :]
        y = _rms(y) * sw_ref[...] * post_scale
        o_ref[rows_of(i), :] = y.astype(o_ref.dtype)

    _flash_tiles(prep_q, finalize, qT_ref, k_ref, vT_ref, kt_ref, vtT_ref, s0_ref, s1_ref, st_ref, m_ref, acc_ref,
                 nq=q_ref.shape[0] // tq, tk=tk, n_main=n_main, has_tail=has_tail)


def _diff_attn(qk, v, qk_tail, v_tail, lam_vec, subln_w, post_scale, batch, has_tail):
    t = qk.shape[1]
    tq = min(512, t)
    tk = min(512, t)
    nqb = _query_blocks_per_head(t, tq, 4)
    tqb = t // nqb
    kern = functools.partial(_diff_attn_kernel, tq=tq, tk=tk, n_main=t // tk, has_tail=has_tail,
                             post_scale=post_scale)
    tt = qk_tail.shape[1]
    return pl.pallas_call(
        kern,
        grid=(batch, DIFF_HEADS, nqb),
        in_specs=[pl.BlockSpec((None, tqb, LANES), lambda b, h, i: (b, i, h)),
                  pl.BlockSpec((None, t, LANES), lambda b, h, i: (b, 0, DIFF_HEADS + h)),
                  pl.BlockSpec((None, t, LANES), lambda b, h, i: (b, 0, h)),
                  pl.BlockSpec((None, tt, LANES), lambda b, h, i: (b, 0, DIFF_HEADS + h)),
                  pl.BlockSpec((None, tt, LANES), lambda b, h, i: (b, 0, h)),
                  pl.BlockSpec((1, LANES), lambda b, h, i: (0, 0)),
                  pl.BlockSpec((1, LANES), lambda b, h, i: (0, 0))],
        out_specs=pl.BlockSpec((None, tqb, LANES), lambda b, h, i: (b, i, h)),
        out_shape=jax.ShapeDtypeStruct((batch, t, DIFF_HEADS * LANES), BF16),
        scratch_shapes=_flash_scratch(2 * tq, tk, t, tt),
        compiler_params=_cparams(("parallel", "parallel", "arbitrary")),
        name="diff_attn",
    )(qk, qk, v, qk_tail, v_tail, lam_vec, subln_w)


def _gqa_kernel(q_ref, k_ref, v_ref, kt_ref, vt_ref, o_ref,
                qT_ref, vT_ref, vtT_ref, s0_ref, s1_ref, st_ref, m_ref, acc_ref, *, tq, tk, n_main, has_tail):
    @pl.when(pl.program_id(2) == 0)
    def _():
        _transpose_values(v_ref, vt_ref, vT_ref, vtT_ref, tk=tk)

    def rows_of(i):
        return pl.ds(pl.multiple_of(i * tq, tq), tq)

    def prep_q(i, slot):
        for r in range(GQA_REP):
            qT_ref[slot, :, r * tq:(r + 1) * tq] = _to_bf16_t(q_ref[rows_of(i), r * LANES:(r + 1) * LANES])

    def finalize(i, o_t):
        o = o_t.T
        for r in range(GQA_REP):
            o_ref[rows_of(i), r * LANES:(r + 1) * LANES] = o[r * tq:(r + 1) * tq, :].astype(o_ref.dtype)

    _flash_tiles(prep_q, finalize, qT_ref, k_ref, vT_ref, kt_ref, vtT_ref, s0_ref, s1_ref, st_ref, m_ref, acc_ref,
                 nq=q_ref.shape[0] // tq, tk=tk, n_main=n_main, has_tail=has_tail)


def _gqa_attn(qk, v, qk_tail, v_tail, batch, has_tail):
    t = qk.shape[1]
    tq = min(256, t)
    tk = min(512, t)
    gw = GQA_REP * LANES
    nqb = _query_blocks_per_head(t, tq, 8)
    tqb = t // nqb
    kern = functools.partial(_gqa_kernel, tq=tq, tk=tk, n_main=t // tk, has_tail=has_tail)
    tt = qk_tail.shape[1]
    return pl.pallas_call(
        kern,
        grid=(batch, GQA_KV_HEADS, nqb),
        in_specs=[pl.BlockSpec((None, tqb, gw), lambda b, g, i: (b, i, g)),
                  pl.BlockSpec((None, t, LANES), lambda b, g, i: (b, 0, GQA_HEADS + g)),
                  pl.BlockSpec((None, t, LANES), lambda b, g, i: (b, 0, g)),
                  pl.BlockSpec((None, tt, LANES), lambda b, g, i: (b, 0, GQA_HEADS + g)),
                  pl.BlockSpec((None, tt, LANES), lambda b, g, i: (b, 0, g))],
        out_specs=pl.BlockSpec((None, tqb, gw), lambda b, g, i: (b, i, g)),
        out_shape=jax.ShapeDtypeStruct((batch, t, GQA_HEADS * LANES), BF16),
        scratch_shapes=_flash_scratch(GQA_REP * tq, tk, t, tt),
        compiler_params=_cparams(("parallel", "parallel", "arbitrary")),
        name="gqa_attn",
    )(qk, qk, v, qk_tail, v_tail)


def _rglru_kernel(xf_ref, xfp_ref, xfn_ref, xb_ref, xbp_ref, xbn_ref, cw_ref, cb_ref, wg_ref, bg_ref, cv_ref,
                  h0_ref, hf_ref, hb_ref, ht_ref, a_scr, b_scr, st_scr, *, tb, nblk):
    i = pl.program_id(1)

    @pl.when(i == 0)
    def _():
        st_scr[...] = h0_ref[...]

    row = lax.broadcasted_iota(jnp.int32, (tb, LRU_WIDTH), 0)

    def gates(d, x_ref, xp_ref, xn_ref, blk):
        x = x_ref[...]
        prev = xp_ref[SUBLANES - 1:SUBLANES, :] * (blk > 0).astype(F32)
        has_next = (blk < nblk - 1).astype(F32)
        nxt0 = xn_ref[0:1, :] * has_next
        nxt1 = xn_ref[1:2, :] * has_next
        xm1 = jnp.where(row == 0, prev, pltpu.roll(x, 1, 0))
        xp1 = jnp.where(row == tb - 1, nxt0, pltpu.roll(x, tb - 1, 0))
        xp2 = jnp.where(row == tb - 2, nxt0, jnp.where(row == tb - 1, nxt1, pltpu.roll(x, tb - 2, 0)))
        y = xm1 * cw_ref[0:1, :] + x * cw_ref[1:2, :] + xp1 * cw_ref[2:3, :] + xp2 * cw_ref[3:4, :] + cb_ref[...]
        yb = y.astype(BF16)
        for c in range(LRU_BLOCKS):
            sl = slice(c * LRU_BLOCK, (c + 1) * LRU_BLOCK)
            z = _dot(yb[:, sl], wg_ref[d, c]) + bg_ref[d, c]
            r = jax.nn.sigmoid(z[:, 0:LRU_BLOCK])
            g = jax.nn.sigmoid(z[:, LRU_BLOCK:2 * LRU_BLOCK])
            log_a = r * cv_ref[d:d + 1, sl]
            a = jnp.exp(log_a)
            a_scr[d, :, sl] = a
            b_scr[d, :, sl] = jnp.sqrt(-jnp.tanh(log_a) * (1.0 + a * a)) * (g * y[:, sl])

    gates(0, xf_ref, xfp_ref, xfn_ref, i)
    gates(1, xb_ref, xbp_ref, xbn_ref, nblk - 1 - i)

    row8 = lax.broadcasted_iota(jnp.int32, (SUBLANES, LRU_WIDTH), 0)
    nt = tb // SUBLANES

    def scan(d, out_ref):
        rev = d == 1

        def body(r, h):
            off = pl.multiple_of((nt - 1 - r if rev else r) * SUBLANES, SUBLANES)
            a8 = a_scr[d, pl.ds(off, SUBLANES), :]
            b8 = b_scr[d, pl.ds(off, SUBLANES), :]
            for s in (1, 2, 4):
                if rev:
                    ok = row8 < SUBLANES - s
                    sh = SUBLANES - s
                else:
                    ok = row8 >= s
                    sh = s
                a_sh = jnp.where(ok, pltpu.roll(a8, sh, 0), 1.0)
                b_sh = jnp.where(ok, pltpu.roll(b8, sh, 0), 0.0)
                b8 = a8 * b_sh + b8
                a8 = a8 * a_sh
            h8 = a8 * h + b8
            out_ref[pl.ds(off, SUBLANES), :] = h8
            return h8[0:1, :] if rev else h8[SUBLANES - 1:SUBLANES, :]

        st_scr[d:d + 1, :] = lax.fori_loop(0, nt, body, st_scr[d:d + 1, :])

    scan(0, hf_ref)
    scan(1, hb_ref)

    @pl.when(i == nblk - 1)
    def _():
        ht_ref[...] = st_scr[...]


def _rglru(gx, seq, batch, conv_w, conv_b, wg, bg, cv, h0):
    m = gx.shape[0]
    tb = min(256, seq)
    nblk = seq // tb
    hb8 = tb // SUBLANES
    last8 = m // SUBLANES - 1
    w = LRU_WIDTH

    def fidx(b, i):
        return b * nblk + i

    def bidx(b, i):
        return b * nblk + nblk - 1 - i

    def specs(idx):
        return [pl.BlockSpec((tb, w), lambda b, i: (idx(b, i), 1)),
                pl.BlockSpec((SUBLANES, w), lambda b, i: (jnp.maximum(idx(b, i) * hb8 - 1, 0), 1)),
                pl.BlockSpec((SUBLANES, w), lambda b, i: (jnp.minimum((idx(b, i) + 1) * hb8, last8), 1))]

    full = lambda shape: pl.BlockSpec(shape, lambda b, i: (0,) * len(shape))
    kern = functools.partial(_rglru_kernel, tb=tb, nblk=nblk)
    return pl.pallas_call(
        kern,
        grid=(batch, nblk),
        in_specs=specs(fidx) + specs(bidx) + [full(conv_w.shape), full((1, w)), full(wg.shape), full(bg.shape),
                                              full(cv.shape), pl.BlockSpec((None, 2, w), lambda b, i: (b, 0, 0))],
        out_specs=[pl.BlockSpec((tb, w), lambda b, i: (fidx(b, i), 0)),
                   pl.BlockSpec((tb, w), lambda b, i: (bidx(b, i), 0)),
                   pl.BlockSpec((None, 2, w), lambda b, i: (b, 0, 0))],
        out_shape=[jax.ShapeDtypeStruct((m, w), F32), jax.ShapeDtypeStruct((m, w), F32),
                   jax.ShapeDtypeStruct((batch, 2, w), F32)],
        scratch_shapes=[pltpu.VMEM((2, tb, w), F32), pltpu.VMEM((2, tb, w), F32), pltpu.VMEM((2, w), F32)],
        compiler_params=_cparams(("parallel", "arbitrary")),
        name="rglru",
    )(gx, gx, gx, gx, gx, gx, conv_w, conv_b.reshape(1, w), wg, bg, cv, h0)


def _hgrn_consts(c):
    t = np.arange(c)
    blocks = [(t[None, :] <= t[:, None]).astype(np.float32)]
    masks = []
    m = c // 2
    while m >= 1:
        mid = (t // (2 * m)) * (2 * m) + m
        right = t >= mid
        if 2 * m < SUBLANES:
            u = t[None, :]
            g = np.where(right[:, None], (u >= mid[:, None]) & (u <= t[:, None]),
                         (u > t[:, None]) & (u < mid[:, None]))
            blocks.append(g.astype(np.float32))
        same = (t[:, None] // (2 * m)) == (t[None, :] // (2 * m))
        masks.append((same & right[:, None] & (~right)[None, :]).astype(np.float32))
        m //= 2
    masks.append(np.eye(c, dtype=np.float32))
    flip = lambda a: a[::-1, ::-1]
    ones = np.ones((BF16_SUBLANES, c), np.float32)
    w = np.stack([np.concatenate(blocks + [ones], 0), np.concatenate([flip(b) for b in blocks] + [ones], 0)])
    cm = np.stack([np.stack(masks), np.stack([flip(a) for a in masks])])
    return w, cm


def _hgrn_kernel(qf_ref, ff_ref, vf_ref, qb_ref, fb_ref, vb_ref, lb_ref, wc_ref, cm_ref, s0_ref,
                 of_ref, ob_ref, st_ref, st_scr, *, c, levels, nchunk):
    i = pl.program_id(1)

    @pl.when(i == 0)
    def _():
        st_scr[...] = s0_ref[...]

    def prep(d, q_ref, f_ref, v_ref):
        q = q_ref[...]
        q = q * jax.nn.sigmoid(q)
        lb = lb_ref[d:d + 1, :]
        f = lb + (1.0 - lb) * jax.nn.sigmoid(f_ref[...])
        kk = 1.0 - f
        g = jnp.log(f)
        g1 = g.astype(BF16)
        g2 = (g - g1.astype(F32)).astype(BF16)
        w = wc_ref[d]
        x = _dot(w, g1) + _dot(w, g2)
        cum = x[0:c]
        n_small = (w.shape[0] - BF16_SUBLANES) // c - 1
        tot = x[(1 + n_small) * c:(1 + n_small) * c + 1]
        e_lev = []
        m, small = c // 2, 0
        while m >= 1:
            if 2 * m >= SUBLANES:
                xr = cum.reshape(c // (2 * m), 2 * m, HGRN_WIDTH)
                ref = xr[:, m - 1 + d:m + d, :]
                e_lev.append(jnp.exp(-jnp.abs(xr - ref)).reshape(c, HGRN_WIDTH))
            else:
                e_lev.append(jnp.exp(x[(1 + small) * c:(2 + small) * c]))
                small += 1
            m //= 2
        return q, kk, v_ref[...].astype(BF16), jnp.exp(cum), jnp.exp(tot - cum), jnp.exp(tot), e_lev

    def head(d, h, q, kk, v, e_in, e_out, e_tot, e_lev, o_ref):
        sl = slice(h * LANES, (h + 1) * LANES)
        st = st_scr[d, h]
        qh, kh, vh = q[:, sl], kk[:, sl], v[:, sl]
        o = _dot_nt((qh * e_in[:, sl]).astype(BF16), st.astype(BF16))
        sc = cm_ref[d, levels] * _dot_nt(qh.astype(BF16), kh.astype(BF16))
        for l in range(levels):
            el = e_lev[l][:, sl]
            sc = sc + cm_ref[d, l] * _dot_nt((qh * el).astype(BF16), (kh * el).astype(BF16))
        o_ref[:, sl] = o + _dot(sc.astype(BF16), vh)
        st_scr[d, h] = st * e_tot[:, sl] + _dot_tn(vh, (kh * e_out[:, sl]).astype(BF16))

    fwd = prep(0, qf_ref, ff_ref, vf_ref)
    bwd = prep(1, qb_ref, fb_ref, vb_ref)
    for h in range(HGRN_HEADS):
        head(0, h, *fwd, of_ref)
        head(1, h, *bwd, ob_ref)

    @pl.when(i == nchunk - 1)
    def _():
        st_ref[...] = st_scr[...]


def _hgrn(z, seq, batch, lb, s0):
    m = z.shape[0]
    c = min(HGRN_CHUNK, seq)
    nchunk = seq // c
    levels = int(math.log2(c))
    wnp, cmnp = _hgrn_consts(c)
    wc = jnp.asarray(wnp, BF16)
    cm = jnp.asarray(cmnp, F32)
    w = HGRN_WIDTH

    def fidx(b, i):
        return b * nchunk + i

    def bidx(b, i):
        return b * nchunk + nchunk - 1 - i

    blk = lambda idx, col: pl.BlockSpec((c, w), lambda b, i: (idx(b, i), col))
    full = lambda shape: pl.BlockSpec(shape, lambda b, i: (0,) * len(shape))
    st_spec = pl.BlockSpec((None, 2, HGRN_HEADS, LANES, LANES), lambda b, i: (b, 0, 0, 0, 0))
    kern = functools.partial(_hgrn_kernel, c=c, levels=levels, nchunk=nchunk)
    return pl.pallas_call(
        kern,
        grid=(batch, nchunk),
        in_specs=[blk(fidx, 0), blk(fidx, 1), blk(fidx, 3), blk(bidx, 0), blk(bidx, 2), blk(bidx, 3),
                  full(lb.shape), full(wc.shape), full(cm.shape), st_spec],
        out_specs=[pl.BlockSpec((c, w), lambda b, i: (fidx(b, i), 0)),
                   pl.BlockSpec((c, w), lambda b, i: (bidx(b, i), 0)), st_spec],
        out_shape=[jax.ShapeDtypeStruct((m, w), F32), jax.ShapeDtypeStruct((m, w), F32),
                   jax.ShapeDtypeStruct((batch, 2, HGRN_HEADS, LANES, LANES), F32)],
        scratch_shapes=[pltpu.VMEM((2, HGRN_HEADS, LANES, LANES), F32)],
        compiler_params=_cparams(("parallel", "arbitrary")),
        name="hgrn2",
    )(z, z, z, z, z, z, lb, wc, cm, s0)


def _outproj_kernel(x_ref, p0_ref, p1_ref, g_ref, att_ref, nw_ref, w_ref, gt_ref, o_ref, *, mode):
    half = w_ref.shape[0] // 2
    s = p0_ref[...] + p1_ref[...]
    g = g_ref[...]
    if mode == "ab":
        a = s * jax.nn.gelu(g, approximate=True)
    else:
        parts = []
        for h in range(HGRN_HEADS):
            sl = slice(h * LANES, (h + 1) * LANES)
            parts.append(_rms(s[:, sl]) * nw_ref[...])
        a = jnp.concatenate(parts, axis=-1) * (g * jax.nn.sigmoid(g))
    acc = _dot(a.astype(BF16), w_ref[0:half, :]) + _dot(att_ref[...], w_ref[half:2 * half, :])
    o_ref[...] = x_ref[...] + gt_ref[...] * acc


def _outproj(x2d, seq, p0, p1, gsrc, gcol, att, head_norm_w, w_out, mod, mode):
    m, d = x2d.shape
    tm = min(256, seq)
    tpb = seq // tm if mod.shape[0] > 1 else m
    hw = w_out.shape[0] // 2
    kern = functools.partial(_outproj_kernel, mode=mode)
    return pl.pallas_call(
        kern,
        grid=(m // tm,),
        in_specs=[pl.BlockSpec((tm, d), lambda i: (i, 0)),
                  pl.BlockSpec((tm, hw), lambda i: (i, 0)),
                  pl.BlockSpec((tm, hw), lambda i: (i, 0)),
                  pl.BlockSpec((tm, hw), lambda i: (i, gcol)),
                  pl.BlockSpec((tm, hw), lambda i: (i, 0)),
                  pl.BlockSpec((1, LANES), lambda i: (0, 0)),
                  pl.BlockSpec(w_out.shape, lambda i: (0, 0)),
                  pl.BlockSpec((None, None, 1, d), lambda i: (i // tpb, 2, 0, 0))],
        out_specs=pl.BlockSpec((tm, d), lambda i: (i, 0)),
        out_shape=jax.ShapeDtypeStruct((m, d), F32),
        compiler_params=_cparams(("parallel",)),
        name="outproj_" + mode,
    )(x2d, p0, p1, gsrc, att, head_norm_w, w_out, mod)


def _ffn_kernel(x_ref, nw_ref, sh_ref, sc_ref, gt_ref, wg_ref, wu_ref, wd_ref, fw_ref, o_ref, hn_ref, *, final):
    j = pl.program_id(1)

    @pl.when(j == 0)
    def _():
        h = _rms(x_ref[...]) * nw_ref[...]
        hn_ref[...] = (h * (1.0 + sc_ref[...]) + sh_ref[...]).astype(BF16)
        o_ref[...] = jnp.zeros(o_ref.shape, F32)

    hn = hn_ref[...]
    g = _dot(hn, wg_ref[...])
    u = _dot(hn, wu_ref[...])
    a = (g * jax.nn.sigmoid(g) * u).astype(BF16)
    o_ref[...] += _dot(a, wd_ref[...])

    @pl.when(j == pl.num_programs(1) - 1)
    def _():
        y = x_ref[...] + gt_ref[...] * o_ref[...]
        if final:
            y = _rms(y) * fw_ref[...]
        o_ref[...] = y


def _ffn(x2d, seq, norm_w, mod, w_gate, w_up, w_down, final_w, final):
    m, d = x2d.shape
    f = w_gate.shape[1]
    tm = min(512, seq)
    tf = 512
    tpb = seq // tm if mod.shape[0] > 1 else m
    mspec = lambda k: pl.BlockSpec((None, None, 1, d), lambda i, j: (i // tpb, k, 0, 0))
    kern = functools.partial(_ffn_kernel, final=final)
    return pl.pallas_call(
        kern,
        grid=(m // tm, f // tf),
        in_specs=[pl.BlockSpec((tm, d), lambda i, j: (i, 0)),
                  pl.BlockSpec((1, d), lambda i, j: (0, 0)),
                  mspec(3), mspec(4), mspec(5),
                  pl.BlockSpec((d, tf), lambda i, j: (0, j)),
                  pl.BlockSpec((d, tf), lambda i, j: (0, j)),
                  pl.BlockSpec((tf, d), lambda i, j: (j, 0)),
                  pl.BlockSpec((1, d), lambda i, j: (0, 0))],
        out_specs=pl.BlockSpec((tm, d), lambda i, j: (i, 0)),
        out_shape=jax.ShapeDtypeStruct((m, d), F32),
        scratch_shapes=[pltpu.VMEM((tm, d), BF16)],
        compiler_params=_cparams(("parallel", "arbitrary")),
        name="ffn",
    )(x2d, norm_w.reshape(1, d), mod, mod, mod, w_gate, w_up, w_down, final_w.reshape(1, d))


def _rope_tables(rows, head_dim):
    n_freq = head_dim // 4
    half = head_dim // 2
    row = jnp.repeat(jnp.arange(rows, dtype=F32), GRID_W)
    col = jnp.tile(jnp.arange(GRID_W, dtype=F32), rows)
    inv = ROPE_THETA ** (-jnp.arange(n_freq, dtype=F32) / n_freq)
    ang = jnp.concatenate([row[:, None] * inv, col[:, None] * inv], axis=-1)
    cos, sin = jnp.cos(ang), jnp.sin(ang)
    reps = LANES // head_dim
    zero = jnp.zeros_like(sin)
    cos_t = jnp.tile(jnp.concatenate([cos, cos], -1), (1, reps))
    if half * 2 == LANES:
        return cos_t, jnp.concatenate([-sin, sin], -1), None
    sin_a = jnp.tile(jnp.concatenate([-sin, zero], -1), (1, reps))
    sin_b = jnp.tile(jnp.concatenate([zero, sin], -1), (1, reps))
    return cos_t, sin_a, sin_b


def _identity_rope(t, head_dim):
    one = jnp.ones((t, LANES), F32)
    zero = jnp.zeros((t, LANES), F32)
    return (one, zero, None) if head_dim == LANES else (one, zero, zero)


def kernel(x, c, ctx, c_ctx, mod_w, mod_b, norm_mix_w, norm_ffn_w, ffn_w_gate, ffn_w_up, ffn_w_down, ab_w_in, ab_w_out, lru_conv_w, lru_conv_b, lru_wa, lru_ba, lru_wx, lru_bx, lru_lambda, diff_lq1, diff_lk1, diff_lq2, diff_lk2, diff_subln_w, cd_w_in, cd_w_out, hgrn_lb_logits, hgrn_norm_w, gqa_q_norm_w, gqa_k_norm_w, final_norm_w):
    batch, seq, d = x.shape
    clen = ctx.shape[1]
    depth = mod_w.shape[0]
    rows = seq // GRID_W

    cc = jnp.zeros((SUBLANES, d), F32).at[0:batch].set(c).at[batch].set(c_ctx)
    mods = _modulation(cc, mod_w, mod_b)
    lb_cum = jnp.cumsum(jax.nn.softmax(hgrn_lb_logits.astype(F32), axis=1), axis=1)

    xl = x.reshape(batch * seq, d)
    xc = ctx.reshape(batch * clen, d)

    for l in range(depth):
        last = l == depth - 1
        m_lat = mods[l, 0:batch].reshape(batch, N_MOD, 1, d)
        m_ctx = mods[l, batch:batch + 1].reshape(1, N_MOD, 1, d)
        streams = ((_normmod(xl, seq, norm_mix_w[l], m_lat), seq), (_normmod(xc, clen, norm_mix_w[l], m_ctx), clen))
        if l % 2 == 0:
            e = l // 2
            lambda_init = 0.8 - 0.6 * math.exp(-0.3 * l)
            w_in = ab_w_in[e]
            qscale = DIFF_HEAD_DIM ** -0.5 * LOG2E
            w_gx = w_in[:, 0:2048].astype(BF16)
            w_qk = jnp.concatenate([w_in[:, 2048:3072] * qscale, w_in[:, 3072:4096]], axis=1).astype(BF16)
            w_v = w_in[:, 4096:5120].astype(BF16)
            ropes = (_rope_tables(rows, DIFF_HEAD_DIM), _identity_rope(clen, DIFF_HEAD_DIM))
            proj = []
            for (hn, t), rp in zip(streams, ropes):
                gx = _inproj(hn, t, w_gx, F32, 1024)
                qk = _inproj(hn, t, w_qk, BF16, 1024, rope=rp, rope_half=DIFF_HEAD_DIM // 2)
                v = _inproj(hn, t, w_v, BF16, 1024)
                proj.append((gx, qk.reshape(batch, t, 2048), v.reshape(batch, t, 1024)))
            (gx_l, qk_l, v_l), (gx_c, qk_c, v_c) = proj
            wg = jnp.concatenate([lru_wa[e], lru_wx[e]], axis=-1).astype(BF16)
            bg = jnp.concatenate([lru_ba[e].reshape(2, LRU_BLOCKS, 1, LRU_BLOCK),
                                  lru_bx[e].reshape(2, LRU_BLOCKS, 1, LRU_BLOCK)], axis=-1)
            cv = -LRU_C * jax.nn.softplus(-lru_lambda[e].astype(F32))
            h0 = jnp.zeros((batch, 2, LRU_WIDTH), F32)
            hf_c, hb_c, h_ctx = _rglru(gx_c, clen, batch, lru_conv_w[e], lru_conv_b[e], wg, bg, cv, h0)
            hf_l, hb_l, _ = _rglru(gx_l, seq, batch, lru_conv_w[e], lru_conv_b[e], wg, bg, cv, h_ctx)
            lam = (jnp.exp(jnp.sum(diff_lq1[e].astype(F32) * diff_lk1[e].astype(F32)))
                   - jnp.exp(jnp.sum(diff_lq2[e].astype(F32) * diff_lk2[e].astype(F32))) + lambda_init)
            lam_vec = jnp.full((1, LANES), lam, F32)
            sw = diff_subln_w[e].reshape(1, LANES)
            d_l = _diff_attn(qk_l, v_l, qk_c, v_c, lam_vec, sw, 1.0 - lambda_init, batch, True)
            w_out = ab_w_out[e].astype(BF16)
            dummy_nw = jnp.ones((1, LANES), F32)
            xl = _outproj(xl, seq, hf_l, hb_l, gx_l, 0, d_l.reshape(batch * seq, 1024), dummy_nw, w_out, m_lat, "ab")
            if not last:
                d_c = _diff_attn(qk_c, v_c, qk_c, v_c, lam_vec, sw, 1.0 - lambda_init, batch, False)
                xc = _outproj(xc, clen, hf_c, hb_c, gx_c, 0, d_c.reshape(batch * clen, 1024), dummy_nw, w_out,
                              m_ctx, "ab")
        else:
            o = l // 2
            lb = lb_cum[:, l] - lb_cum[:, 0]
            w_in = cd_w_in[o]
            w_z = w_in[:, 0:5120].astype(BF16)
            w_qk = w_in[:, 5120:6400].astype(BF16)
            w_v = w_in[:, 6400:6656].astype(BF16)
            qscale = GQA_HEAD_DIM ** -0.5 * LOG2E
            chunk_w = jnp.concatenate([jnp.tile(gqa_q_norm_w[o] * qscale, GQA_HEADS),
                                       jnp.tile(gqa_k_norm_w[o], GQA_KV_HEADS)]).reshape(1, 1280)
            ropes = (_rope_tables(rows, GQA_HEAD_DIM), _identity_rope(clen, GQA_HEAD_DIM))
            proj = []
            for (hn, t), rp in zip(streams, ropes):
                z = _inproj(hn, t, w_z, F32, 1024)
                qk = _inproj(hn, t, w_qk, BF16, 1280, chunk_w=chunk_w, norm_chunks=10,
                             rope=rp[0:2], rope_half=GQA_HEAD_DIM // 2)
                v = _inproj(hn, t, w_v, BF16, 256)
                proj.append((z, qk.reshape(batch, t, 1280), v.reshape(batch, t, 256)))
            (z_l, qk_l, v_l), (z_c, qk_c, v_c) = proj
            s0 = jnp.zeros((batch, 2, HGRN_HEADS, LANES, LANES), F32)
            of_c, ob_c, s_ctx = _hgrn(z_c, clen, batch, lb, s0)
            of_l, ob_l, _ = _hgrn(z_l, seq, batch, lb, s_ctx)
            att_l = _gqa_attn(qk_l, v_l, qk_c, v_c, batch, True)
            w_out = cd_w_out[o].astype(BF16)
            hnw = hgrn_norm_w[o].reshape(1, LANES)
            xl = _outproj(xl, seq, of_l, ob_l, z_l, 4, att_l.reshape(batch * seq, 1024), hnw, w_out, m_lat, "cd")
            if not last:
                att_c = _gqa_attn(qk_c, v_c, qk_c, v_c, batch, False)
                xc = _outproj(xc, clen, of_c, ob_c, z_c, 4, att_c.reshape(batch * clen, 1024), hnw, w_out,
                              m_ctx, "cd")
        wgt, wup, wdn = ffn_w_gate[l].astype(BF16), ffn_w_up[l].astype(BF16), ffn_w_down[l].astype(BF16)
        xl = _ffn(xl, seq, norm_ffn_w[l], m_lat, wgt, wup, wdn, final_norm_w, last)
        if not last:
            xc = _ffn(xc, clen, norm_ffn_w[l], m_ctx, wgt, wup, wdn, final_norm_w, False)

    return xl.reshape(batch, seq, d)
```

```python
import functools
import math

import numpy as np
import jax
import jax.numpy as jnp
from jax import lax
from jax.experimental import pallas as pl
from jax.experimental.pallas import tpu as pltpu

F32 = jnp.float32
BF16 = jnp.bfloat16

GRID_W = 64
NORM_EPS = 1e-6
ROPE_THETA = 10000.0
N_MOD = 6
LRU_WIDTH = 1024
LRU_BLOCKS = 8
LRU_BLOCK = 128
LRU_C = 8.0
DIFF_HEADS = 8
DIFF_HEAD_DIM = 64
HGRN_HEADS = 8
HGRN_WIDTH = 1024
GQA_HEADS = 8
GQA_KV_HEADS = 2
GQA_REP = 4
GQA_HEAD_DIM = 128
LOG2E = 1.4426950408889634

LANES = 128
SUBLANES = 8
BF16_SUBLANES = 16
VMEM_LIMIT = 56 * 1024 * 1024

HGRN_CHUNK = 128


def _cparams(sem):
    return pltpu.CompilerParams(dimension_semantics=sem, vmem_limit_bytes=VMEM_LIMIT)


def _dot(a, b):
    return jnp.dot(a, b, preferred_element_type=F32)


def _dot_nt(a, b):
    return lax.dot_general(a, b, (((1,), (1,)), ((), ())), preferred_element_type=F32)


def _dot_tn(a, b):
    return lax.dot_general(a, b, (((0,), (0,)), ((), ())), preferred_element_type=F32)


def _rms(x):
    return x * lax.rsqrt(jnp.mean(x * x, axis=-1, keepdims=True) + NORM_EPS)


def _mod_kernel(c_ref, w_ref, b_ref, o_ref):
    c = c_ref[...]
    a = c * jax.nn.sigmoid(c)
    o_ref[...] = jnp.dot(a, w_ref[...], preferred_element_type=F32,
                         precision=lax.Precision.HIGHEST) + b_ref[...]


def _modulation(cc, mod_w, mod_b):
    depth, d, n = mod_w.shape
    tn = 1024
    return pl.pallas_call(
        _mod_kernel,
        grid=(depth, n // tn),
        in_specs=[pl.BlockSpec((SUBLANES, d), lambda l, j: (0, 0)),
                  pl.BlockSpec((None, d, tn), lambda l, j: (l, 0, j)),
                  pl.BlockSpec((None, 1, tn), lambda l, j: (l, 0, j))],
        out_specs=pl.BlockSpec((None, SUBLANES, tn), lambda l, j: (l, 0, j)),
        out_shape=jax.ShapeDtypeStruct((depth, SUBLANES, n), F32),
        compiler_params=_cparams(("parallel", "parallel")),
        name="modulation",
    )(cc, mod_w, mod_b.reshape(depth, 1, n))


def _normmod_kernel(x_ref, nw_ref, sh_ref, sc_ref, o_ref):
    h = _rms(x_ref[...]) * nw_ref[...]
    o_ref[...] = (h * (1.0 + sc_ref[...]) + sh_ref[...]).astype(o_ref.dtype)


def _normmod(x2d, seq, norm_w, mod):
    m, d = x2d.shape
    tm = min(512, seq)
    tpb = seq // tm if mod.shape[0] > 1 else m
    return pl.pallas_call(
        _normmod_kernel,
        grid=(m // tm,),
        in_specs=[pl.BlockSpec((tm, d), lambda i: (i, 0)),
                  pl.BlockSpec((1, d), lambda i: (0, 0)),
                  pl.BlockSpec((None, None, 1, d), lambda i: (i // tpb, 0, 0, 0)),
                  pl.BlockSpec((None, None, 1, d), lambda i: (i // tpb, 1, 0, 0))],
        out_specs=pl.BlockSpec((tm, d), lambda i: (i, 0)),
        out_shape=jax.ShapeDtypeStruct((m, d), BF16),
        compiler_params=_cparams(("parallel",)),
        name="normmod",
    )(x2d, norm_w.reshape(1, d), mod, mod)


def _inproj_kernel(*refs, n_chunks, norm_chunks, rope_half):
    it = iter(refs)
    x_ref, w_ref = next(it), next(it)
    cw_ref = next(it) if norm_chunks else None
    if rope_half:
        cos_ref, sa_ref = next(it), next(it)
        sb_ref = next(it) if rope_half * 2 != LANES else None
    o_ref = next(it)

    acc = _dot(x_ref[...], w_ref[...])
    for c in range(n_chunks):
        sl = slice(c * LANES, (c + 1) * LANES)
        y = acc[:, sl]
        if c < norm_chunks:
            y = _rms(y) * cw_ref[:, sl]
        if rope_half:
            if rope_half * 2 == LANES:
                y = y * cos_ref[...] + pltpu.roll(y, rope_half, 1) * sa_ref[...]
            else:
                y = (y * cos_ref[...] + pltpu.roll(y, LANES - rope_half, 1) * sa_ref[...]
                     + pltpu.roll(y, rope_half, 1) * sb_ref[...])
        o_ref[:, sl] = y.astype(o_ref.dtype)


def _inproj(hn, seq, w, out_dtype, tn, chunk_w=None, norm_chunks=0, rope=None, rope_half=0):
    m, d = hn.shape
    n = w.shape[1]
    tm = min(1024, seq)
    in_specs = [pl.BlockSpec((tm, d), lambda i, j: (i, 0)),
                pl.BlockSpec((d, tn), lambda i, j: (0, j))]
    args = [hn, w]
    if norm_chunks:
        in_specs.append(pl.BlockSpec((1, tn), lambda i, j: (0, j)))
        args.append(chunk_w)
    if rope_half:
        spt = seq // tm
        for t in rope:
            in_specs.append(pl.BlockSpec((tm, LANES), lambda i, j: (i % spt, 0)))
            args.append(t)
    kern = functools.partial(_inproj_kernel, n_chunks=tn // LANES, norm_chunks=norm_chunks, rope_half=rope_half)
    return pl.pallas_call(
        kern,
        grid=(m // tm, n // tn),
        in_specs=in_specs,
        out_specs=pl.BlockSpec((tm, tn), lambda i, j: (i, j)),
        out_shape=jax.ShapeDtypeStruct((m, n), out_dtype),
        compiler_params=_cparams(("parallel", "parallel")),
        name="inproj",
    )(*args)


ACC_ROWS = LANES + BF16_SUBLANES


def _to_bf16_t(x):
    return x.astype(F32).T.astype(BF16)


def _transpose_values(v_ref, vt_ref, vT_ref, vtT_ref, *, tk):
    for c in range(v_ref.shape[0] // tk):
        vT_ref[c, 0:LANES, :] = _to_bf16_t(v_ref[c * tk:(c + 1) * tk, :])
        vT_ref[c, LANES:ACC_ROWS, :] = jnp.ones((ACC_ROWS - LANES, tk), BF16)
    vtT_ref[0:LANES, :] = _to_bf16_t(vt_ref[...])
    vtT_ref[LANES:ACC_ROWS, :] = jnp.ones((ACC_ROWS - LANES, vt_ref.shape[0]), BF16)


def _flash_tiles(prep_q, finalize, qT_ref, k_ref, vT_ref, kt_ref, vtT_ref, s0_ref, s1_ref, st_ref, m_ref, acc_ref,
                 *, nq, tk, n_main, has_tail):
    def qk(slot, c):
        off = pl.multiple_of(c * tk, tk)
        return _dot(k_ref[pl.ds(off, tk), :], qT_ref[slot])

    def update(s_ref, vT):
        m_prev = m_ref[...]
        m_new = jnp.maximum(m_prev, jnp.max(s_ref[...], axis=0, keepdims=True))
        alpha = jnp.exp2(m_prev - m_new)
        p = jnp.exp2(s_ref[...] - m_new).astype(BF16)
        acc_ref[...] = alpha * acc_ref[...] + _dot(vT, p)
        m_ref[...] = m_new

    def start():
        m_ref[...] = jnp.full(m_ref.shape, -jnp.inf, F32)
        acc_ref[...] = jnp.zeros(acc_ref.shape, F32)

    def result():
        return acc_ref[0:LANES, :] / acc_ref[LANES:LANES + 1, :]

    if not (has_tail and n_main >= 4 and n_main % 2 == 0):
        def simple_tile(i, carry):
            prep_q(i, 0)
            start()
            if has_tail:
                st_ref[...] = _dot(kt_ref[...], qT_ref[0])
                update(st_ref, vtT_ref[...])

            def body(c, carry2):
                s0_ref[...] = qk(0, c)
                update(s0_ref, vT_ref[c])
                return carry2

            lax.fori_loop(0, n_main, body, 0)
            finalize(i, result())
            return carry

        lax.fori_loop(0, nq, simple_tile, 0)
        return

    pairs = (n_main - 4) // 2
    prep_q(0, 0)
    s0_ref[...] = qk(0, 0)

    def tile(i, carry):
        cur = i % 2
        start()

        def pair(c):
            s1_ref[...] = qk(cur, c + 1)
            update(s0_ref, vT_ref[c])
            s0_ref[...] = qk(cur, c + 2)
            update(s1_ref, vT_ref[c + 1])

        def body(p, carry2):
            pair(2 * p)
            return carry2

        if pairs:
            lax.fori_loop(0, pairs, body, 0, unroll=next(u for u in (7, 3, 2, 1) if pairs % u == 0))
        c = n_main - 4
        pair(c)
        s1_ref[...] = qk(cur, c + 3)
        st_ref[...] = _dot(kt_ref[...], qT_ref[cur])
        update(s0_ref, vT_ref[c + 2])
        prep_q(jnp.minimum(i + 1, nq - 1), 1 - cur)
        s0_ref[...] = qk(1 - cur, 0)
        update(s1_ref, vT_ref[c + 3])
        update(st_ref, vtT_ref[...])
        finalize(i, result())
        return carry

    lax.fori_loop(0, nq, tile, 0)


def _flash_scratch(rows, tk, t, tt):
    return [pltpu.VMEM((2, LANES, rows), BF16), pltpu.VMEM((t // tk, ACC_ROWS, tk), BF16),
            pltpu.VMEM((ACC_ROWS, tt), BF16), pltpu.VMEM((tk, rows), F32), pltpu.VMEM((tk, rows), F32),
            pltpu.VMEM((tt, rows), F32), pltpu.VMEM((1, rows), F32), pltpu.VMEM((ACC_ROWS, rows), F32)]


def _query_blocks_per_head(t, tq, want):
    return max(n for n in range(1, want + 1) if (t // tq) % n == 0)


def _diff_attn_kernel(q_ref, k_ref, v_ref, kt_ref, vt_ref, lam_ref, sw_ref, o_ref,
                      qT_ref, vT_ref, vtT_ref, s0_ref, s1_ref, st_ref, m_ref, acc_ref,
                      *, tq, tk, n_main, has_tail, post_scale):
    @pl.when(pl.program_id(2) == 0)
    def _():
        _transpose_values(v_ref, vt_ref, vT_ref, vtT_ref, tk=tk)

    def rows_of(i):
        return pl.ds(pl.multiple_of(i * tq, tq), tq)

    def prep_q(i, slot):
        q = q_ref[rows_of(i), :].astype(F32)
        lane = lax.broadcasted_iota(jnp.int32, q.shape, 1)
        qT_ref[slot, :, 0:tq] = jnp.where(lane < DIFF_HEAD_DIM, q, 0.0).T.astype(BF16)
        qT_ref[slot, :, tq:2 * tq] = jnp.where(lane >= DIFF_HEAD_DIM, q, 0.0).T.astype(BF16)

    def finalize(i, o_t):
        o = o_t.T
        y = o[0:tq, :] - lam_ref[...] * o[tq:2 * tq, :]
        y = _rms(y) * sw_ref[...] * post_scale
        o_ref[rows_of(i), :] = y.astype(o_ref.dtype)

    _flash_tiles(prep_q, finalize, qT_ref, k_ref, vT_ref, kt_ref, vtT_ref, s0_ref, s1_ref, st_ref, m_ref, acc_ref,
                 nq=q_ref.shape[0] // tq, tk=tk, n_main=n_main, has_tail=has_tail)


def _diff_attn(qk, v, qk_tail, v_tail, lam_vec, subln_w, post_scale, batch, has_tail):
    t = qk.shape[1]
    tq = min(512, t)
    tk = min(512, t)
    nqb = _query_blocks_per_head(t, tq, 4)
    tqb = t // nqb
    kern = functools.partial(_diff_attn_kernel, tq=tq, tk=tk, n_main=t // tk, has_tail=has_tail,
                             post_scale=post_scale)
    tt = qk_tail.shape[1]
    return pl.pallas_call(
        kern,
        grid=(batch, DIFF_HEADS, nqb),
        in_specs=[pl.BlockSpec((None, tqb, LANES), lambda b, h, i: (b, i, h)),
                  pl.BlockSpec((None, t, LANES), lambda b, h, i: (b, 0, DIFF_HEADS + h)),
                  pl.BlockSpec((None, t, LANES), lambda b, h, i: (b, 0, h)),
                  pl.BlockSpec((None, tt, LANES), lambda b, h, i: (b, 0, DIFF_HEADS + h)),
                  pl.BlockSpec((None, tt, LANES), lambda b, h, i: (b, 0, h)),
                  pl.BlockSpec((1, LANES), lambda b, h, i: (0, 0)),
                  pl.BlockSpec((1, LANES), lambda b, h, i: (0, 0))],
        out_specs=pl.BlockSpec((None, tqb, LANES), lambda b, h, i: (b, i, h)),
        out_shape=jax.ShapeDtypeStruct((batch, t, DIFF_HEADS * LANES), BF16),
        scratch_shapes=_flash_scratch(2 * tq, tk, t, tt),
        compiler_params=_cparams(("parallel", "parallel", "arbitrary")),
        name="diff_attn",
    )(qk, qk, v, qk_tail, v_tail, lam_vec, subln_w)


def _gqa_kernel(q_ref, k_ref, v_ref, kt_ref, vt_ref, o_ref,
                qT_ref, vT_ref, vtT_ref, s0_ref, s1_ref, st_ref, m_ref, acc_ref, *, tq, tk, n_main, has_tail):
    @pl.when(pl.program_id(2) == 0)
    def _():
        _transpose_values(v_ref, vt_ref, vT_ref, vtT_ref, tk=tk)

    def rows_of(i):
        return pl.ds(pl.multiple_of(i * tq, tq), tq)

    def prep_q(i, slot):
        for r in range(GQA_REP):
            qT_ref[slot, :, r * tq:(r + 1) * tq] = _to_bf16_t(q_ref[rows_of(i), r * LANES:(r + 1) * LANES])

    def finalize(i, o_t):
        o = o_t.T
        for r in range(GQA_REP):
            o_ref[rows_of(i), r * LANES:(r + 1) * LANES] = o[r * tq:(r + 1) * tq, :].astype(o_ref.dtype)

    _flash_tiles(prep_q, finalize, qT_ref, k_ref, vT_ref, kt_ref, vtT_ref, s0_ref, s1_ref, st_ref, m_ref, acc_ref,
                 nq=q_ref.shape[0] // tq, tk=tk, n_main=n_main, has_tail=has_tail)


def _gqa_attn(qk, v, qk_tail, v_tail, batch, has_tail):
    t = qk.shape[1]
    tq = min(256, t)
    tk = min(512, t)
    gw = GQA_REP * LANES
    nqb = _query_blocks_per_head(t, tq, 8)
    tqb = t // nqb
    kern = functools.partial(_gqa_kernel, tq=tq, tk=tk, n_main=t // tk, has_tail=has_tail)
    tt = qk_tail.shape[1]
    return pl.pallas_call(
        kern,
        grid=(batch, GQA_KV_HEADS, nqb),
        in_specs=[pl.BlockSpec((None, tqb, gw), lambda b, g, i: (b, i, g)),
                  pl.BlockSpec((None, t, LANES), lambda b, g, i: (b, 0, GQA_HEADS + g)),
                  pl.BlockSpec((None, t, LANES), lambda b, g, i: (b, 0, g)),
                  pl.BlockSpec((None, tt, LANES), lambda b, g, i: (b, 0, GQA_HEADS + g)),
                  pl.BlockSpec((None, tt, LANES), lambda b, g, i: (b, 0, g))],
        out_specs=pl.BlockSpec((None, tqb, gw), lambda b, g, i: (b, i, g)),
        out_shape=jax.ShapeDtypeStruct((batch, t, GQA_HEADS * LANES), BF16),
        scratch_shapes=_flash_scratch(GQA_REP * tq, tk, t, tt),
        compiler_params=_cparams(("parallel", "parallel", "arbitrary")),
        name="gqa_attn",
    )(qk, qk, v, qk_tail, v_tail)


def _rglru_kernel(xf_ref, xfp_ref, xfn_ref, xb_ref, xbp_ref, xbn_ref, cw_ref, cb_ref, wg_ref, bg_ref, cv_ref,
                  h0_ref, hf_ref, hb_ref, ht_ref, a_scr, b_scr, st_scr, *, tb, nblk):
    i = pl.program_id(1)

    @pl.when(i == 0)
    def _():
        st_scr[...] = h0_ref[...]

    row = lax.broadcasted_iota(jnp.int32, (tb, LRU_WIDTH), 0)

    def gates(d, x_ref, xp_ref, xn_ref, blk):
        x = x_ref[...]
        prev = xp_ref[SUBLANES - 1:SUBLANES, :] * (blk > 0).astype(F32)
        has_next = (blk < nblk - 1).astype(F32)
        nxt0 = xn_ref[0:1, :] * has_next
        nxt1 = xn_ref[1:2, :] * has_next
        xm1 = jnp.where(row == 0, prev, pltpu.roll(x, 1, 0))
        xp1 = jnp.where(row == tb - 1, nxt0, pltpu.roll(x, tb - 1, 0))
        xp2 = jnp.where(row == tb - 2, nxt0, jnp.where(row == tb - 1, nxt1, pltpu.roll(x, tb - 2, 0)))
        y = xm1 * cw_ref[0:1, :] + x * cw_ref[1:2, :] + xp1 * cw_ref[2:3, :] + xp2 * cw_ref[3:4, :] + cb_ref[...]
        yb = y.astype(BF16)
        for c in range(LRU_BLOCKS):
            sl = slice(c * LRU_BLOCK, (c + 1) * LRU_BLOCK)
            z = _dot(yb[:, sl], wg_ref[d, c]) + bg_ref[d, c]
            r = jax.nn.sigmoid(z[:, 0:LRU_BLOCK])
            g = jax.nn.sigmoid(z[:, LRU_BLOCK:2 * LRU_BLOCK])
            log_a = r * cv_ref[d:d + 1, sl]
            a = jnp.exp(log_a)
            a_scr[d, :, sl] = a
            b_scr[d, :, sl] = jnp.sqrt(-jnp.tanh(log_a) * (1.0 + a * a)) * (g * y[:, sl])

    gates(0, xf_ref, xfp_ref, xfn_ref, i)
    gates(1, xb_ref, xbp_ref, xbn_ref, nblk - 1 - i)

    row8 = lax.broadcasted_iota(jnp.int32, (SUBLANES, LRU_WIDTH), 0)
    nt = tb // SUBLANES

    def scan(d, out_ref):
        rev = d == 1

        def body(r, h):
            off = pl.multiple_of((nt - 1 - r if rev else r) * SUBLANES, SUBLANES)
            a8 = a_scr[d, pl.ds(off, SUBLANES), :]
            b8 = b_scr[d, pl.ds(off, SUBLANES), :]
            for s in (1, 2, 4):
                if rev:
                    ok = row8 < SUBLANES - s
                    sh = SUBLANES - s
                else:
                    ok = row8 >= s
                    sh = s
                a_sh = jnp.where(ok, pltpu.roll(a8, sh, 0), 1.0)
                b_sh = jnp.where(ok, pltpu.roll(b8, sh, 0), 0.0)
                b8 = a8 * b_sh + b8
                a8 = a8 * a_sh
            h8 = a8 * h + b8
            out_ref[pl.ds(off, SUBLANES), :] = h8
            return h8[0:1, :] if rev else h8[SUBLANES - 1:SUBLANES, :]

        st_scr[d:d + 1, :] = lax.fori_loop(0, nt, body, st_scr[d:d + 1, :])

    scan(0, hf_ref)
    scan(1, hb_ref)

    @pl.when(i == nblk - 1)
    def _():
        ht_ref[...] = st_scr[...]


def _rglru(gx, seq, batch, conv_w, conv_b, wg, bg, cv, h0):
    m = gx.shape[0]
    tb = min(512, seq)
    nblk = seq // tb
    hb8 = tb // SUBLANES
    last8 = m // SUBLANES - 1
    w = LRU_WIDTH

    def fidx(b, i):
        return b * nblk + i

    def bidx(b, i):
        return b * nblk + nblk - 1 - i

    def specs(idx):
        return [pl.BlockSpec((tb, w), lambda b, i: (idx(b, i), 1)),
                pl.BlockSpec((SUBLANES, w), lambda b, i: (jnp.maximum(idx(b, i) * hb8 - 1, 0), 1)),
                pl.BlockSpec((SUBLANES, w), lambda b, i: (jnp.minimum((idx(b, i) + 1) * hb8, last8), 1))]

    full = lambda shape: pl.BlockSpec(shape, lambda b, i: (0,) * len(shape))
    kern = functools.partial(_rglru_kernel, tb=tb, nblk=nblk)
    return pl.pallas_call(
        kern,
        grid=(batch, nblk),
        in_specs=specs(fidx) + specs(bidx) + [full(conv_w.shape), full((1, w)), full(wg.shape), full(bg.shape),
                                              full(cv.shape), pl.BlockSpec((None, 2, w), lambda b, i: (b, 0, 0))],
        out_specs=[pl.BlockSpec((tb, w), lambda b, i: (fidx(b, i), 0)),
                   pl.BlockSpec((tb, w), lambda b, i: (bidx(b, i), 0)),
                   pl.BlockSpec((None, 2, w), lambda b, i: (b, 0, 0))],
        out_shape=[jax.ShapeDtypeStruct((m, w), F32), jax.ShapeDtypeStruct((m, w), F32),
                   jax.ShapeDtypeStruct((batch, 2, w), F32)],
        scratch_shapes=[pltpu.VMEM((2, tb, w), F32), pltpu.VMEM((2, tb, w), F32), pltpu.VMEM((2, w), F32)],
        compiler_params=_cparams(("parallel", "arbitrary")),
        name="rglru",
    )(gx, gx, gx, gx, gx, gx, conv_w, conv_b.reshape(1, w), wg, bg, cv, h0)


def _hgrn_consts(c):
    t = np.arange(c)
    blocks = [(t[None, :] <= t[:, None]).astype(np.float32)]
    masks = []
    m = c // 2
    while m >= 1:
        mid = (t // (2 * m)) * (2 * m) + m
        right = t >= mid
        if 2 * m < SUBLANES:
            u = t[None, :]
            g = np.where(right[:, None], (u >= mid[:, None]) & (u <= t[:, None]),
                         (u > t[:, None]) & (u < mid[:, None]))
            blocks.append(g.astype(np.float32))
        same = (t[:, None] // (2 * m)) == (t[None, :] // (2 * m))
        masks.append((same & right[:, None] & (~right)[None, :]).astype(np.float32))
        m //= 2
    masks.append(np.eye(c, dtype=np.float32))
    flip = lambda a: a[::-1, ::-1]
    ones = np.ones((BF16_SUBLANES, c), np.float32)
    w = np.stack([np.concatenate(blocks + [ones], 0), np.concatenate([flip(b) for b in blocks] + [ones], 0)])
    cm = np.stack([np.stack(masks), np.stack([flip(a) for a in masks])])
    return w, cm


def _hgrn_kernel(qf_ref, ff_ref, vf_ref, qb_ref, fb_ref, vb_ref, lb_ref, wc_ref, cm_ref, s0_ref,
                 of_ref, ob_ref, st_ref, st_scr, *, c, levels, nchunk):
    i = pl.program_id(1)

    @pl.when(i == 0)
    def _():
        st_scr[...] = s0_ref[...]

    def prep(d, q_ref, f_ref, v_ref):
        q = q_ref[...]
        q = q * jax.nn.sigmoid(q)
        lb = lb_ref[d:d + 1, :]
        f = lb + (1.0 - lb) * jax.nn.sigmoid(f_ref[...])
        kk = 1.0 - f
        g = jnp.log(f)
        g1 = g.astype(BF16)
        g2 = (g - g1.astype(F32)).astype(BF16)
        w = wc_ref[d]
        x = _dot(w, g1) + _dot(w, g2)
        cum = x[0:c]
        n_small = (w.shape[0] - BF16_SUBLANES) // c - 1
        tot = x[(1 + n_small) * c:(1 + n_small) * c + 1]
        e_lev = []
        m, small = c // 2, 0
        while m >= 1:
            if 2 * m >= SUBLANES:
                xr = cum.reshape(c // (2 * m), 2 * m, HGRN_WIDTH)
                ref = xr[:, m - 1 + d:m + d, :]
                e_lev.append(jnp.exp(-jnp.abs(xr - ref)).reshape(c, HGRN_WIDTH))
            else:
                e_lev.append(jnp.exp(x[(1 + small) * c:(2 + small) * c]))
                small += 1
            m //= 2
        return q, kk, v_ref[...].astype(BF16), jnp.exp(cum), jnp.exp(tot - cum), jnp.exp(tot), e_lev

    def head(d, h, q, kk, v, e_in, e_out, e_tot, e_lev, o_ref):
        sl = slice(h * LANES, (h + 1) * LANES)
        st = st_scr[d, h]
        qh, kh, vh = q[:, sl], kk[:, sl], v[:, sl]
        o = _dot_nt((qh * e_in[:, sl]).astype(BF16), st.astype(BF16))
        sc = cm_ref[d, levels] * _dot_nt(qh.astype(BF16), kh.astype(BF16))
        for l in range(levels):
            el = e_lev[l][:, sl]
            sc = sc + cm_ref[d, l] * _dot_nt((qh * el).astype(BF16), (kh * el).astype(BF16))
        o_ref[:, sl] = o + _dot(sc.astype(BF16), vh)
        st_scr[d, h] = st * e_tot[:, sl] + _dot_tn(vh, (kh * e_out[:, sl]).astype(BF16))

    fwd = prep(0, qf_ref, ff_ref, vf_ref)
    bwd = prep(1, qb_ref, fb_ref, vb_ref)
    for h in range(HGRN_HEADS):
        head(0, h, *fwd, of_ref)
        head(1, h, *bwd, ob_ref)

    @pl.when(i == nchunk - 1)
    def _():
        st_ref[...] = st_scr[...]


def _hgrn(z, seq, batch, lb, s0):
    m = z.shape[0]
    c = min(HGRN_CHUNK, seq)
    nchunk = seq // c
    levels = int(math.log2(c))
    wnp, cmnp = _hgrn_consts(c)
    wc = jnp.asarray(wnp, BF16)
    cm = jnp.asarray(cmnp, F32)
    w = HGRN_WIDTH

    def fidx(b, i):
        return b * nchunk + i

    def bidx(b, i):
        return b * nchunk + nchunk - 1 - i

    blk = lambda idx, col: pl.BlockSpec((c, w), lambda b, i: (idx(b, i), col))
    full = lambda shape: pl.BlockSpec(shape, lambda b, i: (0,) * len(shape))
    st_spec = pl.BlockSpec((None, 2, HGRN_HEADS, LANES, LANES), lambda b, i: (b, 0, 0, 0, 0))
    kern = functools.partial(_hgrn_kernel, c=c, levels=levels, nchunk=nchunk)
    return pl.pallas_call(
        kern,
        grid=(batch, nchunk),
        in_specs=[blk(fidx, 0), blk(fidx, 1), blk(fidx, 3), blk(bidx, 0), blk(bidx, 2), blk(bidx, 3),
                  full(lb.shape), full(wc.shape), full(cm.shape), st_spec],
        out_specs=[pl.BlockSpec((c, w), lambda b, i: (fidx(b, i), 0)),
                   pl.BlockSpec((c, w), lambda b, i: (bidx(b, i), 0)), st_spec],
        out_shape=[jax.ShapeDtypeStruct((m, w), F32), jax.ShapeDtypeStruct((m, w), F32),
                   jax.ShapeDtypeStruct((batch, 2, HGRN_HEADS, LANES, LANES), F32)],
        scratch_shapes=[pltpu.VMEM((2, HGRN_HEADS, LANES, LANES), F32)],
        compiler_params=_cparams(("parallel", "arbitrary")),
        name="hgrn2",
    )(z, z, z, z, z, z, lb, wc, cm, s0)


def _outproj_kernel(x_ref, p0_ref, p1_ref, g_ref, att_ref, nw_ref, w_ref, gt_ref, o_ref, *, mode):
    half = w_ref.shape[0] // 2
    s = p0_ref[...] + p1_ref[...]
    g = g_ref[...]
    if mode == "ab":
        a = s * jax.nn.gelu(g, approximate=True)
    else:
        parts = []
        for h in range(HGRN_HEADS):
            sl = slice(h * LANES, (h + 1) * LANES)
            parts.append(_rms(s[:, sl]) * nw_ref[...])
        a = jnp.concatenate(parts, axis=-1) * (g * jax.nn.sigmoid(g))
    acc = _dot(a.astype(BF16), w_ref[0:half, :]) + _dot(att_ref[...], w_ref[half:2 * half, :])
    o_ref[...] = x_ref[...] + gt_ref[...] * acc


def _outproj(x2d, seq, p0, p1, gsrc, gcol, att, head_norm_w, w_out, mod, mode):
    m, d = x2d.shape
    tm = min(512, seq)
    tpb = seq // tm if mod.shape[0] > 1 else m
    hw = w_out.shape[0] // 2
    kern = functools.partial(_outproj_kernel, mode=mode)
    return pl.pallas_call(
        kern,
        grid=(m // tm,),
        in_specs=[pl.BlockSpec((tm, d), lambda i: (i, 0)),
                  pl.BlockSpec((tm, hw), lambda i: (i, 0)),
                  pl.BlockSpec((tm, hw), lambda i: (i, 0)),
                  pl.BlockSpec((tm, hw), lambda i: (i, gcol)),
                  pl.BlockSpec((tm, hw), lambda i: (i, 0)),
                  pl.BlockSpec((1, LANES), lambda i: (0, 0)),
                  pl.BlockSpec(w_out.shape, lambda i: (0, 0)),
                  pl.BlockSpec((None, None, 1, d), lambda i: (i // tpb, 2, 0, 0))],
        out_specs=pl.BlockSpec((tm, d), lambda i: (i, 0)),
        out_shape=jax.ShapeDtypeStruct((m, d), F32),
        compiler_params=_cparams(("parallel",)),
        name="outproj_" + mode,
    )(x2d, p0, p1, gsrc, att, head_norm_w, w_out, mod)


def _ffn_kernel(x_ref, nw_ref, sh_ref, sc_ref, gt_ref, wg_ref, wu_ref, wd_ref, fw_ref, o_ref, hn_ref, *, final):
    j = pl.program_id(1)

    @pl.when(j == 0)
    def _():
        h = _rms(x_ref[...]) * nw_ref[...]
        hn_ref[...] = (h * (1.0 + sc_ref[...]) + sh_ref[...]).astype(BF16)
        o_ref[...] = jnp.zeros(o_ref.shape, F32)

    hn = hn_ref[...]
    g = _dot(hn, wg_ref[...])
    u = _dot(hn, wu_ref[...])
    a = (g * jax.nn.sigmoid(g) * u).astype(BF16)
    o_ref[...] += _dot(a, wd_ref[...])

    @pl.when(j == pl.num_programs(1) - 1)
    def _():
        y = x_ref[...] + gt_ref[...] * o_ref[...]
        if final:
            y = _rms(y) * fw_ref[...]
        o_ref[...] = y


def _ffn(x2d, seq, norm_w, mod, w_gate, w_up, w_down, final_w, final):
    m, d = x2d.shape
    f = w_gate.shape[1]
    tm = min(512, seq)
    tf = 512
    tpb = seq // tm if mod.shape[0] > 1 else m
    mspec = lambda k: pl.BlockSpec((None, None, 1, d), lambda i, j: (i // tpb, k, 0, 0))
    kern = functools.partial(_ffn_kernel, final=final)
    return pl.pallas_call(
        kern,
        grid=(m // tm, f // tf),
        in_specs=[pl.BlockSpec((tm, d), lambda i, j: (i, 0)),
                  pl.BlockSpec((1, d), lambda i, j: (0, 0)),
                  mspec(3), mspec(4), mspec(5),
                  pl.BlockSpec((d, tf), lambda i, j: (0, j)),
                  pl.BlockSpec((d, tf), lambda i, j: (0, j)),
                  pl.BlockSpec((tf, d), lambda i, j: (j, 0)),
                  pl.BlockSpec((1, d), lambda i, j: (0, 0))],
        out_specs=pl.BlockSpec((tm, d), lambda i, j: (i, 0)),
        out_shape=jax.ShapeDtypeStruct((m, d), F32),
        scratch_shapes=[pltpu.VMEM((tm, d), BF16)],
        compiler_params=_cparams(("parallel", "arbitrary")),
        name="ffn",
    )(x2d, norm_w.reshape(1, d), mod, mod, mod, w_gate, w_up, w_down, final_w.reshape(1, d))


def _rope_tables(rows, head_dim):
    n_freq = head_dim // 4
    half = head_dim // 2
    row = jnp.repeat(jnp.arange(rows, dtype=F32), GRID_W)
    col = jnp.tile(jnp.arange(GRID_W, dtype=F32), rows)
    inv = ROPE_THETA ** (-jnp.arange(n_freq, dtype=F32) / n_freq)
    ang = jnp.concatenate([row[:, None] * inv, col[:, None] * inv], axis=-1)
    cos, sin = jnp.cos(ang), jnp.sin(ang)
    reps = LANES // head_dim
    zero = jnp.zeros_like(sin)
    cos_t = jnp.tile(jnp.concatenate([cos, cos], -1), (1, reps))
    if half * 2 == LANES:
        return cos_t, jnp.concatenate([-sin, sin], -1), None
    sin_a = jnp.tile(jnp.concatenate([-sin, zero], -1), (1, reps))
    sin_b = jnp.tile(jnp.concatenate([zero, sin], -1), (1, reps))
    return cos_t, sin_a, sin_b


def _identity_rope(t, head_dim):
    one = jnp.ones((t, LANES), F32)
    zero = jnp.zeros((t, LANES), F32)
    return (one, zero, None) if head_dim == LANES else (one, zero, zero)


def kernel(x, c, ctx, c_ctx, mod_w, mod_b, norm_mix_w, norm_ffn_w, ffn_w_gate, ffn_w_up, ffn_w_down, ab_w_in, ab_w_out, lru_conv_w, lru_conv_b, lru_wa, lru_ba, lru_wx, lru_bx, lru_lambda, diff_lq1, diff_lk1, diff_lq2, diff_lk2, diff_subln_w, cd_w_in, cd_w_out, hgrn_lb_logits, hgrn_norm_w, gqa_q_norm_w, gqa_k_norm_w, final_norm_w):
    batch, seq, d = x.shape
    clen = ctx.shape[1]
    depth = mod_w.shape[0]
    rows = seq // GRID_W

    cc = jnp.zeros((SUBLANES, d), F32).at[0:batch].set(c).at[batch].set(c_ctx)
    mods = _modulation(cc, mod_w, mod_b)
    lb_cum = jnp.cumsum(jax.nn.softmax(hgrn_lb_logits.astype(F32), axis=1), axis=1)

    xl = x.reshape(batch * seq, d)
    xc = ctx.reshape(batch * clen, d)

    for l in range(depth):
        last = l == depth - 1
        m_lat = mods[l, 0:batch].reshape(batch, N_MOD, 1, d)
        m_ctx = mods[l, batch:batch + 1].reshape(1, N_MOD, 1, d)
        streams = ((_normmod(xl, seq, norm_mix_w[l], m_lat), seq), (_normmod(xc, clen, norm_mix_w[l], m_ctx), clen))
        if l % 2 == 0:
            e = l // 2
            lambda_init = 0.8 - 0.6 * math.exp(-0.3 * l)
            w_in = ab_w_in[e]
            qscale = DIFF_HEAD_DIM ** -0.5 * LOG2E
            w_gx = w_in[:, 0:2048].astype(BF16)
            w_qk = jnp.concatenate([w_in[:, 2048:3072] * qscale, w_in[:, 3072:4096]], axis=1).astype(BF16)
            w_v = w_in[:, 4096:5120].astype(BF16)
            ropes = (_rope_tables(rows, DIFF_HEAD_DIM), _identity_rope(clen, DIFF_HEAD_DIM))
            proj = []
            for (hn, t), rp in zip(streams, ropes):
                gx = _inproj(hn, t, w_gx, F32, 1024)
                qk = _inproj(hn, t, w_qk, BF16, 1024, rope=rp, rope_half=DIFF_HEAD_DIM // 2)
                v = _inproj(hn, t, w_v, BF16, 1024)
                proj.append((gx, qk.reshape(batch, t, 2048), v.reshape(batch, t, 1024)))
            (gx_l, qk_l, v_l), (gx_c, qk_c, v_c) = proj
            wg = jnp.concatenate([lru_wa[e], lru_wx[e]], axis=-1).astype(BF16)
            bg = jnp.concatenate([lru_ba[e].reshape(2, LRU_BLOCKS, 1, LRU_BLOCK),
                                  lru_bx[e].reshape(2, LRU_BLOCKS, 1, LRU_BLOCK)], axis=-1)
            cv = -LRU_C * jax.nn.softplus(-lru_lambda[e].astype(F32))
            h0 = jnp.zeros((batch, 2, LRU_WIDTH), F32)
            hf_c, hb_c, h_ctx = _rglru(gx_c, clen, batch, lru_conv_w[e], lru_conv_b[e], wg, bg, cv, h0)
            hf_l, hb_l, _ = _rglru(gx_l, seq, batch, lru_conv_w[e], lru_conv_b[e], wg, bg, cv, h_ctx)
            lam = (jnp.exp(jnp.sum(diff_lq1[e].astype(F32) * diff_lk1[e].astype(F32)))
                   - jnp.exp(jnp.sum(diff_lq2[e].astype(F32) * diff_lk2[e].astype(F32))) + lambda_init)
            lam_vec = jnp.full((1, LANES), lam, F32)
            sw = diff_subln_w[e].reshape(1, LANES)
            d_l = _diff_attn(qk_l, v_l, qk_c, v_c, lam_vec, sw, 1.0 - lambda_init, batch, True)
            w_out = ab_w_out[e].astype(BF16)
            dummy_nw = jnp.ones((1, LANES), F32)
            xl = _outproj(xl, seq, hf_l, hb_l, gx_l, 0, d_l.reshape(batch * seq, 1024), dummy_nw, w_out, m_lat, "ab")
            if not last:
                d_c = _diff_attn(qk_c, v_c, qk_c, v_c, lam_vec, sw, 1.0 - lambda_init, batch, False)
                xc = _outproj(xc, clen, hf_c, hb_c, gx_c, 0, d_c.reshape(batch * clen, 1024), dummy_nw, w_out,
                              m_ctx, "ab")
        else:
            o = l // 2
            lb = lb_cum[:, l] - lb_cum[:, 0]
            w_in = cd_w_in[o]
            w_z = w_in[:, 0:5120].astype(BF16)
            w_qk = w_in[:, 5120:6400].astype(BF16)
            w_v = w_in[:, 6400:6656].astype(BF16)
            qscale = GQA_HEAD_DIM ** -0.5 * LOG2E
            chunk_w = jnp.concatenate([jnp.tile(gqa_q_norm_w[o] * qscale, GQA_HEADS),
                                       jnp.tile(gqa_k_norm_w[o], GQA_KV_HEADS)]).reshape(1, 1280)
            ropes = (_rope_tables(rows, GQA_HEAD_DIM), _identity_rope(clen, GQA_HEAD_DIM))
            proj = []
            for (hn, t), rp in zip(streams, ropes):
                z = _inproj(hn, t, w_z, F32, 1024)
                qk = _inproj(hn, t, w_qk, BF16, 1280, chunk_w=chunk_w, norm_chunks=10,
                             rope=rp[0:2], rope_half=GQA_HEAD_DIM // 2)
                v = _inproj(hn, t, w_v, BF16, 256)
                proj.append((z, qk.reshape(batch, t, 1280), v.reshape(batch, t, 256)))
            (z_l, qk_l, v_l), (z_c, qk_c, v_c) = proj
            s0 = jnp.zeros((batch, 2, HGRN_HEADS, LANES, LANES), F32)
            of_c, ob_c, s_ctx = _hgrn(z_c, clen, batch, lb, s0)
            of_l, ob_l, _ = _hgrn(z_l, seq, batch, lb, s_ctx)
            att_l = _gqa_attn(qk_l, v_l, qk_c, v_c, batch, True)
            w_out = cd_w_out[o].astype(BF16)
            hnw = hgrn_norm_w[o].reshape(1, LANES)
            xl = _outproj(xl, seq, of_l, ob_l, z_l, 4, att_l.reshape(batch * seq, 1024), hnw, w_out, m_lat, "cd")
            if not last:
                att_c = _gqa_attn(qk_c, v_c, qk_c, v_c, batch, False)
                xc = _outproj(xc, clen, of_c, ob_c, z_c, 4, att_c.reshape(batch * clen, 1024), hnw, w_out,
                              m_ctx, "cd")
        wgt, wup, wdn = ffn_w_gate[l].astype(BF16), ffn_w_up[l].astype(BF16), ffn_w_down[l].astype(BF16)
        xl = _ffn(xl, seq, norm_ffn_w[l], m_lat, wgt, wup, wdn, final_norm_w, last)
        if not last:
            xc = _ffn(xc, clen, norm_ffn_w[l], m_ctx, wgt, wup, wdn, final_norm_w, False)

    return xl.reshape(batch, seq, d)
```

```python
import functools
import math

import numpy as np
import jax
import jax.numpy as jnp
from jax import lax
from jax.experimental import pallas as pl
from jax.experimental.pallas import tpu as pltpu

F32 = jnp.float32
BF16 = jnp.bfloat16

GRID_W = 64
NORM_EPS = 1e-6
ROPE_THETA = 10000.0
N_MOD = 6
LRU_WIDTH = 1024
LRU_BLOCKS = 8
LRU_BLOCK = 128
LRU_C = 8.0
DIFF_HEADS = 8
DIFF_HEAD_DIM = 64
HGRN_HEADS = 8
HGRN_WIDTH = 1024
GQA_HEADS = 8
GQA_KV_HEADS = 2
GQA_REP = 4
GQA_HEAD_DIM = 128
LOG2E = 1.4426950408889634

LANES = 128
SUBLANES = 8
BF16_SUBLANES = 16
VMEM_LIMIT = 56 * 1024 * 1024

HGRN_CHUNK = 128


def _cparams(sem):
    return pltpu.CompilerParams(dimension_semantics=sem, vmem_limit_bytes=VMEM_LIMIT)


def _dot(a, b):
    return jnp.dot(a, b, preferred_element_type=F32)


def _dot_nt(a, b):
    return lax.dot_general(a, b, (((1,), (1,)), ((), ())), preferred_element_type=F32)


def _dot_tn(a, b):
    return lax.dot_general(a, b, (((0,), (0,)), ((), ())), preferred_element_type=F32)


def _rms(x):
    return x * lax.rsqrt(jnp.mean(x * x, axis=-1, keepdims=True) + NORM_EPS)


def _mod_kernel(c_ref, w_ref, b_ref, o_ref):
    c = c_ref[...]
    a = c * jax.nn.sigmoid(c)
    o_ref[...] = jnp.dot(a, w_ref[...], preferred_element_type=F32,
                         precision=lax.Precision.HIGHEST) + b_ref[...]


def _modulation(cc, mod_w, mod_b):
    depth, d, n = mod_w.shape
    tn = 1024
    return pl.pallas_call(
        _mod_kernel,
        grid=(depth, n // tn),
        in_specs=[pl.BlockSpec((SUBLANES, d), lambda l, j: (0, 0)),
                  pl.BlockSpec((None, d, tn), lambda l, j: (l, 0, j)),
                  pl.BlockSpec((None, 1, tn), lambda l, j: (l, 0, j))],
        out_specs=pl.BlockSpec((None, SUBLANES, tn), lambda l, j: (l, 0, j)),
        out_shape=jax.ShapeDtypeStruct((depth, SUBLANES, n), F32),
        compiler_params=_cparams(("parallel", "parallel")),
        name="modulation",
    )(cc, mod_w, mod_b.reshape(depth, 1, n))


def _normmod_kernel(x_ref, nw_ref, sh_ref, sc_ref, o_ref):
    h = _rms(x_ref[...]) * nw_ref[...]
    o_ref[...] = (h * (1.0 + sc_ref[...]) + sh_ref[...]).astype(o_ref.dtype)


def _normmod(x2d, seq, norm_w, mod):
    m, d = x2d.shape
    tm = min(512, seq)
    tpb = seq // tm if mod.shape[0] > 1 else m
    return pl.pallas_call(
        _normmod_kernel,
        grid=(m // tm,),
        in_specs=[pl.BlockSpec((tm, d), lambda i: (i, 0)),
                  pl.BlockSpec((1, d), lambda i: (0, 0)),
                  pl.BlockSpec((None, None, 1, d), lambda i: (i // tpb, 0, 0, 0)),
                  pl.BlockSpec((None, None, 1, d), lambda i: (i // tpb, 1, 0, 0))],
        out_specs=pl.BlockSpec((tm, d), lambda i: (i, 0)),
        out_shape=jax.ShapeDtypeStruct((m, d), BF16),
        compiler_params=_cparams(("parallel",)),
        name="normmod",
    )(x2d, norm_w.reshape(1, d), mod, mod)


def _inproj_kernel(*refs, n_chunks, norm_chunks, rope_half):
    it = iter(refs)
    x_ref, w_ref = next(it), next(it)
    cw_ref = next(it) if norm_chunks else None
    if rope_half:
        cos_ref, sa_ref = next(it), next(it)
        sb_ref = next(it) if rope_half * 2 != LANES else None
    o_ref = next(it)

    acc = _dot(x_ref[...], w_ref[...])
    for c in range(n_chunks):
        sl = slice(c * LANES, (c + 1) * LANES)
        y = acc[:, sl]
        if c < norm_chunks:
            y = _rms(y) * cw_ref[:, sl]
        if rope_half:
            if rope_half * 2 == LANES:
                y = y * cos_ref[...] + pltpu.roll(y, rope_half, 1) * sa_ref[...]
            else:
                y = (y * cos_ref[...] + pltpu.roll(y, LANES - rope_half, 1) * sa_ref[...]
                     + pltpu.roll(y, rope_half, 1) * sb_ref[...])
        o_ref[:, sl] = y.astype(o_ref.dtype)


def _inproj(hn, seq, w, col0, n, out_dtype, tn, chunk_w=None, norm_chunks=0, rope=None, rope_half=0):
    m, d = hn.shape
    tm = min(1024, seq)
    assert col0 % tn == 0 and n % tn == 0
    jb = col0 // tn
    in_specs = [pl.BlockSpec((tm, d), lambda i, j: (i, 0)),
                pl.BlockSpec((d, tn), lambda i, j: (0, jb + j))]
    args = [hn, w]
    if norm_chunks:
        in_specs.append(pl.BlockSpec((1, tn), lambda i, j: (0, j)))
        args.append(chunk_w)
    if rope_half:
        spt = seq // tm
        for t in rope:
            if t.ndim == 3:
                in_specs.append(pl.BlockSpec((None, tm, LANES), lambda i, j: (j, i % spt, 0)))
            else:
                in_specs.append(pl.BlockSpec((tm, LANES), lambda i, j: (i % spt, 0)))
            args.append(t)
    kern = functools.partial(_inproj_kernel, n_chunks=tn // LANES, norm_chunks=norm_chunks, rope_half=rope_half)
    return pl.pallas_call(
        kern,
        grid=(m // tm, n // tn),
        in_specs=in_specs,
        out_specs=pl.BlockSpec((tm, tn), lambda i, j: (i, j)),
        out_shape=jax.ShapeDtypeStruct((m, n), out_dtype),
        compiler_params=_cparams(("parallel", "parallel")),
        name="inproj",
    )(*args)


ACC_ROWS = LANES + BF16_SUBLANES


def _to_bf16_t(x):
    return x.astype(F32).T.astype(BF16)


def _transpose_values(v_ref, vt_ref, vT_ref, vtT_ref, *, tk):
    for c in range(v_ref.shape[0] // tk):
        vT_ref[c, 0:LANES, :] = _to_bf16_t(v_ref[c * tk:(c + 1) * tk, :])
        vT_ref[c, LANES:ACC_ROWS, :] = jnp.ones((ACC_ROWS - LANES, tk), BF16)
    vtT_ref[0:LANES, :] = _to_bf16_t(vt_ref[...])
    vtT_ref[LANES:ACC_ROWS, :] = jnp.ones((ACC_ROWS - LANES, vt_ref.shape[0]), BF16)


def _flash_tiles(prep_q, finalize, qT_ref, k_ref, vT_ref, kt_ref, vtT_ref, s0_ref, s1_ref, st_ref, m_ref, acc_ref,
                 *, nq, tk, n_main, has_tail):
    def qk(slot, c):
        off = pl.multiple_of(c * tk, tk)
        return _dot(k_ref[pl.ds(off, tk), :], qT_ref[slot])

    def update(s_ref, vT):
        m_prev = m_ref[...]
        m_new = jnp.maximum(m_prev, jnp.max(s_ref[...], axis=0, keepdims=True))
        alpha = jnp.exp2(m_prev - m_new)
        p = jnp.exp2(s_ref[...] - m_new).astype(BF16)
        acc_ref[...] = alpha * acc_ref[...] + _dot(vT, p)
        m_ref[...] = m_new

    def start():
        m_ref[...] = jnp.full(m_ref.shape, -jnp.inf, F32)
        acc_ref[...] = jnp.zeros(acc_ref.shape, F32)

    def result():
        return acc_ref[0:LANES, :] / acc_ref[LANES:LANES + 1, :]

    if not (has_tail and n_main >= 4 and n_main % 2 == 0):
        def simple_tile(i, carry):
            prep_q(i, 0)
            start()
            if has_tail:
                st_ref[...] = _dot(kt_ref[...], qT_ref[0])
                update(st_ref, vtT_ref[...])

            def body(c, carry2):
                s0_ref[...] = qk(0, c)
                update(s0_ref, vT_ref[c])
                return carry2

            lax.fori_loop(0, n_main, body, 0)
            finalize(i, result())
            return carry

        lax.fori_loop(0, nq, simple_tile, 0)
        return

    pairs = (n_main - 4) // 2
    prep_q(0, 0)
    s0_ref[...] = qk(0, 0)

    def tile(i, carry):
        cur = i % 2
        start()

        def pair(c):
            s1_ref[...] = qk(cur, c + 1)
            update(s0_ref, vT_ref[c])
            s0_ref[...] = qk(cur, c + 2)
            update(s1_ref, vT_ref[c + 1])

        def body(p, carry2):
            pair(2 * p)
            return carry2

        if pairs:
            lax.fori_loop(0, pairs, body, 0, unroll=next(u for u in (7, 3, 2, 1) if pairs % u == 0))
        c = n_main - 4
        pair(c)
        s1_ref[...] = qk(cur, c + 3)
        st_ref[...] = _dot(kt_ref[...], qT_ref[cur])
        update(s0_ref, vT_ref[c + 2])
        prep_q(jnp.minimum(i + 1, nq - 1), 1 - cur)
        s0_ref[...] = qk(1 - cur, 0)
        update(s1_ref, vT_ref[c + 3])
        update(st_ref, vtT_ref[...])
        finalize(i, result())
        return carry

    lax.fori_loop(0, nq, tile, 0)


def _flash_scratch(rows, tk, t, tt):
    return [pltpu.VMEM((2, LANES, rows), BF16), pltpu.VMEM((t // tk, ACC_ROWS, tk), BF16),
            pltpu.VMEM((ACC_ROWS, tt), BF16), pltpu.VMEM((tk, rows), F32), pltpu.VMEM((tk, rows), F32),
            pltpu.VMEM((tt, rows), F32), pltpu.VMEM((1, rows), F32), pltpu.VMEM((ACC_ROWS, rows), F32)]


def _query_blocks_per_head(t, tq, want):
    return max(n for n in range(1, want + 1) if (t // tq) % n == 0)


def _diff_attn_kernel(q_ref, k_ref, v_ref, kt_ref, vt_ref, lam_ref, sw_ref, o_ref,
                      qT_ref, vT_ref, vtT_ref, s0_ref, s1_ref, st_ref, m_ref, acc_ref,
                      *, tq, tk, n_main, has_tail, post_scale):
    @pl.when(pl.program_id(2) == 0)
    def _():
        _transpose_values(v_ref, vt_ref, vT_ref, vtT_ref, tk=tk)

    def rows_of(i):
        return pl.ds(pl.multiple_of(i * tq, tq), tq)

    def prep_q(i, slot):
        q = q_ref[rows_of(i), :].astype(F32)
        lane = lax.broadcasted_iota(jnp.int32, q.shape, 1)
        qT_ref[slot, :, 0:tq] = jnp.where(lane < DIFF_HEAD_DIM, q, 0.0).T.astype(BF16)
        qT_ref[slot, :, tq:2 * tq] = jnp.where(lane >= DIFF_HEAD_DIM, q, 0.0).T.astype(BF16)

    def finalize(i, o_t):
        o = o_t.T
        y = o[0:tq, :] - lam_ref[...] * o[tq:2 * tq, :]
        y = _rms(y) * sw_ref[...] * post_scale
        o_ref[rows_of(i), :] = y.astype(o_ref.dtype)

    _flash_tiles(prep_q, finalize, qT_ref, k_ref, vT_ref, kt_ref, vtT_ref, s0_ref, s1_ref, st_ref, m_ref, acc_ref,
                 nq=q_ref.shape[0] // tq, tk=tk, n_main=n_main, has_tail=has_tail)


def _diff_attn(qk, v, qk_tail, v_tail, lam_vec, subln_w, post_scale, batch, has_tail):
    t = qk.shape[1]
    tq = min(512, t)
    tk = min(512, t)
    nqb = _query_blocks_per_head(t, tq, 4)
    tqb = t // nqb
    kern = functools.partial(_diff_attn_kernel, tq=tq, tk=tk, n_main=t // tk, has_tail=has_tail,
                             post_scale=post_scale)
    tt = qk_tail.shape[1]
    return pl.pallas_call(
        kern,
        grid=(batch, DIFF_HEADS, nqb),
        in_specs=[pl.BlockSpec((None, tqb, LANES), lambda b, h, i: (b, i, h)),
                  pl.BlockSpec((None, t, LANES), lambda b, h, i: (b, 0, DIFF_HEADS + h)),
                  pl.BlockSpec((None, t, LANES), lambda b, h, i: (b, 0, h)),
                  pl.BlockSpec((None, tt, LANES), lambda b, h, i: (b, 0, DIFF_HEADS + h)),
                  pl.BlockSpec((None, tt, LANES), lambda b, h, i: (b, 0, h)),
                  pl.BlockSpec((1, LANES), lambda b, h, i: (0, 0)),
                  pl.BlockSpec((1, LANES), lambda b, h, i: (0, 0))],
        out_specs=pl.BlockSpec((None, tqb, LANES), lambda b, h, i: (b, i, h)),
        out_shape=jax.ShapeDtypeStruct((batch, t, DIFF_HEADS * LANES), BF16),
        scratch_shapes=_flash_scratch(2 * tq, tk, t, tt),
        compiler_params=_cparams(("parallel", "parallel", "arbitrary")),
        name="diff_attn",
    )(qk, qk, v, qk_tail, v_tail, lam_vec, subln_w)


def _gqa_kernel(q_ref, k_ref, v_ref, kt_ref, vt_ref, o_ref,
                qT_ref, vT_ref, vtT_ref, s0_ref, s1_ref, st_ref, m_ref, acc_ref, *, tq, tk, n_main, has_tail):
    @pl.when(pl.program_id(2) == 0)
    def _():
        _transpose_values(v_ref, vt_ref, vT_ref, vtT_ref, tk=tk)

    def rows_of(i):
        return pl.ds(pl.multiple_of(i * tq, tq), tq)

    def prep_q(i, slot):
        for r in range(GQA_REP):
            qT_ref[slot, :, r * tq:(r + 1) * tq] = _to_bf16_t(q_ref[rows_of(i), r * LANES:(r + 1) * LANES])

    def finalize(i, o_t):
        o = o_t.T
        for r in range(GQA_REP):
            o_ref[rows_of(i), r * LANES:(r + 1) * LANES] = o[r * tq:(r + 1) * tq, :].astype(o_ref.dtype)

    _flash_tiles(prep_q, finalize, qT_ref, k_ref, vT_ref, kt_ref, vtT_ref, s0_ref, s1_ref, st_ref, m_ref, acc_ref,
                 nq=q_ref.shape[0] // tq, tk=tk, n_main=n_main, has_tail=has_tail)


def _gqa_attn(qk, v, qk_tail, v_tail, batch, has_tail):
    t = qk.shape[1]
    tq = min(256, t)
    tk = min(512, t)
    gw = GQA_REP * LANES
    nqb = _query_blocks_per_head(t, tq, 8)
    tqb = t // nqb
    kern = functools.partial(_gqa_kernel, tq=tq, tk=tk, n_main=t // tk, has_tail=has_tail)
    tt = qk_tail.shape[1]
    return pl.pallas_call(
        kern,
        grid=(batch, GQA_KV_HEADS, nqb),
        in_specs=[pl.BlockSpec((None, tqb, gw), lambda b, g, i: (b, i, g)),
                  pl.BlockSpec((None, t, LANES), lambda b, g, i: (b, 0, GQA_HEADS + g)),
                  pl.BlockSpec((None, t, LANES), lambda b, g, i: (b, 0, g)),
                  pl.BlockSpec((None, tt, LANES), lambda b, g, i: (b, 0, GQA_HEADS + g)),
                  pl.BlockSpec((None, tt, LANES), lambda b, g, i: (b, 0, g))],
        out_specs=pl.BlockSpec((None, tqb, gw), lambda b, g, i: (b, i, g)),
        out_shape=jax.ShapeDtypeStruct((batch, t, GQA_HEADS * LANES), BF16),
        scratch_shapes=_flash_scratch(GQA_REP * tq, tk, t, tt),
        compiler_params=_cparams(("parallel", "parallel", "arbitrary")),
        name="gqa_attn",
    )(qk, qk, v, qk_tail, v_tail)


def _rglru_kernel(xf_ref, xfp_ref, xfn_ref, xb_ref, xbp_ref, xbn_ref, cw_ref, cb_ref, wg_ref, bg_ref, cv_ref,
                  h0_ref, hf_ref, hb_ref, ht_ref, a_scr, b_scr, st_scr, *, tb, nblk):
    i = pl.program_id(1)

    @pl.when(i == 0)
    def _():
        st_scr[...] = h0_ref[...]

    row = lax.broadcasted_iota(jnp.int32, (tb, LRU_WIDTH), 0)

    def gates(d, x_ref, xp_ref, xn_ref, blk):
        x = x_ref[...]
        prev = xp_ref[SUBLANES - 1:SUBLANES, :] * (blk > 0).astype(F32)
        has_next = (blk < nblk - 1).astype(F32)
        nxt0 = xn_ref[0:1, :] * has_next
        nxt1 = xn_ref[1:2, :] * has_next
        xm1 = jnp.where(row == 0, prev, pltpu.roll(x, 1, 0))
        xp1 = jnp.where(row == tb - 1, nxt0, pltpu.roll(x, tb - 1, 0))
        xp2 = jnp.where(row == tb - 2, nxt0, jnp.where(row == tb - 1, nxt1, pltpu.roll(x, tb - 2, 0)))
        y = xm1 * cw_ref[0:1, :] + x * cw_ref[1:2, :] + xp1 * cw_ref[2:3, :] + xp2 * cw_ref[3:4, :] + cb_ref[...]
        yb = y.astype(BF16)
        for c in range(LRU_BLOCKS):
            sl = slice(c * LRU_BLOCK, (c + 1) * LRU_BLOCK)
            z = _dot(yb[:, sl], wg_ref[d, c]) + bg_ref[d, c]
            r = jax.nn.sigmoid(z[:, 0:LRU_BLOCK])
            g = jax.nn.sigmoid(z[:, LRU_BLOCK:2 * LRU_BLOCK])
            log_a = r * cv_ref[d:d + 1, sl]
            a = jnp.exp(log_a)
            a_scr[d, :, sl] = a
            b_scr[d, :, sl] = jnp.sqrt(-jnp.tanh(log_a) * (1.0 + a * a)) * (g * y[:, sl])

    gates(0, xf_ref, xfp_ref, xfn_ref, i)
    gates(1, xb_ref, xbp_ref, xbn_ref, nblk - 1 - i)

    row8 = lax.broadcasted_iota(jnp.int32, (SUBLANES, LRU_WIDTH), 0)
    nt = tb // SUBLANES

    def scan(d, out_ref):
        rev = d == 1

        def body(r, h):
            off = pl.multiple_of((nt - 1 - r if rev else r) * SUBLANES, SUBLANES)
            a8 = a_scr[d, pl.ds(off, SUBLANES), :]
            b8 = b_scr[d, pl.ds(off, SUBLANES), :]
            for s in (1, 2, 4):
                if rev:
                    ok = row8 < SUBLANES - s
                    sh = SUBLANES - s
                else:
                    ok = row8 >= s
                    sh = s
                a_sh = jnp.where(ok, pltpu.roll(a8, sh, 0), 1.0)
                b_sh = jnp.where(ok, pltpu.roll(b8, sh, 0), 0.0)
                b8 = a8 * b_sh + b8
                a8 = a8 * a_sh
            h8 = a8 * h + b8
            out_ref[pl.ds(off, SUBLANES), :] = h8
            return h8[0:1, :] if rev else h8[SUBLANES - 1:SUBLANES, :]

        st_scr[d:d + 1, :] = lax.fori_loop(0, nt, body, st_scr[d:d + 1, :])

    scan(0, hf_ref)
    scan(1, hb_ref)

    @pl.when(i == nblk - 1)
    def _():
        ht_ref[...] = st_scr[...]


def _rglru(gx, seq, batch, conv_w, conv_b, wg, bg, cv, h0):
    m = gx.shape[0]
    tb = min(512, seq)
    nblk = seq // tb
    hb8 = tb // SUBLANES
    last8 = m // SUBLANES - 1
    w = LRU_WIDTH

    def fidx(b, i):
        return b * nblk + i

    def bidx(b, i):
        return b * nblk + nblk - 1 - i

    def specs(idx):
        return [pl.BlockSpec((tb, w), lambda b, i: (idx(b, i), 1)),
                pl.BlockSpec((SUBLANES, w), lambda b, i: (jnp.maximum(idx(b, i) * hb8 - 1, 0), 1)),
                pl.BlockSpec((SUBLANES, w), lambda b, i: (jnp.minimum((idx(b, i) + 1) * hb8, last8), 1))]

    full = lambda shape: pl.BlockSpec(shape, lambda b, i: (0,) * len(shape))
    kern = functools.partial(_rglru_kernel, tb=tb, nblk=nblk)
    return pl.pallas_call(
        kern,
        grid=(batch, nblk),
        in_specs=specs(fidx) + specs(bidx) + [full(conv_w.shape), full((1, w)), full(wg.shape), full(bg.shape),
                                              full(cv.shape), pl.BlockSpec((None, 2, w), lambda b, i: (b, 0, 0))],
        out_specs=[pl.BlockSpec((tb, w), lambda b, i: (fidx(b, i), 0)),
                   pl.BlockSpec((tb, w), lambda b, i: (bidx(b, i), 0)),
                   pl.BlockSpec((None, 2, w), lambda b, i: (b, 0, 0))],
        out_shape=[jax.ShapeDtypeStruct((m, w), F32), jax.ShapeDtypeStruct((m, w), F32),
                   jax.ShapeDtypeStruct((batch, 2, w), F32)],
        scratch_shapes=[pltpu.VMEM((2, tb, w), F32), pltpu.VMEM((2, tb, w), F32), pltpu.VMEM((2, w), F32)],
        compiler_params=_cparams(("parallel", "arbitrary")),
        name="rglru",
    )(gx, gx, gx, gx, gx, gx, conv_w, conv_b.reshape(1, w), wg, bg, cv, h0)


def _hgrn_consts(c):
    t = np.arange(c)
    blocks = [(t[None, :] <= t[:, None]).astype(np.float32)]
    masks = []
    m = c // 2
    while m >= 1:
        mid = (t // (2 * m)) * (2 * m) + m
        right = t >= mid
        if 2 * m < SUBLANES:
            u = t[None, :]
            g = np.where(right[:, None], (u >= mid[:, None]) & (u <= t[:, None]),
                         (u > t[:, None]) & (u < mid[:, None]))
            blocks.append(g.astype(np.float32))
        same = (t[:, None] // (2 * m)) == (t[None, :] // (2 * m))
        masks.append((same & right[:, None] & (~right)[None, :]).astype(np.float32))
        m //= 2
    masks.append(np.eye(c, dtype=np.float32))
    flip = lambda a: a[::-1, ::-1]
    ones = np.ones((BF16_SUBLANES, c), np.float32)
    w = np.stack([np.concatenate(blocks + [ones], 0), np.concatenate([flip(b) for b in blocks] + [ones], 0)])
    cm = np.stack([np.stack(masks), np.stack([flip(a) for a in masks])])
    return w, cm


def _hgrn_kernel(qf_ref, ff_ref, vf_ref, qb_ref, fb_ref, vb_ref, lb_ref, wc_ref, cm_ref, s0_ref,
                 of_ref, ob_ref, st_ref, st_scr, *, c, levels, nchunk):
    i = pl.program_id(1)

    @pl.when(i == 0)
    def _():
        st_scr[...] = s0_ref[...]

    def prep(d, q_ref, f_ref, v_ref):
        q = q_ref[...]
        q = q * jax.nn.sigmoid(q)
        lb = lb_ref[d:d + 1, :]
        f = lb + (1.0 - lb) * jax.nn.sigmoid(f_ref[...])
        kk = 1.0 - f
        g = jnp.log(f)
        g1 = g.astype(BF16)
        g2 = (g - g1.astype(F32)).astype(BF16)
        w = wc_ref[d]
        x = _dot(w, g1) + _dot(w, g2)
        cum = x[0:c]
        n_small = (w.shape[0] - BF16_SUBLANES) // c - 1
        tot = x[(1 + n_small) * c:(1 + n_small) * c + 1]
        e_lev = []
        m, small = c // 2, 0
        while m >= 1:
            if 2 * m >= SUBLANES:
                xr = cum.reshape(c // (2 * m), 2 * m, HGRN_WIDTH)
                ref = xr[:, m - 1 + d:m + d, :]
                e_lev.append(jnp.exp(-jnp.abs(xr - ref)).reshape(c, HGRN_WIDTH))
            else:
                e_lev.append(jnp.exp(x[(1 + small) * c:(2 + small) * c]))
                small += 1
            m //= 2
        return q, kk, v_ref[...].astype(BF16), jnp.exp(cum), jnp.exp(tot - cum), jnp.exp(tot), e_lev

    def head(d, h, q, kk, v, e_in, e_out, e_tot, e_lev, o_ref):
        sl = slice(h * LANES, (h + 1) * LANES)
        st = st_scr[d, h]
        qh, kh, vh = q[:, sl], kk[:, sl], v[:, sl]
        o = _dot_nt((qh * e_in[:, sl]).astype(BF16), st.astype(BF16))
        sc = cm_ref[d, levels] * _dot_nt(qh.astype(BF16), kh.astype(BF16))
        for l in range(levels):
            el = e_lev[l][:, sl]
            sc = sc + cm_ref[d, l] * _dot_nt((qh * el).astype(BF16), (kh * el).astype(BF16))
        o_ref[:, sl] = o + _dot(sc.astype(BF16), vh)
        st_scr[d, h] = st * e_tot[:, sl] + _dot_tn(vh, (kh * e_out[:, sl]).astype(BF16))

    fwd = prep(0, qf_ref, ff_ref, vf_ref)
    bwd = prep(1, qb_ref, fb_ref, vb_ref)
    for h in range(HGRN_HEADS):
        head(0, h, *fwd, of_ref)
        head(1, h, *bwd, ob_ref)

    @pl.when(i == nchunk - 1)
    def _():
        st_ref[...] = st_scr[...]


def _hgrn(z, seq, batch, lb, s0):
    m = z.shape[0]
    c = min(HGRN_CHUNK, seq)
    nchunk = seq // c
    levels = int(math.log2(c))
    wnp, cmnp = _hgrn_consts(c)
    wc = jnp.asarray(wnp, BF16)
    cm = jnp.asarray(cmnp, F32)
    w = HGRN_WIDTH

    def fidx(b, i):
        return b * nchunk + i

    def bidx(b, i):
        return b * nchunk + nchunk - 1 - i

    blk = lambda idx, col: pl.BlockSpec((c, w), lambda b, i: (idx(b, i), col))
    full = lambda shape: pl.BlockSpec(shape, lambda b, i: (0,) * len(shape))
    st_spec = pl.BlockSpec((None, 2, HGRN_HEADS, LANES, LANES), lambda b, i: (b, 0, 0, 0, 0))
    kern = functools.partial(_hgrn_kernel, c=c, levels=levels, nchunk=nchunk)
    return pl.pallas_call(
        kern,
        grid=(batch, nchunk),
        in_specs=[blk(fidx, 0), blk(fidx, 1), blk(fidx, 3), blk(bidx, 0), blk(bidx, 2), blk(bidx, 3),
                  full(lb.shape), full(wc.shape), full(cm.shape), st_spec],
        out_specs=[pl.BlockSpec((c, w), lambda b, i: (fidx(b, i), 0)),
                   pl.BlockSpec((c, w), lambda b, i: (bidx(b, i), 0)), st_spec],
        out_shape=[jax.ShapeDtypeStruct((m, w), F32), jax.ShapeDtypeStruct((m, w), F32),
                   jax.ShapeDtypeStruct((batch, 2, HGRN_HEADS, LANES, LANES), F32)],
        scratch_shapes=[pltpu.VMEM((2, HGRN_HEADS, LANES, LANES), F32)],
        compiler_params=_cparams(("parallel", "arbitrary")),
        name="hgrn2",
    )(z, z, z, z, z, z, lb, wc, cm, s0)


def _outproj_kernel(x_ref, p0_ref, p1_ref, g_ref, att_ref, nw_ref, w_ref, gt_ref, o_ref, *, mode):
    half = w_ref.shape[0] // 2
    s = p0_ref[...] + p1_ref[...]
    g = g_ref[...]
    if mode == "ab":
        a = s * jax.nn.gelu(g, approximate=True)
    else:
        parts = []
        for h in range(HGRN_HEADS):
            sl = slice(h * LANES, (h + 1) * LANES)
            parts.append(_rms(s[:, sl]) * nw_ref[...])
        a = jnp.concatenate(parts, axis=-1) * (g * jax.nn.sigmoid(g))
    acc = _dot(a.astype(BF16), w_ref[0:half, :]) + _dot(att_ref[...], w_ref[half:2 * half, :])
    o_ref[...] = x_ref[...] + gt_ref[...] * acc


def _outproj(x2d, seq, p0, p1, gsrc, gcol, att, head_norm_w, w_out, mod, mode):
    m, d = x2d.shape
    tm = min(512, seq)
    tpb = seq // tm if mod.shape[0] > 1 else m
    hw = w_out.shape[0] // 2
    kern = functools.partial(_outproj_kernel, mode=mode)
    return pl.pallas_call(
        kern,
        grid=(m // tm,),
        in_specs=[pl.BlockSpec((tm, d), lambda i: (i, 0)),
                  pl.BlockSpec((tm, hw), lambda i: (i, 0)),
                  pl.BlockSpec((tm, hw), lambda i: (i, 0)),
                  pl.BlockSpec((tm, hw), lambda i: (i, gcol)),
                  pl.BlockSpec((tm, hw), lambda i: (i, 0)),
                  pl.BlockSpec((1, LANES), lambda i: (0, 0)),
                  pl.BlockSpec(w_out.shape, lambda i: (0, 0)),
                  pl.BlockSpec((None, None, 1, d), lambda i: (i // tpb, 2, 0, 0))],
        out_specs=pl.BlockSpec((tm, d), lambda i: (i, 0)),
        out_shape=jax.ShapeDtypeStruct((m, d), F32),
        compiler_params=_cparams(("parallel",)),
        name="outproj_" + mode,
    )(x2d, p0, p1, gsrc, att, head_norm_w, w_out, mod)


def _ffn_kernel(x_ref, nw_ref, sh_ref, sc_ref, gt_ref, wg_ref, wu_ref, wd_ref, fw_ref, o_ref, hn_ref, *, final):
    j = pl.program_id(1)

    @pl.when(j == 0)
    def _():
        h = _rms(x_ref[...]) * nw_ref[...]
        hn_ref[...] = (h * (1.0 + sc_ref[...]) + sh_ref[...]).astype(BF16)
        o_ref[...] = jnp.zeros(o_ref.shape, F32)

    hn = hn_ref[...]
    g = _dot(hn, wg_ref[...])
    u = _dot(hn, wu_ref[...])
    a = (g * jax.nn.sigmoid(g) * u).astype(BF16)
    o_ref[...] += _dot(a, wd_ref[...])

    @pl.when(j == pl.num_programs(1) - 1)
    def _():
        y = x_ref[...] + gt_ref[...] * o_ref[...]
        if final:
            y = _rms(y) * fw_ref[...]
        o_ref[...] = y


def _ffn(x2d, seq, norm_w, mod, w_gate, w_up, w_down, final_w, final):
    m, d = x2d.shape
    f = w_gate.shape[1]
    tm = min(512, seq)
    tf = 512
    tpb = seq // tm if mod.shape[0] > 1 else m
    mspec = lambda k: pl.BlockSpec((None, None, 1, d), lambda i, j: (i // tpb, k, 0, 0))
    kern = functools.partial(_ffn_kernel, final=final)
    return pl.pallas_call(
        kern,
        grid=(m // tm, f // tf),
        in_specs=[pl.BlockSpec((tm, d), lambda i, j: (i, 0)),
                  pl.BlockSpec((1, d), lambda i, j: (0, 0)),
                  mspec(3), mspec(4), mspec(5),
                  pl.BlockSpec((d, tf), lambda i, j: (0, j)),
                  pl.BlockSpec((d, tf), lambda i, j: (0, j)),
                  pl.BlockSpec((tf, d), lambda i, j: (j, 0)),
                  pl.BlockSpec((1, d), lambda i, j: (0, 0))],
        out_specs=pl.BlockSpec((tm, d), lambda i, j: (i, 0)),
        out_shape=jax.ShapeDtypeStruct((m, d), F32),
        scratch_shapes=[pltpu.VMEM((tm, d), BF16)],
        compiler_params=_cparams(("parallel", "arbitrary")),
        name="ffn",
    )(x2d, norm_w.reshape(1, d), mod, mod, mod, w_gate, w_up, w_down, final_w.reshape(1, d))


def _rope_tables(rows, head_dim):
    n_freq = head_dim // 4
    half = head_dim // 2
    row = jnp.repeat(jnp.arange(rows, dtype=F32), GRID_W)
    col = jnp.tile(jnp.arange(GRID_W, dtype=F32), rows)
    inv = ROPE_THETA ** (-jnp.arange(n_freq, dtype=F32) / n_freq)
    ang = jnp.concatenate([row[:, None] * inv, col[:, None] * inv], axis=-1)
    cos, sin = jnp.cos(ang), jnp.sin(ang)
    reps = LANES // head_dim
    zero = jnp.zeros_like(sin)
    cos_t = jnp.tile(jnp.concatenate([cos, cos], -1), (1, reps))
    if half * 2 == LANES:
        return cos_t, jnp.concatenate([-sin, sin], -1), None
    sin_a = jnp.tile(jnp.concatenate([-sin, zero], -1), (1, reps))
    sin_b = jnp.tile(jnp.concatenate([zero, sin], -1), (1, reps))
    return cos_t, sin_a, sin_b


def _identity_rope(t, head_dim):
    one = jnp.ones((t, LANES), F32)
    zero = jnp.zeros((t, LANES), F32)
    return (one, zero, None) if head_dim == LANES else (one, zero, zero)


def kernel(x, c, ctx, c_ctx, mod_w, mod_b, norm_mix_w, norm_ffn_w, ffn_w_gate, ffn_w_up, ffn_w_down, ab_w_in, ab_w_out, lru_conv_w, lru_conv_b, lru_wa, lru_ba, lru_wx, lru_bx, lru_lambda, diff_lq1, diff_lk1, diff_lq2, diff_lk2, diff_subln_w, cd_w_in, cd_w_out, hgrn_lb_logits, hgrn_norm_w, gqa_q_norm_w, gqa_k_norm_w, final_norm_w):
    batch, seq, d = x.shape
    clen = ctx.shape[1]
    depth = mod_w.shape[0]
    rows = seq // GRID_W

    cc = jnp.zeros((SUBLANES, d), F32).at[0:batch].set(c).at[batch].set(c_ctx)
    mods = _modulation(cc, mod_w, mod_b)
    lb_cum = jnp.cumsum(jax.nn.softmax(hgrn_lb_logits.astype(F32), axis=1), axis=1)

    xl = x.reshape(batch * seq, d)
    xc = ctx.reshape(batch * clen, d)

    for l in range(depth):
        last = l == depth - 1
        m_lat = mods[l, 0:batch].reshape(batch, N_MOD, 1, d)
        m_ctx = mods[l, batch:batch + 1].reshape(1, N_MOD, 1, d)
        streams = ((_normmod(xl, seq, norm_mix_w[l], m_lat), seq), (_normmod(xc, clen, norm_mix_w[l], m_ctx), clen))
        if l % 2 == 0:
            e = l // 2
            lambda_init = 0.8 - 0.6 * math.exp(-0.3 * l)
            w_in = ab_w_in[e].astype(BF16)
            qscale = DIFF_HEAD_DIM ** -0.5 * LOG2E
            ropes = (_rope_tables(rows, DIFF_HEAD_DIM), _identity_rope(clen, DIFF_HEAD_DIM))
            proj = []
            for (hn, t), rp in zip(streams, ropes):
                rp = [jnp.stack([tab * qscale, tab]) for tab in rp]
                gx = _inproj(hn, t, w_in, 0, 2048, F32, 1024)
                qk = _inproj(hn, t, w_in, 2048, 2048, BF16, 1024, rope=rp, rope_half=DIFF_HEAD_DIM // 2)
                v = _inproj(hn, t, w_in, 4096, 1024, BF16, 1024)
                proj.append((gx, qk.reshape(batch, t, 2048), v.reshape(batch, t, 1024)))
            (gx_l, qk_l, v_l), (gx_c, qk_c, v_c) = proj
            wg = jnp.concatenate([lru_wa[e], lru_wx[e]], axis=-1).astype(BF16)
            bg = jnp.concatenate([lru_ba[e].reshape(2, LRU_BLOCKS, 1, LRU_BLOCK),
                                  lru_bx[e].reshape(2, LRU_BLOCKS, 1, LRU_BLOCK)], axis=-1)
            cv = -LRU_C * jax.nn.softplus(-lru_lambda[e].astype(F32))
            h0 = jnp.zeros((batch, 2, LRU_WIDTH), F32)
            hf_c, hb_c, h_ctx = _rglru(gx_c, clen, batch, lru_conv_w[e], lru_conv_b[e], wg, bg, cv, h0)
            hf_l, hb_l, _ = _rglru(gx_l, seq, batch, lru_conv_w[e], lru_conv_b[e], wg, bg, cv, h_ctx)
            lam = (jnp.exp(jnp.sum(diff_lq1[e].astype(F32) * diff_lk1[e].astype(F32)))
                   - jnp.exp(jnp.sum(diff_lq2[e].astype(F32) * diff_lk2[e].astype(F32))) + lambda_init)
            lam_vec = jnp.full((1, LANES), lam, F32)
            sw = diff_subln_w[e].reshape(1, LANES)
            d_l = _diff_attn(qk_l, v_l, qk_c, v_c, lam_vec, sw, 1.0 - lambda_init, batch, True)
            w_out = ab_w_out[e].astype(BF16)
            dummy_nw = jnp.ones((1, LANES), F32)
            xl = _outproj(xl, seq, hf_l, hb_l, gx_l, 0, d_l.reshape(batch * seq, 1024), dummy_nw, w_out, m_lat, "ab")
            if not last:
                d_c = _diff_attn(qk_c, v_c, qk_c, v_c, lam_vec, sw, 1.0 - lambda_init, batch, False)
                xc = _outproj(xc, clen, hf_c, hb_c, gx_c, 0, d_c.reshape(batch * clen, 1024), dummy_nw, w_out,
                              m_ctx, "ab")
        else:
            o = l // 2
            lb = lb_cum[:, l] - lb_cum[:, 0]
            w_in = cd_w_in[o].astype(BF16)
            qscale = GQA_HEAD_DIM ** -0.5 * LOG2E
            chunk_w = jnp.concatenate([jnp.tile(gqa_q_norm_w[o] * qscale, GQA_HEADS),
                                       jnp.tile(gqa_k_norm_w[o], GQA_KV_HEADS)]).reshape(1, 1280)
            ropes = (_rope_tables(rows, GQA_HEAD_DIM), _identity_rope(clen, GQA_HEAD_DIM))
            proj = []
            for (hn, t), rp in zip(streams, ropes):
                z = _inproj(hn, t, w_in, 0, 5120, F32, 1024)
                qk = _inproj(hn, t, w_in, 5120, 1280, BF16, 1280, chunk_w=chunk_w, norm_chunks=10,
                             rope=rp[0:2], rope_half=GQA_HEAD_DIM // 2)
                v = _inproj(hn, t, w_in, 6400, 256, BF16, 256)
                proj.append((z, qk.reshape(batch, t, 1280), v.reshape(batch, t, 256)))
            (z_l, qk_l, v_l), (z_c, qk_c, v_c) = proj
            s0 = jnp.zeros((batch, 2, HGRN_HEADS, LANES, LANES), F32)
            of_c, ob_c, s_ctx = _hgrn(z_c, clen, batch, lb, s0)
            of_l, ob_l, _ = _hgrn(z_l, seq, batch, lb, s_ctx)
            att_l = _gqa_attn(qk_l, v_l, qk_c, v_c, batch, True)
            w_out = cd_w_out[o].astype(BF16)
            hnw = hgrn_norm_w[o].reshape(1, LANES)
            xl = _outproj(xl, seq, of_l, ob_l, z_l, 4, att_l.reshape(batch * seq, 1024), hnw, w_out, m_lat, "cd")
            if not last:
                att_c = _gqa_attn(qk_c, v_c, qk_c, v_c, batch, False)
                xc = _outproj(xc, clen, of_c, ob_c, z_c, 4, att_c.reshape(batch * clen, 1024), hnw, w_out,
                              m_ctx, "cd")
        wgt, wup, wdn = ffn_w_gate[l].astype(BF16), ffn_w_up[l].astype(BF16), ffn_w_down[l].astype(BF16)
        xl = _ffn(xl, seq, norm_ffn_w[l], m_lat, wgt, wup, wdn, final_norm_w, last)
        if not last:
            xc = _ffn(xc, clen, norm_ffn_w[l], m_ctx, wgt, wup, wdn, final_norm_w, False)

    return xl.reshape(batch, seq, d)
```

```python
import functools
import math

import numpy as np
import jax
import jax.numpy as jnp
from jax import lax
from jax.experimental import pallas as pl
from jax.experimental.pallas import tpu as pltpu

F32 = jnp.float32
BF16 = jnp.bfloat16

GRID_W = 64
NORM_EPS = 1e-6
ROPE_THETA = 10000.0
N_MOD = 6
LRU_WIDTH = 1024
LRU_BLOCKS = 8
LRU_BLOCK = 128
LRU_C = 8.0
DIFF_HEADS = 8
DIFF_HEAD_DIM = 64
HGRN_HEADS = 8
HGRN_WIDTH = 1024
GQA_HEADS = 8
GQA_KV_HEADS = 2
GQA_REP = 4
GQA_HEAD_DIM = 128
LOG2E = 1.4426950408889634

LANES = 128
SUBLANES = 8
BF16_SUBLANES = 16
VMEM_LIMIT = 56 * 1024 * 1024

HGRN_CHUNK = 128


def _cparams(sem):
    return pltpu.CompilerParams(dimension_semantics=sem, vmem_limit_bytes=VMEM_LIMIT)


def _dot(a, b):
    return jnp.dot(a, b, preferred_element_type=F32)


def _dot_nt(a, b):
    return lax.dot_general(a, b, (((1,), (1,)), ((), ())), preferred_element_type=F32)


def _dot_tn(a, b):
    return lax.dot_general(a, b, (((0,), (0,)), ((), ())), preferred_element_type=F32)


def _rms(x):
    return x * lax.rsqrt(jnp.mean(x * x, axis=-1, keepdims=True) + NORM_EPS)


def _mod_kernel(c_ref, w_ref, b_ref, o_ref):
    c = c_ref[...]
    a = c * jax.nn.sigmoid(c)
    o_ref[...] = jnp.dot(a, w_ref[...], preferred_element_type=F32,
                         precision=lax.Precision.HIGHEST) + b_ref[...]


def _modulation(cc, mod_w, mod_b):
    depth, d, n = mod_w.shape
    tn = 1024
    return pl.pallas_call(
        _mod_kernel,
        grid=(depth, n // tn),
        in_specs=[pl.BlockSpec((SUBLANES, d), lambda l, j: (0, 0)),
                  pl.BlockSpec((None, d, tn), lambda l, j: (l, 0, j)),
                  pl.BlockSpec((None, 1, tn), lambda l, j: (l, 0, j))],
        out_specs=pl.BlockSpec((None, SUBLANES, tn), lambda l, j: (l, 0, j)),
        out_shape=jax.ShapeDtypeStruct((depth, SUBLANES, n), F32),
        compiler_params=_cparams(("parallel", "parallel")),
        name="modulation",
    )(cc, mod_w, mod_b.reshape(depth, 1, n))


def _normmod_kernel(x_ref, nw_ref, sh_ref, sc_ref, o_ref):
    h = _rms(x_ref[...]) * nw_ref[...]
    o_ref[...] = (h * (1.0 + sc_ref[...]) + sh_ref[...]).astype(o_ref.dtype)


def _normmod(x2d, seq, norm_w, mod):
    m, d = x2d.shape
    tm = min(512, seq)
    tpb = seq // tm if mod.shape[0] > 1 else m
    return pl.pallas_call(
        _normmod_kernel,
        grid=(m // tm,),
        in_specs=[pl.BlockSpec((tm, d), lambda i: (i, 0)),
                  pl.BlockSpec((1, d), lambda i: (0, 0)),
                  pl.BlockSpec((None, None, 1, d), lambda i: (i // tpb, 0, 0, 0)),
                  pl.BlockSpec((None, None, 1, d), lambda i: (i // tpb, 1, 0, 0))],
        out_specs=pl.BlockSpec((tm, d), lambda i: (i, 0)),
        out_shape=jax.ShapeDtypeStruct((m, d), BF16),
        compiler_params=_cparams(("parallel",)),
        name="normmod",
    )(x2d, norm_w.reshape(1, d), mod, mod)


def _inproj_kernel(*refs, n_chunks, norm_chunks, rope_half):
    it = iter(refs)
    x_ref, w_ref = next(it), next(it)
    cw_ref = next(it) if norm_chunks else None
    if rope_half:
        cos_ref, sa_ref = next(it), next(it)
        sb_ref = next(it) if rope_half * 2 != LANES else None
    o_ref = next(it)

    acc = _dot(x_ref[...], w_ref[...])
    for c in range(n_chunks):
        sl = slice(c * LANES, (c + 1) * LANES)
        y = acc[:, sl]
        if c < norm_chunks:
            y = _rms(y) * cw_ref[:, sl]
        if rope_half:
            if rope_half * 2 == LANES:
                y = y * cos_ref[...] + pltpu.roll(y, rope_half, 1) * sa_ref[...]
            else:
                y = (y * cos_ref[...] + pltpu.roll(y, LANES - rope_half, 1) * sa_ref[...]
                     + pltpu.roll(y, rope_half, 1) * sb_ref[...])
        o_ref[:, sl] = y.astype(o_ref.dtype)


def _inproj(hn, seq, w, col0, n, out_dtype, tn, chunk_w=None, norm_chunks=0, rope=None, rope_half=0):
    m, d = hn.shape
    tm = min(1024, seq)
    assert col0 % tn == 0 and n % tn == 0
    jb = col0 // tn
    in_specs = [pl.BlockSpec((tm, d), lambda i, j: (i, 0)),
                pl.BlockSpec((d, tn), lambda i, j: (0, jb + j))]
    args = [hn, w]
    if norm_chunks:
        in_specs.append(pl.BlockSpec((1, tn), lambda i, j: (0, j)))
        args.append(chunk_w)
    if rope_half:
        spt = seq // tm
        for t in rope:
            if t.ndim == 3:
                in_specs.append(pl.BlockSpec((None, tm, LANES), lambda i, j: (j, i % spt, 0)))
            else:
                in_specs.append(pl.BlockSpec((tm, LANES), lambda i, j: (i % spt, 0)))
            args.append(t)
    kern = functools.partial(_inproj_kernel, n_chunks=tn // LANES, norm_chunks=norm_chunks, rope_half=rope_half)
    return pl.pallas_call(
        kern,
        grid=(m // tm, n // tn),
        in_specs=in_specs,
        out_specs=pl.BlockSpec((tm, tn), lambda i, j: (i, j)),
        out_shape=jax.ShapeDtypeStruct((m, n), out_dtype),
        compiler_params=_cparams(("parallel", "parallel")),
        name="inproj",
    )(*args)


ACC_ROWS = LANES + BF16_SUBLANES


def _to_bf16_t(x):
    return x.astype(F32).T.astype(BF16)


def _transpose_values(v_ref, vt_ref, vT_ref, vtT_ref, *, tk):
    for c in range(v_ref.shape[0] // tk):
        vT_ref[c, 0:LANES, :] = _to_bf16_t(v_ref[c * tk:(c + 1) * tk, :])
        vT_ref[c, LANES:ACC_ROWS, :] = jnp.ones((ACC_ROWS - LANES, tk), BF16)
    vtT_ref[0:LANES, :] = _to_bf16_t(vt_ref[...])
    vtT_ref[LANES:ACC_ROWS, :] = jnp.ones((ACC_ROWS - LANES, vt_ref.shape[0]), BF16)


def _flash_tiles(prep_q, finalize, qT_ref, k_ref, vT_ref, kt_ref, vtT_ref, s0_ref, s1_ref, st_ref, m_ref, acc_ref,
                 *, nq, tk, n_main, has_tail):
    def qk(slot, c):
        off = pl.multiple_of(c * tk, tk)
        return _dot(k_ref[pl.ds(off, tk), :], qT_ref[slot])

    def update(s_ref, vT):
        m_prev = m_ref[...]
        m_new = jnp.maximum(m_prev, jnp.max(s_ref[...], axis=0, keepdims=True))
        alpha = jnp.exp2(m_prev - m_new)
        p = jnp.exp2(s_ref[...] - m_new).astype(BF16)
        acc_ref[...] = alpha * acc_ref[...] + _dot(vT, p)
        m_ref[...] = m_new

    def start():
        m_ref[...] = jnp.full(m_ref.shape, -jnp.inf, F32)
        acc_ref[...] = jnp.zeros(acc_ref.shape, F32)

    def result():
        return acc_ref[0:LANES, :] / acc_ref[LANES:LANES + 1, :]

    if not (has_tail and n_main >= 4 and n_main % 2 == 0):
        def simple_tile(i, carry):
            prep_q(i, 0)
            start()
            if has_tail:
                st_ref[...] = _dot(kt_ref[...], qT_ref[0])
                update(st_ref, vtT_ref[...])

            def body(c, carry2):
                s0_ref[...] = qk(0, c)
                update(s0_ref, vT_ref[c])
                return carry2

            lax.fori_loop(0, n_main, body, 0)
            finalize(i, result())
            return carry

        lax.fori_loop(0, nq, simple_tile, 0)
        return

    pairs = (n_main - 4) // 2
    prep_q(0, 0)
    s0_ref[...] = qk(0, 0)

    def tile(i, carry):
        cur = i % 2
        start()

        def pair(c):
            s1_ref[...] = qk(cur, c + 1)
            update(s0_ref, vT_ref[c])
            s0_ref[...] = qk(cur, c + 2)
            update(s1_ref, vT_ref[c + 1])

        def body(p, carry2):
            pair(2 * p)
            return carry2

        if pairs:
            lax.fori_loop(0, pairs, body, 0, unroll=next(u for u in (7, 3, 2, 1) if pairs % u == 0))
        c = n_main - 4
        pair(c)
        s1_ref[...] = qk(cur, c + 3)
        st_ref[...] = _dot(kt_ref[...], qT_ref[cur])
        update(s0_ref, vT_ref[c + 2])
        prep_q(jnp.minimum(i + 1, nq - 1), 1 - cur)
        s0_ref[...] = qk(1 - cur, 0)
        update(s1_ref, vT_ref[c + 3])
        update(st_ref, vtT_ref[...])
        finalize(i, result())
        return carry

    lax.fori_loop(0, nq, tile, 0)


def _flash_scratch(rows, tk, t, tt):
    return [pltpu.VMEM((2, LANES, rows), BF16), pltpu.VMEM((t // tk, ACC_ROWS, tk), BF16),
            pltpu.VMEM((ACC_ROWS, tt), BF16), pltpu.VMEM((tk, rows), F32), pltpu.VMEM((tk, rows), F32),
            pltpu.VMEM((tt, rows), F32), pltpu.VMEM((1, rows), F32), pltpu.VMEM((ACC_ROWS, rows), F32)]


def _query_blocks_per_head(t, tq, want):
    return max(n for n in range(1, want + 1) if (t // tq) % n == 0)


def _diff_attn_kernel(q_ref, k_ref, v_ref, kt_ref, vt_ref, lam_ref, sw_ref, o_ref,
                      qT_ref, vT_ref, vtT_ref, s0_ref, s1_ref, st_ref, m_ref, acc_ref,
                      *, tq, tk, n_main, has_tail, post_scale):
    @pl.when(pl.program_id(2) == 0)
    def _():
        _transpose_values(v_ref, vt_ref, vT_ref, vtT_ref, tk=tk)

    def rows_of(i):
        return pl.ds(pl.multiple_of(i * tq, tq), tq)

    def prep_q(i, slot):
        q = q_ref[rows_of(i), :].astype(F32)
        lane = lax.broadcasted_iota(jnp.int32, q.shape, 1)
        qT_ref[slot, :, 0:tq] = jnp.where(lane < DIFF_HEAD_DIM, q, 0.0).T.astype(BF16)
        qT_ref[slot, :, tq:2 * tq] = jnp.where(lane >= DIFF_HEAD_DIM, q, 0.0).T.astype(BF16)

    def finalize(i, o_t):
        o = o_t.T
        y = o[0:tq, :] - lam_ref[...] * o[tq:2 * tq, :]
        y = _rms(y) * sw_ref[...] * post_scale
        o_ref[rows_of(i), :] = y.astype(o_ref.dtype)

    _flash_tiles(prep_q, finalize, qT_ref, k_ref, vT_ref, kt_ref, vtT_ref, s0_ref, s1_ref, st_ref, m_ref, acc_ref,
                 nq=q_ref.shape[0] // tq, tk=tk, n_main=n_main, has_tail=has_tail)


def _diff_attn(qk, v, qk_tail, v_tail, lam_vec, subln_w, post_scale, batch, has_tail):
    t = qk.shape[1]
    tq = min(512, t)
    tk = min(512, t)
    nqb = _query_blocks_per_head(t, tq, 4)
    tqb = t // nqb
    kern = functools.partial(_diff_attn_kernel, tq=tq, tk=tk, n_main=t // tk, has_tail=has_tail,
                             post_scale=post_scale)
    tt = qk_tail.shape[1]
    return pl.pallas_call(
        kern,
        grid=(batch, DIFF_HEADS, nqb),
        in_specs=[pl.BlockSpec((None, tqb, LANES), lambda b, h, i: (b, i, h)),
                  pl.BlockSpec((None, t, LANES), lambda b, h, i: (b, 0, DIFF_HEADS + h)),
                  pl.BlockSpec((None, t, LANES), lambda b, h, i: (b, 0, h)),
                  pl.BlockSpec((None, tt, LANES), lambda b, h, i: (b, 0, DIFF_HEADS + h)),
                  pl.BlockSpec((None, tt, LANES), lambda b, h, i: (b, 0, h)),
                  pl.BlockSpec((1, LANES), lambda b, h, i: (0, 0)),
                  pl.BlockSpec((1, LANES), lambda b, h, i: (0, 0))],
        out_specs=pl.BlockSpec((None, tqb, LANES), lambda b, h, i: (b, i, h)),
        out_shape=jax.ShapeDtypeStruct((batch, t, DIFF_HEADS * LANES), BF16),
        scratch_shapes=_flash_scratch(2 * tq, tk, t, tt),
        compiler_params=_cparams(("parallel", "parallel", "arbitrary")),
        name="diff_attn",
    )(qk, qk, v, qk_tail, v_tail, lam_vec, subln_w)


def _gqa_kernel(q_ref, k_ref, v_ref, kt_ref, vt_ref, o_ref,
                qT_ref, vT_ref, vtT_ref, s0_ref, s1_ref, st_ref, m_ref, acc_ref, *, tq, tk, n_main, has_tail):
    @pl.when(pl.program_id(2) == 0)
    def _():
        _transpose_values(v_ref, vt_ref, vT_ref, vtT_ref, tk=tk)

    def rows_of(i):
        return pl.ds(pl.multiple_of(i * tq, tq), tq)

    def prep_q(i, slot):
        for r in range(GQA_REP):
            qT_ref[slot, :, r * tq:(r + 1) * tq] = _to_bf16_t(q_ref[rows_of(i), r * LANES:(r + 1) * LANES])

    def finalize(i, o_t):
        o = o_t.T
        for r in range(GQA_REP):
            o_ref[rows_of(i), r * LANES:(r + 1) * LANES] = o[r * tq:(r + 1) * tq, :].astype(o_ref.dtype)

    _flash_tiles(prep_q, finalize, qT_ref, k_ref, vT_ref, kt_ref, vtT_ref, s0_ref, s1_ref, st_ref, m_ref, acc_ref,
                 nq=q_ref.shape[0] // tq, tk=tk, n_main=n_main, has_tail=has_tail)


def _gqa_attn(qk, v, qk_tail, v_tail, batch, has_tail):
    t = qk.shape[1]
    tq = min(256, t)
    tk = min(512, t)
    gw = GQA_REP * LANES
    nqb = _query_blocks_per_head(t, tq, 8)
    tqb = t // nqb
    kern = functools.partial(_gqa_kernel, tq=tq, tk=tk, n_main=t // tk, has_tail=has_tail)
    tt = qk_tail.shape[1]
    return pl.pallas_call(
        kern,
        grid=(batch, GQA_KV_HEADS, nqb),
        in_specs=[pl.BlockSpec((None, tqb, gw), lambda b, g, i: (b, i, g)),
                  pl.BlockSpec((None, t, LANES), lambda b, g, i: (b, 0, GQA_HEADS + g)),
                  pl.BlockSpec((None, t, LANES), lambda b, g, i: (b, 0, g)),
                  pl.BlockSpec((None, tt, LANES), lambda b, g, i: (b, 0, GQA_HEADS + g)),
                  pl.BlockSpec((None, tt, LANES), lambda b, g, i: (b, 0, g))],
        out_specs=pl.BlockSpec((None, tqb, gw), lambda b, g, i: (b, i, g)),
        out_shape=jax.ShapeDtypeStruct((batch, t, GQA_HEADS * LANES), BF16),
        scratch_shapes=_flash_scratch(GQA_REP * tq, tk, t, tt),
        compiler_params=_cparams(("parallel", "parallel", "arbitrary")),
        name="gqa_attn",
    )(qk, qk, v, qk_tail, v_tail)


def _rglru_kernel(xf_ref, xfp_ref, xfn_ref, xb_ref, xbp_ref, xbn_ref, cw_ref, cb_ref, wg_ref, bg_ref, cv_ref,
                  h0_ref, hf_ref, hb_ref, ht_ref, a_scr, b_scr, st_scr, *, tb, nblk):
    i = pl.program_id(1)

    @pl.when(i == 0)
    def _():
        st_scr[...] = h0_ref[...]

    row = lax.broadcasted_iota(jnp.int32, (tb, LRU_WIDTH), 0)

    def gates(d, x_ref, xp_ref, xn_ref, blk):
        x = x_ref[...]
        prev = xp_ref[SUBLANES - 1:SUBLANES, :] * (blk > 0).astype(F32)
        has_next = (blk < nblk - 1).astype(F32)
        nxt0 = xn_ref[0:1, :] * has_next
        nxt1 = xn_ref[1:2, :] * has_next
        xm1 = jnp.where(row == 0, prev, pltpu.roll(x, 1, 0))
        xp1 = jnp.where(row == tb - 1, nxt0, pltpu.roll(x, tb - 1, 0))
        xp2 = jnp.where(row == tb - 2, nxt0, jnp.where(row == tb - 1, nxt1, pltpu.roll(x, tb - 2, 0)))
        y = xm1 * cw_ref[0:1, :] + x * cw_ref[1:2, :] + xp1 * cw_ref[2:3, :] + xp2 * cw_ref[3:4, :] + cb_ref[...]
        yb = y.astype(BF16)
        for c in range(LRU_BLOCKS):
            sl = slice(c * LRU_BLOCK, (c + 1) * LRU_BLOCK)
            z = _dot(yb[:, sl], wg_ref[d, c]) + bg_ref[d, c]
            r = jax.nn.sigmoid(z[:, 0:LRU_BLOCK])
            g = jax.nn.sigmoid(z[:, LRU_BLOCK:2 * LRU_BLOCK])
            log_a = r * cv_ref[d:d + 1, sl]
            a = jnp.exp(log_a)
            a_scr[d, :, sl] = a
            b_scr[d, :, sl] = jnp.sqrt(-jnp.tanh(log_a) * (1.0 + a * a)) * (g * y[:, sl])

    gates(0, xf_ref, xfp_ref, xfn_ref, i)
    gates(1, xb_ref, xbp_ref, xbn_ref, nblk - 1 - i)

    row8 = lax.broadcasted_iota(jnp.int32, (SUBLANES, LRU_WIDTH), 0)
    nt = tb // SUBLANES

    def scan(d, out_ref):
        rev = d == 1

        def body(r, h):
            off = pl.multiple_of((nt - 1 - r if rev else r) * SUBLANES, SUBLANES)
            a8 = a_scr[d, pl.ds(off, SUBLANES), :]
            b8 = b_scr[d, pl.ds(off, SUBLANES), :]
            for s in (1, 2, 4):
                if rev:
                    ok = row8 < SUBLANES - s
                    sh = SUBLANES - s
                else:
                    ok = row8 >= s
                    sh = s
                a_sh = jnp.where(ok, pltpu.roll(a8, sh, 0), 1.0)
                b_sh = jnp.where(ok, pltpu.roll(b8, sh, 0), 0.0)
                b8 = a8 * b_sh + b8
                a8 = a8 * a_sh
            h8 = a8 * h + b8
            out_ref[pl.ds(off, SUBLANES), :] = h8
            return h8[0:1, :] if rev else h8[SUBLANES - 1:SUBLANES, :]

        st_scr[d:d + 1, :] = lax.fori_loop(0, nt, body, st_scr[d:d + 1, :])

    scan(0, hf_ref)
    scan(1, hb_ref)

    @pl.when(i == nblk - 1)
    def _():
        ht_ref[...] = st_scr[...]


def _rglru(gx, seq, batch, conv_w, conv_b, wg, bg, cv, h0):
    m = gx.shape[0]
    tb = min(512, seq)
    nblk = seq // tb
    hb8 = tb // SUBLANES
    last8 = m // SUBLANES - 1
    w = LRU_WIDTH

    def fidx(b, i):
        return b * nblk + i

    def bidx(b, i):
        return b * nblk + nblk - 1 - i

    def specs(idx):
        return [pl.BlockSpec((tb, w), lambda b, i: (idx(b, i), 1)),
                pl.BlockSpec((SUBLANES, w), lambda b, i: (jnp.maximum(idx(b, i) * hb8 - 1, 0), 1)),
                pl.BlockSpec((SUBLANES, w), lambda b, i: (jnp.minimum((idx(b, i) + 1) * hb8, last8), 1))]

    full = lambda shape: pl.BlockSpec(shape, lambda b, i: (0,) * len(shape))
    kern = functools.partial(_rglru_kernel, tb=tb, nblk=nblk)
    return pl.pallas_call(
        kern,
        grid=(batch, nblk),
        in_specs=specs(fidx) + specs(bidx) + [full(conv_w.shape), full((1, w)), full(wg.shape), full(bg.shape),
                                              full(cv.shape), pl.BlockSpec((None, 2, w), lambda b, i: (b, 0, 0))],
        out_specs=[pl.BlockSpec((tb, w), lambda b, i: (fidx(b, i), 0)),
                   pl.BlockSpec((tb, w), lambda b, i: (bidx(b, i), 0)),
                   pl.BlockSpec((None, 2, w), lambda b, i: (b, 0, 0))],
        out_shape=[jax.ShapeDtypeStruct((m, w), F32), jax.ShapeDtypeStruct((m, w), F32),
                   jax.ShapeDtypeStruct((batch, 2, w), F32)],
        scratch_shapes=[pltpu.VMEM((2, tb, w), F32), pltpu.VMEM((2, tb, w), F32), pltpu.VMEM((2, w), F32)],
        compiler_params=_cparams(("parallel", "arbitrary")),
        name="rglru",
    )(gx, gx, gx, gx, gx, gx, conv_w, conv_b.reshape(1, w), wg, bg, cv, h0)


def _hgrn_consts(c):
    t = np.arange(c)
    blocks = [(t[None, :] <= t[:, None]).astype(np.float32)]
    masks = []
    m = c // 2
    while m >= 1:
        mid = (t // (2 * m)) * (2 * m) + m
        right = t >= mid
        if 2 * m < SUBLANES:
            u = t[None, :]
            g = np.where(right[:, None], (u >= mid[:, None]) & (u <= t[:, None]),
                         (u > t[:, None]) & (u < mid[:, None]))
            blocks.append(g.astype(np.float32))
        same = (t[:, None] // (2 * m)) == (t[None, :] // (2 * m))
        masks.append((same & right[:, None] & (~right)[None, :]).astype(np.float32))
        m //= 2
    masks.append(np.eye(c, dtype=np.float32))
    flip = lambda a: a[::-1, ::-1]
    ones = np.ones((BF16_SUBLANES, c), np.float32)
    w = np.stack([np.concatenate(blocks + [ones], 0), np.concatenate([flip(b) for b in blocks] + [ones], 0)])
    cm = np.stack([np.stack(masks), np.stack([flip(a) for a in masks])])
    return w, cm


def _hgrn_kernel(qf_ref, ff_ref, vf_ref, qb_ref, fb_ref, vb_ref, lb_ref, wc_ref, cm_ref, s0_ref,
                 of_ref, ob_ref, st_ref, st_scr, *, c, levels, nchunk):
    i = pl.program_id(1)

    @pl.when(i == 0)
    def _():
        st_scr[...] = s0_ref[...]

    def prep(d, q_ref, f_ref, v_ref):
        q = q_ref[...]
        q = q * jax.nn.sigmoid(q)
        lb = lb_ref[d:d + 1, :]
        f = lb + (1.0 - lb) * jax.nn.sigmoid(f_ref[...])
        kk = 1.0 - f
        g = jnp.log(f)
        g1 = g.astype(BF16)
        g2 = (g - g1.astype(F32)).astype(BF16)
        w = wc_ref[d]
        x = _dot(w, g1) + _dot(w, g2)
        cum = x[0:c]
        n_small = (w.shape[0] - BF16_SUBLANES) // c - 1
        tot = x[(1 + n_small) * c:(1 + n_small) * c + 1]
        e_lev = []
        m, small = c // 2, 0
        while m >= 1:
            if 2 * m >= SUBLANES:
                xr = cum.reshape(c // (2 * m), 2 * m, HGRN_WIDTH)
                ref = xr[:, m - 1 + d:m + d, :]
                e_lev.append(jnp.exp(-jnp.abs(xr - ref)).reshape(c, HGRN_WIDTH))
            else:
                e_lev.append(jnp.exp(x[(1 + small) * c:(2 + small) * c]))
                small += 1
            m //= 2
        return q, kk, v_ref[...].astype(BF16), jnp.exp(cum), jnp.exp(tot - cum), jnp.exp(tot), e_lev

    def head(d, h, q, kk, v, e_in, e_out, e_tot, e_lev, o_ref):
        sl = slice(h * LANES, (h + 1) * LANES)
        st = st_scr[d, h]
        qh, kh, vh = q[:, sl], kk[:, sl], v[:, sl]
        o = _dot_nt((qh * e_in[:, sl]).astype(BF16), st.astype(BF16))
        sc = cm_ref[d, levels] * _dot_nt(qh.astype(BF16), kh.astype(BF16))
        for l in range(levels):
            el = e_lev[l][:, sl]
            sc = sc + cm_ref[d, l] * _dot_nt((qh * el).astype(BF16), (kh * el).astype(BF16))
        o_ref[:, sl] = o + _dot(sc.astype(BF16), vh)
        st_scr[d, h] = st * e_tot[:, sl] + _dot_tn(vh, (kh * e_out[:, sl]).astype(BF16))

    fwd = prep(0, qf_ref, ff_ref, vf_ref)
    bwd = prep(1, qb_ref, fb_ref, vb_ref)
    for h in range(HGRN_HEADS):
        head(0, h, *fwd, of_ref)
        head(1, h, *bwd, ob_ref)

    @pl.when(i == nchunk - 1)
    def _():
        st_ref[...] = st_scr[...]


def _hgrn(z, seq, batch, lb, s0):
    m = z.shape[0]
    c = min(HGRN_CHUNK, seq)
    nchunk = seq // c
    levels = int(math.log2(c))
    wnp, cmnp = _hgrn_consts(c)
    wc = jnp.asarray(wnp, BF16)
    cm = jnp.asarray(cmnp, F32)
    w = HGRN_WIDTH

    def fidx(b, i):
        return b * nchunk + i

    def bidx(b, i):
        return b * nchunk + nchunk - 1 - i

    blk = lambda idx, col: pl.BlockSpec((c, w), lambda b, i: (idx(b, i), col))
    full = lambda shape: pl.BlockSpec(shape, lambda b, i: (0,) * len(shape))
    st_spec = pl.BlockSpec((None, 2, HGRN_HEADS, LANES, LANES), lambda b, i: (b, 0, 0, 0, 0))
    kern = functools.partial(_hgrn_kernel, c=c, levels=levels, nchunk=nchunk)
    return pl.pallas_call(
        kern,
        grid=(batch, nchunk),
        in_specs=[blk(fidx, 0), blk(fidx, 1), blk(fidx, 3), blk(bidx, 0), blk(bidx, 2), blk(bidx, 3),
                  full(lb.shape), full(wc.shape), full(cm.shape), st_spec],
        out_specs=[pl.BlockSpec((c, w), lambda b, i: (fidx(b, i), 0)),
                   pl.BlockSpec((c, w), lambda b, i: (bidx(b, i), 0)), st_spec],
        out_shape=[jax.ShapeDtypeStruct((m, w), F32), jax.ShapeDtypeStruct((m, w), F32),
                   jax.ShapeDtypeStruct((batch, 2, HGRN_HEADS, LANES, LANES), F32)],
        scratch_shapes=[pltpu.VMEM((2, HGRN_HEADS, LANES, LANES), F32)],
        compiler_params=_cparams(("parallel", "arbitrary")),
        name="hgrn2",
    )(z, z, z, z, z, z, lb, wc, cm, s0)


def _outproj_kernel(x_ref, p0_ref, p1_ref, g_ref, att_ref, nw_ref, w_ref, gt_ref, o_ref, *, mode):
    half = w_ref.shape[0] // 2
    s = p0_ref[...] + p1_ref[...]
    g = g_ref[...]
    if mode == "ab":
        a = s * jax.nn.gelu(g, approximate=True)
    else:
        parts = []
        for h in range(HGRN_HEADS):
            sl = slice(h * LANES, (h + 1) * LANES)
            parts.append(_rms(s[:, sl]) * nw_ref[...])
        a = jnp.concatenate(parts, axis=-1) * (g * jax.nn.sigmoid(g))
    acc = _dot(a.astype(BF16), w_ref[0:half, :]) + _dot(att_ref[...], w_ref[half:2 * half, :])
    o_ref[...] = x_ref[...] + gt_ref[...] * acc


def _outproj(x2d, seq, p0, p1, gsrc, gcol, att, head_norm_w, w_out, mod, mode):
    m, d = x2d.shape
    tm = min(512, seq)
    tpb = seq // tm if mod.shape[0] > 1 else m
    hw = w_out.shape[0] // 2
    kern = functools.partial(_outproj_kernel, mode=mode)
    return pl.pallas_call(
        kern,
        grid=(m // tm,),
        in_specs=[pl.BlockSpec((tm, d), lambda i: (i, 0)),
                  pl.BlockSpec((tm, hw), lambda i: (i, 0)),
                  pl.BlockSpec((tm, hw), lambda i: (i, 0)),
                  pl.BlockSpec((tm, hw), lambda i: (i, gcol)),
                  pl.BlockSpec((tm, hw), lambda i: (i, 0)),
                  pl.BlockSpec((1, LANES), lambda i: (0, 0)),
                  pl.BlockSpec(w_out.shape, lambda i: (0, 0)),
                  pl.BlockSpec((None, None, 1, d), lambda i: (i // tpb, 2, 0, 0))],
        out_specs=pl.BlockSpec((tm, d), lambda i: (i, 0)),
        out_shape=jax.ShapeDtypeStruct((m, d), F32),
        compiler_params=_cparams(("parallel",)),
        name="outproj_" + mode,
    )(x2d, p0, p1, gsrc, att, head_norm_w, w_out, mod)


def _ffn_kernel(x_ref, nw_ref, sh_ref, sc_ref, gt_ref, wg_ref, wu_ref, wd_ref, fw_ref, o_ref, hn_ref, *, final):
    j = pl.program_id(1)

    @pl.when(j == 0)
    def _():
        h = _rms(x_ref[...]) * nw_ref[...]
        hn_ref[...] = (h * (1.0 + sc_ref[...]) + sh_ref[...]).astype(BF16)
        o_ref[...] = jnp.zeros(o_ref.shape, F32)

    hn = hn_ref[...]
    half = wg_ref.shape[1] // 2
    acc = None
    for k in range(2):
        sl = slice(k * half, (k + 1) * half)
        g = _dot(hn, wg_ref[:, sl])
        u = _dot(hn, wu_ref[:, sl])
        a = (g * jax.nn.sigmoid(g) * u).astype(BF16)
        part = _dot(a, wd_ref[sl, :])
        acc = part if acc is None else acc + part
    o_ref[...] += acc

    @pl.when(j == pl.num_programs(1) - 1)
    def _():
        y = x_ref[...] + gt_ref[...] * o_ref[...]
        if final:
            y = _rms(y) * fw_ref[...]
        o_ref[...] = y


def _ffn(x2d, seq, norm_w, mod, w_gate, w_up, w_down, final_w, final):
    m, d = x2d.shape
    f = w_gate.shape[1]
    tm = min(512, seq)
    tf = 512
    tpb = seq // tm if mod.shape[0] > 1 else m
    mspec = lambda k: pl.BlockSpec((None, None, 1, d), lambda i, j: (i // tpb, k, 0, 0))
    kern = functools.partial(_ffn_kernel, final=final)
    return pl.pallas_call(
        kern,
        grid=(m // tm, f // tf),
        in_specs=[pl.BlockSpec((tm, d), lambda i, j: (i, 0)),
                  pl.BlockSpec((1, d), lambda i, j: (0, 0)),
                  mspec(3), mspec(4), mspec(5),
                  pl.BlockSpec((d, tf), lambda i, j: (0, j)),
                  pl.BlockSpec((d, tf), lambda i, j: (0, j)),
                  pl.BlockSpec((tf, d), lambda i, j: (j, 0)),
                  pl.BlockSpec((1, d), lambda i, j: (0, 0))],
        out_specs=pl.BlockSpec((tm, d), lambda i, j: (i, 0)),
        out_shape=jax.ShapeDtypeStruct((m, d), F32),
        scratch_shapes=[pltpu.VMEM((tm, d), BF16)],
        compiler_params=_cparams(("parallel", "arbitrary")),
        name="ffn",
    )(x2d, norm_w.reshape(1, d), mod, mod, mod, w_gate, w_up, w_down, final_w.reshape(1, d))


def _rope_tables(rows, head_dim):
    n_freq = head_dim // 4
    half = head_dim // 2
    row = jnp.repeat(jnp.arange(rows, dtype=F32), GRID_W)
    col = jnp.tile(jnp.arange(GRID_W, dtype=F32), rows)
    inv = ROPE_THETA ** (-jnp.arange(n_freq, dtype=F32) / n_freq)
    ang = jnp.concatenate([row[:, None] * inv, col[:, None] * inv], axis=-1)
    cos, sin = jnp.cos(ang), jnp.sin(ang)
    reps = LANES // head_dim
    zero = jnp.zeros_like(sin)
    cos_t = jnp.tile(jnp.concatenate([cos, cos], -1), (1, reps))
    if half * 2 == LANES:
        return cos_t, jnp.concatenate([-sin, sin], -1), None
    sin_a = jnp.tile(jnp.concatenate([-sin, zero], -1), (1, reps))
    sin_b = jnp.tile(jnp.concatenate([zero, sin], -1), (1, reps))
    return cos_t, sin_a, sin_b


def _identity_rope(t, head_dim):
    one = jnp.ones((t, LANES), F32)
    zero = jnp.zeros((t, LANES), F32)
    return (one, zero, None) if head_dim == LANES else (one, zero, zero)


def kernel(x, c, ctx, c_ctx, mod_w, mod_b, norm_mix_w, norm_ffn_w, ffn_w_gate, ffn_w_up, ffn_w_down, ab_w_in, ab_w_out, lru_conv_w, lru_conv_b, lru_wa, lru_ba, lru_wx, lru_bx, lru_lambda, diff_lq1, diff_lk1, diff_lq2, diff_lk2, diff_subln_w, cd_w_in, cd_w_out, hgrn_lb_logits, hgrn_norm_w, gqa_q_norm_w, gqa_k_norm_w, final_norm_w):
    batch, seq, d = x.shape
    clen = ctx.shape[1]
    depth = mod_w.shape[0]
    rows = seq // GRID_W

    cc = jnp.zeros((SUBLANES, d), F32).at[0:batch].set(c).at[batch].set(c_ctx)
    mods = _modulation(cc, mod_w, mod_b)
    lb_cum = jnp.cumsum(jax.nn.softmax(hgrn_lb_logits.astype(F32), axis=1), axis=1)

    xl = x.reshape(batch * seq, d)
    xc = ctx.reshape(batch * clen, d)

    for l in range(depth):
        last = l == depth - 1
        m_lat = mods[l, 0:batch].reshape(batch, N_MOD, 1, d)
        m_ctx = mods[l, batch:batch + 1].reshape(1, N_MOD, 1, d)
        streams = ((_normmod(xl, seq, norm_mix_w[l], m_lat), seq), (_normmod(xc, clen, norm_mix_w[l], m_ctx), clen))
        if l % 2 == 0:
            e = l // 2
            lambda_init = 0.8 - 0.6 * math.exp(-0.3 * l)
            w_in = ab_w_in[e].astype(BF16)
            qscale = DIFF_HEAD_DIM ** -0.5 * LOG2E
            ropes = (_rope_tables(rows, DIFF_HEAD_DIM), _identity_rope(clen, DIFF_HEAD_DIM))
            proj = []
            for (hn, t), rp in zip(streams, ropes):
                rp = [jnp.stack([tab * qscale, tab]) for tab in rp]
                gx = _inproj(hn, t, w_in, 0, 2048, F32, 1024)
                qk = _inproj(hn, t, w_in, 2048, 2048, BF16, 1024, rope=rp, rope_half=DIFF_HEAD_DIM // 2)
                v = _inproj(hn, t, w_in, 4096, 1024, BF16, 1024)
                proj.append((gx, qk.reshape(batch, t, 2048), v.reshape(batch, t, 1024)))
            (gx_l, qk_l, v_l), (gx_c, qk_c, v_c) = proj
            wg = jnp.concatenate([lru_wa[e], lru_wx[e]], axis=-1).astype(BF16)
            bg = jnp.concatenate([lru_ba[e].reshape(2, LRU_BLOCKS, 1, LRU_BLOCK),
                                  lru_bx[e].reshape(2, LRU_BLOCKS, 1, LRU_BLOCK)], axis=-1)
            cv = -LRU_C * jax.nn.softplus(-lru_lambda[e].astype(F32))
            h0 = jnp.zeros((batch, 2, LRU_WIDTH), F32)
            hf_c, hb_c, h_ctx = _rglru(gx_c, clen, batch, lru_conv_w[e], lru_conv_b[e], wg, bg, cv, h0)
            hf_l, hb_l, _ = _rglru(gx_l, seq, batch, lru_conv_w[e], lru_conv_b[e], wg, bg, cv, h_ctx)
            lam = (jnp.exp(jnp.sum(diff_lq1[e].astype(F32) * diff_lk1[e].astype(F32)))
                   - jnp.exp(jnp.sum(diff_lq2[e].astype(F32) * diff_lk2[e].astype(F32))) + lambda_init)
            lam_vec = jnp.full((1, LANES), lam, F32)
            sw = diff_subln_w[e].reshape(1, LANES)
            d_l = _diff_attn(qk_l, v_l, qk_c, v_c, lam_vec, sw, 1.0 - lambda_init, batch, True)
            w_out = ab_w_out[e].astype(BF16)
            dummy_nw = jnp.ones((1, LANES), F32)
            xl = _outproj(xl, seq, hf_l, hb_l, gx_l, 0, d_l.reshape(batch * seq, 1024), dummy_nw, w_out, m_lat, "ab")
            if not last:
                d_c = _diff_attn(qk_c, v_c, qk_c, v_c, lam_vec, sw, 1.0 - lambda_init, batch, False)
                xc = _outproj(xc, clen, hf_c, hb_c, gx_c, 0, d_c.reshape(batch * clen, 1024), dummy_nw, w_out,
                              m_ctx, "ab")
        else:
            o = l // 2
            lb = lb_cum[:, l] - lb_cum[:, 0]
            w_in = cd_w_in[o].astype(BF16)
            qscale = GQA_HEAD_DIM ** -0.5 * LOG2E
            chunk_w = jnp.concatenate([jnp.tile(gqa_q_norm_w[o] * qscale, GQA_HEADS),
                                       jnp.tile(gqa_k_norm_w[o], GQA_KV_HEADS)]).reshape(1, 1280)
            ropes = (_rope_tables(rows, GQA_HEAD_DIM), _identity_rope(clen, GQA_HEAD_DIM))
            proj = []
            for (hn, t), rp in zip(streams, ropes):
                z = _inproj(hn, t, w_in, 0, 5120, F32, 1024)
                qk = _inproj(hn, t, w_in, 5120, 1280, BF16, 1280, chunk_w=chunk_w, norm_chunks=10,
                             rope=rp[0:2], rope_half=GQA_HEAD_DIM // 2)
                v = _inproj(hn, t, w_in, 6400, 256, BF16, 256)
                proj.append((z, qk.reshape(batch, t, 1280), v.reshape(batch, t, 256)))
            (z_l, qk_l, v_l), (z_c, qk_c, v_c) = proj
            s0 = jnp.zeros((batch, 2, HGRN_HEADS, LANES, LANES), F32)
            of_c, ob_c, s_ctx = _hgrn(z_c, clen, batch, lb, s0)
            of_l, ob_l, _ = _hgrn(z_l, seq, batch, lb, s_ctx)
            att_l = _gqa_attn(qk_l, v_l, qk_c, v_c, batch, True)
            w_out = cd_w_out[o].astype(BF16)
            hnw = hgrn_norm_w[o].reshape(1, LANES)
            xl = _outproj(xl, seq, of_l, ob_l, z_l, 4, att_l.reshape(batch * seq, 1024), hnw, w_out, m_lat, "cd")
            if not last:
                att_c = _gqa_attn(qk_c, v_c, qk_c, v_c, batch, False)
                xc = _outproj(xc, clen, of_c, ob_c, z_c, 4, att_c.reshape(batch * clen, 1024), hnw, w_out,
                              m_ctx, "cd")
        wgt, wup, wdn = ffn_w_gate[l].astype(BF16), ffn_w_up[l].astype(BF16), ffn_w_down[l].astype(BF16)
        xl = _ffn(xl, seq, norm_ffn_w[l], m_lat, wgt, wup, wdn, final_norm_w, last)
        if not last:
            xc = _ffn(xc, clen, norm_ffn_w[l], m_ctx, wgt, wup, wdn, final_norm_w, False)

    return xl.reshape(batch, seq, d)
```
